```python
import math
import jax, jax.numpy as jnp
from jax import lax
import numpy as np

D_MODEL = 1024
BATCH = 8
SEQ = 2048
DEPTH = 4
DEC_BATCH = 4
DEC_SEQ = 8192
PAST_LEN = 128

D_MIX = D_MODEL
CONV_W = D_MIX // 2
CONV_K = 3
N_HEADS = 4
V_DIM = (D_MIX // 2) // N_HEADS
QK_DIM = V_DIM // 2
QK_W = N_HEADS * QK_DIM
V_W = N_HEADS * V_DIM
N_GATES = 4
CHUNK = 128
N_EXPERTS = 16
EXPERT_FF = 1024
CAPACITY_FACTOR = 2
EPS = 1e-6
SPLITS = (CONV_W, 2 * CONV_W, 3 * CONV_W,
          3 * CONV_W + QK_W, 3 * CONV_W + 2 * QK_W,
          3 * CONV_W + 2 * QK_W + V_W, 3 * CONV_W + 2 * QK_W + 2 * V_W)
D_IN = 3 * CONV_W + 2 * QK_W + 2 * V_W + N_GATES * N_HEADS

kernel_name = "hybrid_conv_mlstm_ec_moe_encoder"


def rms_norm(x, g):
    xf = x.astype(jnp.float32)
    y = xf * lax.rsqrt(jnp.mean(xf * xf, axis=-1, keepdims=True) + EPS)
    return (y * g.astype(jnp.float32)).astype(x.dtype)


def short_conv(u, w):
    up = jnp.pad(u, ((0, 0), (1, 1), (0, 0)))
    s = u.shape[1]
    return up[:, 0:s] * w[0] + up[:, 1:s + 1] * w[1] + up[:, 2:s + 2] * w[2]


def mlstm_scan(q, k, v, li, lf):
    b, s, h, dk = q.shape
    dv = v.shape[-1]
    nc = s // CHUNK

    def chunks(a):
        return jnp.moveaxis(a.reshape(b, nc, CHUNK, h, a.shape[-1]), (1, 3), (0, 2))

    def gchunks(a):
        return jnp.moveaxis(a.reshape(b, nc, CHUNK, h), (1, 3), (0, 2))

    tril = jnp.tril(jnp.ones((CHUNK, CHUNK), dtype=bool))

    def step(carry, xs):
        c_st, n_st, m_st = carry
        qc, kc, vc, lic, lfc = xs
        bcum = jnp.cumsum(lfc, axis=-1)
        g = bcum[..., -1]
        log_d = bcum[..., :, None] - bcum[..., None, :] + lic[..., None, :]
        log_d = jnp.where(tril, log_d, -jnp.inf)
        m_inter = bcum + m_st[..., None]
        m_t = jnp.maximum(m_inter, jnp.max(log_d, axis=-1))
        s_mat = jnp.einsum('bhtk,bhsk->bhts', qc, kc) * jnp.exp(log_d - m_t[..., None])
        scale_inter = jnp.exp(m_inter - m_t)
        num = jnp.einsum('bhts,bhsv->bhtv', s_mat, vc) + \
            scale_inter[..., None] * jnp.einsum('bhtk,bhkv->bhtv', qc, c_st)
        den = jnp.sum(s_mat, axis=-1) + scale_inter * jnp.einsum('bhtk,bhk->bht', qc, n_st)
        h_out = num / jnp.maximum(jnp.abs(den), jnp.exp(-m_t))[..., None]
        log_w = g[..., None] - bcum + lic
        m_new = jnp.maximum(g + m_st, jnp.max(log_w, axis=-1))
        w = jnp.exp(log_w - m_new[..., None])
        decay = jnp.exp(g + m_st - m_new)
        c_new = decay[..., None, None] * c_st + jnp.einsum('bhs,bhsk,bhsv->bhkv', w, kc, vc)
        n_new = decay[..., None] * n_st + jnp.einsum('bhs,bhsk->bhk', w, kc)
        return (c_new, n_new, m_new), h_out

    init = (jnp.zeros((b, h, dk, dv), jnp.float32),
            jnp.zeros((b, h, dk), jnp.float32),
            jnp.zeros((b, h), jnp.float32))
    _, hs = lax.scan(step, init, (chunks(q), chunks(k), chunks(v), gchunks(li), gchunks(lf)))
    return jnp.moveaxis(hs, (0, 2), (1, 3)).reshape(b, s, h, dv)


def mixer_sublayer(x, norm_g, w_in, conv_w, gate_bias, head_norm_g, w_out):
    bsz, s, _ = x.shape
    hn = rms_norm(x, norm_g)
    proj = jnp.einsum('bsd,de->bse', hn, w_in)
    gb, gc, u, q, k, v, o, gates = jnp.split(proj, SPLITS, axis=-1)
    conv_out = gb * short_conv(gc * u, conv_w)
    q = q.astype(jnp.float32).reshape(bsz, s, N_HEADS, QK_DIM) * (QK_DIM ** -0.5)
    k = k.astype(jnp.float32).reshape(bsz, s, N_HEADS, QK_DIM)
    v = v.astype(jnp.float32).reshape(bsz, s, N_HEADS, V_DIM)
    gates = gates.astype(jnp.float32).reshape(bsz, s, N_GATES, N_HEADS) + gate_bias.astype(jnp.float32)
    li_f, lf_f = gates[:, :, 0], jax.nn.log_sigmoid(gates[:, :, 1])
    li_b, lf_b = gates[:, :, 2], jax.nn.log_sigmoid(gates[:, :, 3])
    h_fwd = mlstm_scan(q, k, v, li_f, lf_f)
    flip = lambda a: jnp.flip(a, axis=1)
    h_bwd = flip(mlstm_scan(flip(q), flip(k), flip(v), flip(li_b), flip(lf_b)))
    ht = h_fwd + h_bwd
    ht = ht * lax.rsqrt(jnp.mean(ht * ht, axis=-1, keepdims=True) + EPS)
    ht = ht * head_norm_g.astype(jnp.float32).reshape(N_HEADS, V_DIM)
    mlstm_out = jax.nn.sigmoid(o) * ht.reshape(bsz, s, V_W).astype(x.dtype)
    mix = jnp.concatenate([conv_out, mlstm_out], axis=-1)
    return x + jnp.einsum('bse,ed->bsd', mix, w_out)


def expert_choice_sublayer(x, norm_g, w_router, w_gate, w_up, w_down):
    bsz, s, d = x.shape
    n_tok = bsz * s
    cap = CAPACITY_FACTOR * n_tok // N_EXPERTS
    tokens = rms_norm(x, norm_g).reshape(n_tok, d)
    logits = jnp.einsum('nd,de->ne', tokens.astype(jnp.float32), w_router.astype(jnp.float32))
    affinity = jax.nn.softmax(logits, axis=-1)
    gate, idx = lax.top_k(affinity.T, cap)
    xe = tokens[idx]
    hid = jax.nn.silu(jnp.einsum('ecd,edf->ecf', xe, w_gate)) * jnp.einsum('ecd,edf->ecf', xe, w_up)
    ye = jnp.einsum('ecf,efd->ecd', hid, w_down) * gate[..., None].astype(x.dtype)
    out = jnp.zeros((n_tok, d), x.dtype).at[idx.reshape(-1)].add(ye.reshape(-1, d))
    return x + out.reshape(bsz, s, d)


def setup_inputs(seed: int = 0) -> dict:
    key = jax.random.key(seed)
    ks = jax.random.split(key, 16)
    f32 = jnp.float32
    x_prompt = jax.random.normal(ks[0], (BATCH, SEQ, D_MODEL), f32)
    x_sample = jax.random.normal(ks[1], (DEC_BATCH, DEC_SEQ, D_MODEL), f32)
    norm1_g = 1.0 + 0.05 * jax.random.normal(ks[2], (DEPTH, D_MODEL), f32)
    w_in = jax.random.normal(ks[3], (DEPTH, D_MODEL, D_IN), f32) * D_MODEL ** -0.5
    conv_w = jax.random.normal(ks[4], (DEPTH, CONV_K, CONV_W), f32) * CONV_K ** -0.5
    i_bias = 0.1 * jax.random.normal(ks[5], (DEPTH, 2, N_HEADS), f32)
    f_bias = jnp.linspace(3.0, 6.0, N_HEADS, dtype=f32)[None, None, :] + \
        0.1 * jax.random.normal(ks[6], (DEPTH, 2, N_HEADS), f32)
    gate_bias = jnp.stack([i_bias[:, 0], f_bias[:, 0], i_bias[:, 1], f_bias[:, 1]], axis=1)
    head_norm_g = 1.0 + 0.05 * jax.random.normal(ks[7], (DEPTH, V_W), f32)
    w_out = jax.random.normal(ks[8], (DEPTH, D_MIX, D_MODEL), f32) * D_MIX ** -0.5
    norm2_g = 1.0 + 0.05 * jax.random.normal(ks[9], (DEPTH, D_MODEL), f32)
    w_router = jax.random.normal(ks[10], (DEPTH, D_MODEL, N_EXPERTS), f32) * D_MODEL ** -0.5
    w_gate = jax.random.normal(ks[11], (DEPTH, N_EXPERTS, D_MODEL, EXPERT_FF), f32) * D_MODEL ** -0.5
    w_up = jax.random.normal(ks[12], (DEPTH, N_EXPERTS, D_MODEL, EXPERT_FF), f32) * D_MODEL ** -0.5
    w_down = jax.random.normal(ks[13], (DEPTH, N_EXPERTS, EXPERT_FF, D_MODEL), f32) * EXPERT_FF ** -0.5
    final_g = 1.0 + 0.05 * jax.random.normal(ks[14], (D_MODEL,), f32)
    return {"x_prompt": x_prompt, "x_sample": x_sample, "norm1_g": norm1_g, "w_in": w_in,
            "conv_w": conv_w, "gate_bias": gate_bias, "head_norm_g": head_norm_g, "w_out": w_out,
            "norm2_g": norm2_g, "w_router": w_router, "w_gate": w_gate, "w_up": w_up,
            "w_down": w_down, "final_g": final_g}


def reference(x_prompt, x_sample, norm1_g, w_in, conv_w, gate_bias, head_norm_g, w_out,
              norm2_g, w_router, w_gate, w_up, w_down, final_g):
    xp, xs = x_prompt, x_sample
    for l in range(DEPTH):
        xp = mixer_sublayer(xp, norm1_g[l], w_in[l], conv_w[l], gate_bias[l], head_norm_g[l], w_out[l])
        xs = mixer_sublayer(xs, norm1_g[l], w_in[l], conv_w[l], gate_bias[l], head_norm_g[l], w_out[l])
        xp = expert_choice_sublayer(xp, norm2_g[l], w_router[l], w_gate[l], w_up[l], w_down[l])
        xs = expert_choice_sublayer(xs, norm2_g[l], w_router[l], w_gate[l], w_up[l], w_down[l])
    y_prompt = rms_norm(xp, final_g)
    y_sample = rms_norm(xs, final_g)
    return (y_prompt, y_sample)
```

```python
import functools

import jax
import jax.numpy as jnp
from jax import lax
from jax.experimental import pallas as pl
from jax.experimental.pallas import tpu as pltpu

F32 = jnp.float32
BF16 = jnp.bfloat16
I32 = jnp.int32

LANES = 128
N_EXPERTS = 16
CAPACITY_DIV = 8
TOKEN_BITS = 16
VALID_BIT = 24


def _threshold_kernel(aff_ref, thr_ref, need_ref, *, cap):
    bits = pltpu.bitcast(aff_ref[...], I32)
    cap_f = jnp.float32(cap)

    def count_ge(cand):
        return jnp.sum(jnp.where(bits >= cand, 1.0, 0.0), axis=1, keepdims=True)

    def body(i, thr):
        cand = thr | jnp.left_shift(jnp.int32(1), 30 - i)
        return jnp.where(count_ge(cand) >= cap_f, cand, thr)

    thr = lax.fori_loop(0, 31, body, jnp.zeros((N_EXPERTS, 1), I32))
    n_gt = jnp.sum(jnp.where(bits > thr, 1.0, 0.0), axis=1, keepdims=True)
    need = (cap_f - n_gt).astype(I32)
    thr_ref[...] = jnp.broadcast_to(thr, thr_ref.shape)
    need_ref[...] = jnp.broadcast_to(need, need_ref.shape)


def _lane_inclusive_scan(x, lane):
    for b in range(7):
        s = 1 << b
        x = x + jnp.where(lane >= s, pltpu.roll(x, s, axis=1), 0.0)
    return x


def _row_exclusive_scan(t, row, n_rows):
    inc = t
    s = 1
    while s < n_rows:
        inc = inc + jnp.where(row >= s, pltpu.roll(inc, s, axis=0), 0.0)
        s *= 2
    return inc - t


def _token_exclusive_scan(x, lane, row, n_rows):
    inc = _lane_inclusive_scan(x, lane)
    tot = jnp.broadcast_to(inc[:, LANES - 1:LANES], x.shape)
    return inc - x + _row_exclusive_scan(tot, row, n_rows)


def _compact_kernel(thr_ref, need_ref, aff_ref, packed_ref, rank_ref, cnt_ref, acc_ref, *, n_rows, cap_rows):
    e = pl.program_id(0)

    @pl.when(e == 0)
    def _():
        acc_ref[...] = jnp.zeros_like(acc_ref)

    shape = (n_rows, LANES)
    lane = lax.broadcasted_iota(I32, shape, 1)
    row = lax.broadcasted_iota(I32, shape, 0)
    bits = pltpu.bitcast(aff_ref[0], I32)
    thr = thr_ref[e]
    need = need_ref[e].astype(F32)
    eq = bits == thr
    pre_eq = _token_exclusive_scan(jnp.where(eq, 1.0, 0.0), lane, row, n_rows)
    sel = (bits > thr) | (eq & (pre_eq < need))
    sel_f = jnp.where(sel, 1.0, 0.0)
    rank = _token_exclusive_scan(sel_f, lane, row, n_rows).astype(I32)
    rank_ref[0] = rank
    kk = acc_ref[...]
    acc_ref[...] = kk + jnp.where(sel, 1, 0)
    cnt_ref[...] = acc_ref[...]

    pos = row * LANES + lane
    disp = pos - rank
    v = jnp.where(sel, disp | (kk << TOKEN_BITS) | (1 << VALID_BIT), 0)
    n_bits = (n_rows * LANES - 1).bit_length()
    for b in range(n_bits):
        if b < 7:
            s = 1 << b
            r1 = pltpu.roll(v, LANES - s, axis=1)
            r2 = pltpu.roll(r1, n_rows - 1, axis=0)
            moved = jnp.where(lane < LANES - s, r1, r2)
        else:
            sr = 1 << (b - 7)
            moved = pltpu.roll(v, n_rows - sr, axis=0)
        take = ((moved >> VALID_BIT) & 1 == 1) & ((moved >> b) & 1 == 1)
        stay = ((v >> VALID_BIT) & 1 == 1) & ((v >> b) & 1 == 0)
        v = jnp.where(take, moved, jnp.where(stay, v, 0))
    tok = pos + (v & ((1 << (TOKEN_BITS - 1)) - 1))
    out = tok | (v & (0xF << TOKEN_BITS))
    packed_ref[0] = out[:cap_rows]


def route(aff_t):
    n_exp, n = aff_t.shape
    cap = n // CAPACITY_DIV
    n_rows = n // LANES
    cap_rows = cap // LANES
    thr, need = pl.pallas_call(
        functools.partial(_threshold_kernel, cap=cap),
        out_shape=(jax.ShapeDtypeStruct((n_exp, LANES), I32), jax.ShapeDtypeStruct((n_exp, LANES), I32)),
        name="route_threshold",
    )(aff_t)
    grid_spec = pltpu.PrefetchScalarGridSpec(
        num_scalar_prefetch=2,
        grid=(n_exp,),
        in_specs=[pl.BlockSpec((1, n_rows, LANES), lambda e, *_: (e, 0, 0))],
        out_specs=[
            pl.BlockSpec((1, cap_rows, LANES), lambda e, *_: (e, 0, 0)),
            pl.BlockSpec((1, n_rows, LANES), lambda e, *_: (e, 0, 0)),
            pl.BlockSpec((n_rows, LANES), lambda e, *_: (0, 0)),
        ],
        scratch_shapes=[pltpu.VMEM((n_rows, LANES), I32)],
    )
    packed, rank, cnt = pl.pallas_call(
        functools.partial(_compact_kernel, n_rows=n_rows, cap_rows=cap_rows),
        grid_spec=grid_spec,
        out_shape=(
            jax.ShapeDtypeStruct((n_exp, cap_rows, LANES), I32),
            jax.ShapeDtypeStruct((n_exp, n_rows, LANES), I32),
            jax.ShapeDtypeStruct((n_rows, LANES), I32),
        ),
        compiler_params=pltpu.CompilerParams(dimension_semantics=("arbitrary",)),
        name="route_compact",
    )(thr[:, 0], need[:, 0], aff_t.reshape(n_exp, n_rows, LANES))
    return packed.reshape(n_exp, cap), rank.reshape(n_exp, n), cnt.reshape(n)


D_MODEL = 1024
CONV_W = 512
N_HEADS = 4
V_DIM = 128
QK_DIM = 64
QK_W = N_HEADS * QK_DIM
V_W = N_HEADS * V_DIM
CHUNK = 128
EXPERT_FF = 1024
EPS = 1e-6
C_GB, C_GC, C_U, C_Q, C_K, C_V, C_O, C_G = 0, 512, 1024, 1536, 1792, 2048, 2560, 3072
D_IN = 3088
D_IN_PAD = 3200
TOK_EXT = D_MODEL + LANES

ROW_TILE = 512
MLSTM_BLOCK = 256
SLOT_TILE = 256
COMBINE_TILE = 256
VMEM_LIMIT = 56 * 1024 * 1024


def _cparams(*sem):
    return pltpu.CompilerParams(dimension_semantics=sem, vmem_limit_bytes=VMEM_LIMIT)


def _rms(x, g):
    return x * lax.rsqrt(jnp.mean(x * x, axis=-1, keepdims=True) + EPS) * g


def _inproj_kernel(x_ref, g_ref, w_ref, bias_ref, gb_ref, gcu_ref, q_ref, k_ref, v_ref, os_ref, gcol_ref, grow_ref):
    hn = _rms(x_ref[...], g_ref[...]).astype(BF16)

    def seg(a, b):
        return jnp.dot(hn, w_ref[:, a:b], preferred_element_type=F32)

    gb_ref[...] = seg(C_GB, C_GC).astype(BF16)
    gcu_ref[...] = (seg(C_GC, C_U) * seg(C_U, C_Q)).astype(BF16)
    q_ref[...] = (seg(C_Q, C_K) * (QK_DIM ** -0.5)).astype(BF16)
    k_ref[...] = seg(C_K, C_V).astype(BF16)
    v_ref[...] = seg(C_V, C_O).astype(BF16)
    os_ref[...] = jax.nn.sigmoid(seg(C_O, C_G)).astype(BF16)
    gates = seg(C_G, D_IN_PAD) + bias_ref[...]
    lane = lax.broadcasted_iota(I32, gates.shape, 1)
    log_sig = jnp.minimum(gates, 0.0) - jnp.log1p(jnp.exp(-jnp.abs(gates)))
    gcol = jnp.where((lane >> 2) & 1 == 1, log_sig, gates)
    gcol_ref[...] = gcol
    grow_ref[...] = gcol.T[:4 * N_HEADS]


def inproj(x, g1, w_in_p, bias_p):
    n = x.shape[0]
    tm = ROW_TILE
    row = lambda w: pl.BlockSpec((tm, w), lambda i: (i, 0))
    full = lambda a: pl.BlockSpec(a.shape, lambda i: (0,) * a.ndim)
    return pl.pallas_call(
        _inproj_kernel,
        grid=(n // tm,),
        in_specs=[row(D_MODEL), full(g1), full(w_in_p), full(bias_p)],
        out_specs=[row(CONV_W), row(CONV_W), row(QK_W), row(QK_W), row(V_W), row(V_W), row(LANES),
                   pl.BlockSpec((4 * N_HEADS, tm), lambda i: (0, i))],
        out_shape=[
            jax.ShapeDtypeStruct((n, CONV_W), BF16), jax.ShapeDtypeStruct((n, CONV_W), BF16),
            jax.ShapeDtypeStruct((n, QK_W), BF16), jax.ShapeDtypeStruct((n, QK_W), BF16),
            jax.ShapeDtypeStruct((n, V_W), BF16), jax.ShapeDtypeStruct((n, V_W), BF16),
            jax.ShapeDtypeStruct((n, LANES), F32), jax.ShapeDtypeStruct((4 * N_HEADS, n), F32),
        ],
        compiler_params=_cparams("parallel"),
        name="mixer_inproj",
    )(x, g1, w_in_p, bias_p)


def _mlstm_chunk(q2, k2, vh, li_row, lf_row, li_col, lf_col, c_old, m_old, half_mask, tri, tri_t):
    km = k2 * half_mask
    bcum_col = jnp.sum(jnp.where(tri, lf_row, 0.0), axis=1, keepdims=True)
    bcum_row = jnp.sum(jnp.where(tri_t, lf_col, 0.0), axis=0, keepdims=True)
    a_row = li_row - bcum_row
    a_col = li_col - bcum_col
    neg_inf = jnp.float32(-jnp.inf)
    pm_col = jnp.max(jnp.where(tri, a_row, neg_inf), axis=1, keepdims=True)
    mm_col = jnp.maximum(m_old, pm_col)
    mm_last = jnp.maximum(m_old, jnp.max(a_row, axis=1, keepdims=True))
    g_tot = jnp.sum(lf_row, axis=1, keepdims=True)
    dmat = jnp.where(tri, jnp.exp(a_row - mm_col), 0.0)
    s_mat = lax.dot_general(q2, km, (((1,), (1,)), ((), ())), preferred_element_type=F32) * dmat
    scale_inter = jnp.exp(m_old - mm_col)
    qc = jnp.dot(q2, c_old.astype(BF16), preferred_element_type=F32)
    num = jnp.dot(s_mat.astype(BF16), vh, preferred_element_type=F32) + scale_inter * qc[:, :V_DIM]
    den = jnp.sum(s_mat, axis=1, keepdims=True) + scale_inter * qc[:, V_DIM:V_DIM + 1]
    m_t = bcum_col + mm_col
    h = num / jnp.maximum(jnp.abs(den), jnp.exp(-m_t))
    w_col = jnp.exp(a_col - mm_last)
    decay = jnp.exp(m_old - mm_last)
    lane = lax.broadcasted_iota(I32, (CHUNK, LANES), 1)
    vext = jnp.concatenate([vh.astype(F32) * w_col, jnp.where(lane == 0, w_col, 0.0)], axis=1).astype(BF16)
    kv = lax.dot_general(km, vext, (((0,), (0,)), ((), ())), preferred_element_type=F32)
    c_new = decay * c_old + kv
    m_new = g_tot + mm_last
    return h, c_new, m_new


def _mlstm_kernel(*refs, seq_len, reverse):
    if reverse:
        q_ref, k_ref, v_ref, gcol_ref, grow_ref, hf_ref, os_ref, hng_ref, out_ref, c_ref, m_ref = refs
    else:
        q_ref, k_ref, v_ref, gcol_ref, grow_ref, out_ref, c_ref, m_ref = refs
    j = pl.program_id(0)
    nb = pl.num_programs(0)
    blk = MLSTM_BLOCK
    if reverse:
        seq_start = (((nb - j) * blk) % seq_len) == 0
    else:
        seq_start = ((j * blk) % seq_len) == 0

    @pl.when(seq_start)
    def _():
        c_ref[...] = jnp.zeros_like(c_ref)
        m_ref[...] = jnp.zeros_like(m_ref)

    t_i = lax.broadcasted_iota(I32, (CHUNK, CHUNK), 0)
    s_i = lax.broadcasted_iota(I32, (CHUNK, CHUNK), 1)
    tri = (s_i >= t_i) if reverse else (s_i <= t_i)
    tri_t = (t_i >= s_i) if reverse else (t_i <= s_i)
    lane = lax.broadcasted_iota(I32, (CHUNK, LANES), 1)
    half_masks = [jnp.where((lane >> 6) == hh, 1.0, 0.0).astype(BF16) for hh in range(2)]
    g_off = 2 * N_HEADS if reverse else 0
    n_chunks = blk // CHUNK
    order = range(n_chunks - 1, -1, -1) if reverse else range(n_chunks)
    for c in order:
        r0 = c * CHUNK
        rows = slice(r0, r0 + CHUNK)
        for h in range(N_HEADS):
            pair = slice((h // 2) * LANES, (h // 2 + 1) * LANES)
            hv = slice(h * V_DIM, (h + 1) * V_DIM)
            gi, gf = g_off + h, g_off + N_HEADS + h
            h_out, c_new, m_new = _mlstm_chunk(
                q_ref[rows, pair], k_ref[rows, pair], v_ref[rows, hv],
                grow_ref[gi:gi + 1, rows], grow_ref[gf:gf + 1, rows],
                gcol_ref[rows, gi:gi + 1], gcol_ref[rows, gf:gf + 1],
                c_ref[h], m_ref[h:h + 1, 0:1], half_masks[h % 2], tri, tri_t)
            c_ref[h] = c_new
            m_ref[h:h + 1, :] = jnp.broadcast_to(m_new, (1, LANES))
            if reverse:
                ht = hf_ref[rows, hv] + h_out
                ht = _rms(ht, hng_ref[:, hv])
                out_ref[rows, hv] = (os_ref[rows, hv].astype(F32) * ht).astype(BF16)
            else:
                out_ref[rows, hv] = h_out


def mlstm(q, k, v, gcol, grow, seq_len, reverse, hf=None, osig=None, hng=None):
    n = q.shape[0]
    blk = MLSTM_BLOCK
    nb = n // blk
    if reverse:
        imap = lambda j: (nb - 1 - j, 0)
        imap_t = lambda j: (0, nb - 1 - j)
    else:
        imap = lambda j: (j, 0)
        imap_t = lambda j: (0, j)
    row = lambda w: pl.BlockSpec((blk, w), imap)
    in_specs = [row(QK_W), row(QK_W), row(V_W), row(LANES), pl.BlockSpec((4 * N_HEADS, blk), imap_t)]
    args = [q, k, v, gcol, grow]
    if reverse:
        in_specs += [row(V_W), row(V_W), pl.BlockSpec((1, V_W), lambda j: (0, 0))]
        args += [hf, osig, hng]
    return pl.pallas_call(
        functools.partial(_mlstm_kernel, seq_len=seq_len, reverse=reverse),
        grid=(nb,),
        in_specs=in_specs,
        out_specs=row(V_W),
        out_shape=jax.ShapeDtypeStruct((n, V_W), BF16 if reverse else F32),
        scratch_shapes=[pltpu.VMEM((N_HEADS, LANES, 2 * LANES), F32), pltpu.VMEM((8, LANES), F32)],
        compiler_params=_cparams("arbitrary"),
        name="mlstm_bwd" if reverse else "mlstm_fwd",
    )(*args)


HALO = 16


def _outproj_kernel(gb_ref, gcu_ref, gprev_ref, gnext_ref, mo_ref, x_ref, cw_ref, wo_ref, g2_ref, wr_ref,
                    x1_ref, tok_ref, afft_ref, *, seq_len):
    i = pl.program_id(0)
    tm = ROW_TILE
    first = ((i * tm) % seq_len) == 0
    last = (((i + 1) * tm) % seq_len) == 0
    g = gcu_ref[...].astype(F32)
    prev_row = jnp.where(first, 0.0, gprev_ref[HALO - 1:HALO, :].astype(F32))
    next_row = jnp.where(last, 0.0, gnext_ref[0:1, :].astype(F32))
    rid = lax.broadcasted_iota(I32, g.shape, 0)
    dn = jnp.where(rid == 0, prev_row, pltpu.roll(g, 1, axis=0))
    up = jnp.where(rid == tm - 1, next_row, pltpu.roll(g, tm - 1, axis=0))
    conv = dn * cw_ref[0:1, :] + g * cw_ref[1:2, :] + up * cw_ref[2:3, :]
    co = (gb_ref[...].astype(F32) * conv).astype(BF16)
    y = jnp.dot(co, wo_ref[:CONV_W, :], preferred_element_type=F32)
    y = y + jnp.dot(mo_ref[...], wo_ref[CONV_W:, :], preferred_element_type=F32)
    x1 = x_ref[...] + y
    x1_ref[...] = x1
    tokens = _rms(x1, g2_ref[...])
    logits = jnp.dot(tokens, wr_ref[...], preferred_element_type=F32, precision=lax.Precision.HIGHEST)
    lane = lax.broadcasted_iota(I32, logits.shape, 1)
    logits = jnp.where(lane < N_EXPERTS, logits, -jnp.inf)
    ex = jnp.exp(logits - jnp.max(logits, axis=-1, keepdims=True))
    aff = ex / jnp.sum(ex, axis=-1, keepdims=True)
    tok_ref[:, :D_MODEL] = tokens
    tok_ref[:, D_MODEL:] = aff
    afft_ref[...] = aff.T[:N_EXPERTS]


def outproj(gb, gcu, mo, x, cw_p, w_out_b, g2, wr_p, seq_len):
    n = x.shape[0]
    tm = ROW_TILE
    hb = tm // HALO
    n_halo = n // HALO
    row = lambda w: pl.BlockSpec((tm, w), lambda i: (i, 0))
    full = lambda a: pl.BlockSpec(a.shape, lambda i: (0,) * a.ndim)
    prev = pl.BlockSpec((HALO, CONV_W), lambda i: (jnp.maximum(i * hb - 1, 0), 0))
    nxt = pl.BlockSpec((HALO, CONV_W), lambda i: (jnp.minimum((i + 1) * hb, n_halo - 1), 0))
    return pl.pallas_call(
        functools.partial(_outproj_kernel, seq_len=seq_len),
        grid=(n // tm,),
        in_specs=[row(CONV_W), row(CONV_W), prev, nxt, row(V_W), row(D_MODEL),
                  full(cw_p), full(w_out_b), full(g2), full(wr_p)],
        out_specs=[row(D_MODEL), row(TOK_EXT), pl.BlockSpec((N_EXPERTS, tm), lambda i: (0, i))],
        out_shape=[jax.ShapeDtypeStruct((n, D_MODEL), F32), jax.ShapeDtypeStruct((n, TOK_EXT), F32),
                   jax.ShapeDtypeStruct((N_EXPERTS, n), F32)],
        compiler_params=_cparams("parallel"),
        name="mixer_outproj_router",
    )(gb, gcu, gcu, gcu, mo, x, cw_p, w_out_b, g2, wr_p)


TOKEN_MASK = (1 << TOKEN_BITS) - 1


def _ffn_kernel(idx_ref, tok_hbm, wg_ref, wu_ref, wd_ref, ye_ref, xg_ref, sem):
    e = pl.program_id(0)
    s = pl.program_id(1)
    ts = SLOT_TILE
    base = (e * pl.num_programs(1) + s) * ts

    def issue(i, carry):
        tok = idx_ref[base + i] & TOKEN_MASK
        pltpu.make_async_copy(tok_hbm.at[pl.ds(tok, 1)], xg_ref.at[pl.ds(i, 1)], sem).start()
        return carry

    lax.fori_loop(0, ts, issue, 0, unroll=8)
    pltpu.make_async_copy(tok_hbm.at[pl.ds(0, ts)], xg_ref, sem).wait()

    xb = xg_ref[:, :D_MODEL].astype(BF16)
    aff = xg_ref[:, D_MODEL:]
    lane = lax.broadcasted_iota(I32, aff.shape, 1)
    gate = jnp.sum(jnp.where(lane == e, aff, 0.0), axis=1, keepdims=True)
    hg = jnp.dot(xb, wg_ref[0], preferred_element_type=F32)
    hu = jnp.dot(xb, wu_ref[0], preferred_element_type=F32)
    hid = (hg * jax.nn.sigmoid(hg) * hu).astype(BF16)
    ye_ref[...] = jnp.dot(hid, wd_ref[0], preferred_element_type=F32) * gate


def expert_ffn(idx_flat, tok_ext, wg, wu, wd, cap):
    n_tiles = cap // SLOT_TILE
    wspec = lambda: pl.BlockSpec((1, D_MODEL, EXPERT_FF), lambda e, s, *_: (e, 0, 0))
    grid_spec = pltpu.PrefetchScalarGridSpec(
        num_scalar_prefetch=1,
        grid=(N_EXPERTS, n_tiles),
        in_specs=[pl.BlockSpec(memory_space=pl.ANY), wspec(), wspec(),
                  pl.BlockSpec((1, EXPERT_FF, D_MODEL), lambda e, s, *_: (e, 0, 0))],
        out_specs=pl.BlockSpec((SLOT_TILE, D_MODEL), lambda e, s, *_: (e * n_tiles + s, 0)),
        scratch_shapes=[pltpu.VMEM((SLOT_TILE, TOK_EXT), F32), pltpu.SemaphoreType.DMA(())],
    )
    return pl.pallas_call(
        _ffn_kernel,
        grid_spec=grid_spec,
        out_shape=jax.ShapeDtypeStruct((N_EXPERTS * cap, D_MODEL), F32),
        compiler_params=_cparams("arbitrary", "arbitrary"),
        name="expert_ffn",
    )(idx_flat, tok_ext, wg, wu, wd)


def _combine_kernel(idx_ref, off_ref, cnt_ref, x1_ref, ye_hbm, *rest, cap, n_blocks, final):
    if final:
        fg_ref, out_ref, stage_ref, sem = rest
    else:
        out_ref, stage_ref, sem = rest
    j = pl.program_id(0)
    tc = COMBINE_TILE
    cnt_row = cnt_ref[0]
    maxk = jnp.max(cnt_row)

    def zero(k, carry):
        stage_ref[k] = jnp.zeros((tc, D_MODEL), F32)
        return carry

    lax.fori_loop(0, maxk, zero, 0)

    total = jnp.int32(0)
    for e in range(N_EXPERTS):
        lo = off_ref[e * (n_blocks + 1) + j]
        hi = off_ref[e * (n_blocks + 1) + j + 1]

        def issue(slot, carry, e=e):
            p = idx_ref[e * cap + slot]
            t_local = (p & TOKEN_MASK) - j * tc
            k = (p >> TOKEN_BITS) & 0xF
            pltpu.make_async_copy(ye_hbm.at[pl.ds(e * cap + slot, 1)],
                                  stage_ref.at[k, pl.ds(t_local, 1)], sem).start()
            return carry

        lax.fori_loop(lo, hi, issue, 0)
        total = total + (hi - lo)

    def wait_rows(n_rows):
        pltpu.make_async_copy(ye_hbm.at[pl.ds(0, n_rows)], ye_hbm.at[pl.ds(0, n_rows)], sem).wait()

    for bit in range((N_EXPERTS * min(tc, cap)).bit_length()):
        @pl.when((total & (1 << bit)) != 0)
        def _(bit=bit):
            wait_rows(1 << bit)

    def add(k, acc):
        return acc + stage_ref[k]

    out = lax.fori_loop(0, maxk, add, x1_ref[...])
    if final:
        out = _rms(out, fg_ref[...])
    out_ref[...] = out


def combine(idx_flat, off_flat, cnt3, x1, ye, cap, final_g=None):
    n = x1.shape[0]
    tc = COMBINE_TILE
    nb = n // tc
    final = final_g is not None
    in_specs = [pl.BlockSpec((1, 1, tc), lambda j, *_: (j, 0, 0)),
                pl.BlockSpec((tc, D_MODEL), lambda j, *_: (j, 0)),
                pl.BlockSpec(memory_space=pl.ANY)]
    args = [cnt3, x1, ye]
    if final:
        in_specs.append(pl.BlockSpec((1, D_MODEL), lambda j, *_: (0, 0)))
        args.append(final_g)
    grid_spec = pltpu.PrefetchScalarGridSpec(
        num_scalar_prefetch=2,
        grid=(nb,),
        in_specs=in_specs,
        out_specs=pl.BlockSpec((tc, D_MODEL), lambda j, *_: (j, 0)),
        scratch_shapes=[pltpu.VMEM((N_EXPERTS, tc, D_MODEL), F32), pltpu.SemaphoreType.DMA(())],
    )
    return pl.pallas_call(
        functools.partial(_combine_kernel, cap=cap, n_blocks=nb, final=final),
        grid_spec=grid_spec,
        out_shape=jax.ShapeDtypeStruct((n, D_MODEL), F32),
        compiler_params=_cparams("arbitrary"),
        name="moe_combine",
    )(idx_flat, off_flat, *args)


def moe(x1, tok_ext, aff_t, wg, wu, wd, final_g=None):
    n = x1.shape[0]
    cap = n // CAPACITY_DIV
    nb = n // COMBINE_TILE
    packed, rank, cnt = route(aff_t)
    idx_flat = packed.reshape(-1)
    off = jnp.concatenate([rank[:, ::COMBINE_TILE], jnp.full((N_EXPERTS, 1), cap, I32)], axis=1)
    ye = expert_ffn(idx_flat, tok_ext, wg, wu, wd, cap)
    return combine(idx_flat, off.reshape(-1), cnt.reshape(nb, 1, COMBINE_TILE), x1, ye, cap, final_g)


def kernel(x_prompt, x_sample, norm1_g, w_in, conv_w, gate_bias, head_norm_g, w_out, norm2_g, w_router, w_gate, w_up, w_down, final_g):
    depth = w_in.shape[0]
    w_in_p = jnp.pad(w_in, ((0, 0), (0, 0), (0, D_IN_PAD - D_IN))).astype(BF16)
    bias_p = jnp.pad(gate_bias.reshape(depth, 1, 4 * N_HEADS), ((0, 0), (0, 0), (0, LANES - 4 * N_HEADS)))
    cw_p = jnp.pad(conv_w, ((0, 0), (0, 8 - conv_w.shape[1]), (0, 0)))
    wr_p = jnp.pad(w_router, ((0, 0), (0, 0), (0, LANES - N_EXPERTS)))
    w_out_b = w_out.astype(BF16)
    wg_b, wu_b, wd_b = w_gate.astype(BF16), w_up.astype(BF16), w_down.astype(BF16)
    fg = final_g.reshape(1, D_MODEL)

    outs = []
    for x in (x_prompt, x_sample):
        bsz, seq, _ = x.shape
        xf = x.reshape(bsz * seq, D_MODEL)
        for l in range(depth):
            gb, gcu, q, k, v, osig, gcol, grow = inproj(xf, norm1_g[l].reshape(1, -1), w_in_p[l], bias_p[l])
            hf = mlstm(q, k, v, gcol, grow, seq, False)
            mo = mlstm(q, k, v, gcol, grow, seq, True, hf, osig, head_norm_g[l].reshape(1, -1))
            x1, tok_ext, aff_t = outproj(gb, gcu, mo, xf, cw_p[l], w_out_b[l], norm2_g[l].reshape(1, -1), wr_p[l], seq)
            xf = moe(x1, tok_ext, aff_t, wg_b[l], wu_b[l], wd_b[l], fg if l == depth - 1 else None)
        outs.append(xf.reshape(bsz, seq, D_MODEL))
    return tuple(outs)
```

```python
import functools

import jax
import jax.numpy as jnp
from jax import lax
from jax.experimental import pallas as pl
from jax.experimental.pallas import tpu as pltpu

F32 = jnp.float32
BF16 = jnp.bfloat16
I32 = jnp.int32

LANES = 128
N_EXPERTS = 16
CAPACITY_DIV = 8
DISP_BITS = 16
VALID_BIT = 24
TOKEN_ID_LANE = N_EXPERTS


def _threshold_kernel(aff_ref, thr_ref, need_ref, *, cap):
    bits = pltpu.bitcast(aff_ref[...], I32)
    cap_f = jnp.float32(cap)

    def count_ge(cand):
        return jnp.sum(jnp.where(bits >= cand, 1.0, 0.0), axis=1, keepdims=True)

    def body(i, thr):
        cand = thr | jnp.left_shift(jnp.int32(1), 30 - i)
        return jnp.where(count_ge(cand) >= cap_f, cand, thr)

    thr = lax.fori_loop(0, 31, body, jnp.zeros((N_EXPERTS, 1), I32))
    n_gt = jnp.sum(jnp.where(bits > thr, 1.0, 0.0), axis=1, keepdims=True)
    need = (cap_f - n_gt).astype(I32)
    thr_ref[...] = jnp.broadcast_to(thr, thr_ref.shape)
    need_ref[...] = jnp.broadcast_to(need, need_ref.shape)


def _lane_inclusive_scan(x, lane):
    for b in range(7):
        s = 1 << b
        x = x + jnp.where(lane >= s, pltpu.roll(x, s, axis=1), 0.0)
    return x


def _row_exclusive_scan(t, row, n_rows):
    inc = t
    s = 1
    while s < n_rows:
        inc = inc + jnp.where(row >= s, pltpu.roll(inc, s, axis=0), 0.0)
        s *= 2
    return inc - t


def _token_exclusive_scan(x, lane, row, n_rows):
    inc = _lane_inclusive_scan(x, lane)
    tot = jnp.broadcast_to(inc[:, LANES - 1:LANES], x.shape)
    return inc - x + _row_exclusive_scan(tot, row, n_rows)


def _compact_kernel(thr_ref, need_ref, aff_ref, idx_ref, rank_ref, *, n_rows, cap_rows):
    e = pl.program_id(0)
    shape = (n_rows, LANES)
    lane = lax.broadcasted_iota(I32, shape, 1)
    row = lax.broadcasted_iota(I32, shape, 0)
    bits = pltpu.bitcast(aff_ref[0], I32)
    thr = thr_ref[e]
    need = need_ref[e].astype(F32)
    eq = bits == thr
    pre_eq = _token_exclusive_scan(jnp.where(eq, 1.0, 0.0), lane, row, n_rows)
    sel = (bits > thr) | (eq & (pre_eq < need))
    sel_f = jnp.where(sel, 1.0, 0.0)
    rank = _token_exclusive_scan(sel_f, lane, row, n_rows).astype(I32)
    rank_ref[0] = rank

    pos = row * LANES + lane
    disp = pos - rank
    v = jnp.where(sel, disp | (1 << VALID_BIT), 0)
    n_bits = (n_rows * LANES - 1).bit_length()
    for b in range(n_bits):
        if b < 7:
            s = 1 << b
            r1 = pltpu.roll(v, LANES - s, axis=1)
            r2 = pltpu.roll(r1, n_rows - 1, axis=0)
            moved = jnp.where(lane < LANES - s, r1, r2)
        else:
            sr = 1 << (b - 7)
            moved = pltpu.roll(v, n_rows - sr, axis=0)
        take = ((moved >> VALID_BIT) & 1 == 1) & ((moved >> b) & 1 == 1)
        stay = ((v >> VALID_BIT) & 1 == 1) & ((v >> b) & 1 == 0)
        v = jnp.where(take, moved, jnp.where(stay, v, 0))
    idx_ref[0] = (pos + (v & ((1 << DISP_BITS) - 1)))[:cap_rows]


def route(aff_t):
    n_exp, n = aff_t.shape
    assert n <= (1 << DISP_BITS)
    cap = n // CAPACITY_DIV
    n_rows = n // LANES
    cap_rows = cap // LANES
    thr, need = pl.pallas_call(
        functools.partial(_threshold_kernel, cap=cap),
        out_shape=(jax.ShapeDtypeStruct((n_exp, LANES), I32), jax.ShapeDtypeStruct((n_exp, LANES), I32)),
        name="route_threshold",
    )(aff_t)
    grid_spec = pltpu.PrefetchScalarGridSpec(
        num_scalar_prefetch=2,
        grid=(n_exp,),
        in_specs=[pl.BlockSpec((1, n_rows, LANES), lambda e, *_: (e, 0, 0))],
        out_specs=[
            pl.BlockSpec((1, cap_rows, LANES), lambda e, *_: (e, 0, 0)),
            pl.BlockSpec((1, n_rows, LANES), lambda e, *_: (e, 0, 0)),
        ],
    )
    idx, rank = pl.pallas_call(
        functools.partial(_compact_kernel, n_rows=n_rows, cap_rows=cap_rows),
        grid_spec=grid_spec,
        out_shape=(
            jax.ShapeDtypeStruct((n_exp, cap_rows, LANES), I32),
            jax.ShapeDtypeStruct((n_exp, n_rows, LANES), I32),
        ),
        compiler_params=pltpu.CompilerParams(dimension_semantics=("parallel",)),
        name="route_compact",
    )(thr[:, 0], need[:, 0], aff_t.reshape(n_exp, n_rows, LANES))
    return idx.reshape(n_exp, cap), rank.reshape(n_exp, n)


D_MODEL = 1024
CONV_W = 512
N_HEADS = 4
V_DIM = 128
QK_DIM = 64
QK_W = N_HEADS * QK_DIM
V_W = N_HEADS * V_DIM
CHUNK = 128
EXPERT_FF = 1024
EPS = 1e-6
C_GB, C_GC, C_U, C_Q, C_K, C_V, C_O, C_G = 0, 512, 1024, 1536, 1792, 2048, 2560, 3072
D_IN = 3088
D_IN_PAD = 3200
TOK_EXT = D_MODEL + LANES

ROW_TILE = 512
MLSTM_BLOCK = 256
SLOT_TILE = 256
COMBINE_TILE = 256
VMEM_LIMIT = 56 * 1024 * 1024


def _cparams(*sem):
    return pltpu.CompilerParams(dimension_semantics=sem, vmem_limit_bytes=VMEM_LIMIT)


def _rms(x, g):
    return x * lax.rsqrt(jnp.mean(x * x, axis=-1, keepdims=True) + EPS) * g


GATE_ROWS = 4 * N_HEADS


def _inproj_kernel(x_ref, g_ref, w_ref, bias_ref, gb_ref, gcu_ref, q_ref, k_ref, v_ref, os_ref, gcol_ref, grow_ref):
    hn = _rms(x_ref[...], g_ref[...]).astype(BF16)

    def seg(a, b):
        return jnp.dot(hn, w_ref[:, a:b], preferred_element_type=F32)

    gb_ref[...] = seg(C_GB, C_GC).astype(BF16)
    gcu_ref[...] = (seg(C_GC, C_U) * seg(C_U, C_Q)).astype(BF16)
    q_ref[...] = (seg(C_Q, C_K) * (QK_DIM ** -0.5)).astype(BF16)
    k_ref[...] = seg(C_K, C_V).astype(BF16)
    v_ref[...] = seg(C_V, C_O).astype(BF16)
    os_ref[...] = jax.nn.sigmoid(seg(C_O, C_G)).astype(BF16)
    gates = seg(C_G, D_IN_PAD) + bias_ref[...]
    lane = lax.broadcasted_iota(I32, gates.shape, 1)
    log_sig = jnp.minimum(gates, 0.0) - jnp.log1p(jnp.exp(-jnp.abs(gates)))
    gcol = jnp.where((lane >> 2) & 1 == 1, log_sig, gates)
    gcol_ref[...] = gcol
    grow_ref[...] = gcol.T[:GATE_ROWS]


def inproj(x, g1, w_in_p, bias_p):
    n = x.shape[0]
    tm = ROW_TILE
    row = lambda w: pl.BlockSpec((tm, w), lambda i: (i, 0))
    full = lambda a: pl.BlockSpec(a.shape, lambda i: (0,) * a.ndim)
    return pl.pallas_call(
        _inproj_kernel,
        grid=(n // tm,),
        in_specs=[row(D_MODEL), full(g1), full(w_in_p), full(bias_p)],
        out_specs=[row(CONV_W), row(CONV_W), row(QK_W), row(QK_W), row(V_W), row(V_W), row(LANES),
                   pl.BlockSpec((GATE_ROWS, tm), lambda i: (0, i))],
        out_shape=[
            jax.ShapeDtypeStruct((n, CONV_W), BF16), jax.ShapeDtypeStruct((n, CONV_W), BF16),
            jax.ShapeDtypeStruct((n, QK_W), BF16), jax.ShapeDtypeStruct((n, QK_W), BF16),
            jax.ShapeDtypeStruct((n, V_W), BF16), jax.ShapeDtypeStruct((n, V_W), BF16),
            jax.ShapeDtypeStruct((n, LANES), F32), jax.ShapeDtypeStruct((GATE_ROWS, n), F32),
        ],
        compiler_params=_cparams("parallel"),
        name="mixer_inproj",
    )(x, g1, w_in_p, bias_p)


def _mlstm_chunk(q2, k2, vh, li_row, lf_row, li_col, lf_col, c_old, m_old, half_mask, tri, tri_t):
    km = k2 * half_mask
    bcum_col = jnp.sum(jnp.where(tri, lf_row, 0.0), axis=1, keepdims=True)
    bcum_row = jnp.sum(jnp.where(tri_t, lf_col, 0.0), axis=0, keepdims=True)
    a_row = li_row - bcum_row
    a_col = li_col - bcum_col
    neg_inf = jnp.float32(-jnp.inf)
    pm_col = jnp.max(jnp.where(tri, a_row, neg_inf), axis=1, keepdims=True)
    mm_col = jnp.maximum(m_old, pm_col)
    mm_last = jnp.maximum(m_old, jnp.max(a_row, axis=1, keepdims=True))
    g_tot = jnp.sum(lf_row, axis=1, keepdims=True)
    dmat = jnp.where(tri, jnp.exp(a_row - mm_col), 0.0)
    s_mat = lax.dot_general(q2, km, (((1,), (1,)), ((), ())), preferred_element_type=F32) * dmat
    scale_inter = jnp.exp(m_old - mm_col)
    qc = jnp.dot(q2, c_old.astype(BF16), preferred_element_type=F32)
    num = jnp.dot(s_mat.astype(BF16), vh, preferred_element_type=F32) + scale_inter * qc[:, :V_DIM]
    den = jnp.sum(s_mat, axis=1, keepdims=True) + scale_inter * qc[:, V_DIM:V_DIM + 1]
    m_t = bcum_col + mm_col
    h = num / jnp.maximum(jnp.abs(den), jnp.exp(-m_t))
    w_col = jnp.exp(a_col - mm_last)
    decay = jnp.exp(m_old - mm_last)
    lane = lax.broadcasted_iota(I32, (CHUNK, LANES), 1)
    vext = jnp.concatenate([vh.astype(F32) * w_col, jnp.where(lane == 0, w_col, 0.0)], axis=1).astype(BF16)
    kv = lax.dot_general(km, vext, (((0,), (0,)), ((), ())), preferred_element_type=F32)
    return h, decay * c_old + kv, g_tot + mm_last


def _mlstm_kernel(*refs, seq_len, reverse):
    if reverse:
        q_ref, k_ref, v_ref, gcol_ref, grow_ref, hf_ref, os_ref, hng_ref, out_ref, c_ref, m_ref = refs
    else:
        q_ref, k_ref, v_ref, gcol_ref, grow_ref, out_ref, c_ref, m_ref = refs
    j = pl.program_id(0)
    nb = pl.num_programs(0)
    blk = MLSTM_BLOCK
    if reverse:
        seq_start = (((nb - j) * blk) % seq_len) == 0
    else:
        seq_start = ((j * blk) % seq_len) == 0

    @pl.when(seq_start)
    def _():
        c_ref[...] = jnp.zeros_like(c_ref)
        m_ref[...] = jnp.zeros_like(m_ref)

    t_i = lax.broadcasted_iota(I32, (CHUNK, CHUNK), 0)
    s_i = lax.broadcasted_iota(I32, (CHUNK, CHUNK), 1)
    tri = (s_i >= t_i) if reverse else (s_i <= t_i)
    tri_t = (t_i >= s_i) if reverse else (t_i <= s_i)
    lane = lax.broadcasted_iota(I32, (CHUNK, LANES), 1)
    half_masks = [jnp.where((lane >> 6) == hh, 1.0, 0.0).astype(BF16) for hh in range(2)]
    g_off = 2 * N_HEADS if reverse else 0
    n_chunks = blk // CHUNK
    order = range(n_chunks - 1, -1, -1) if reverse else range(n_chunks)
    for c in order:
        r0 = c * CHUNK
        rows = slice(r0, r0 + CHUNK)
        for h in range(N_HEADS):
            pair = slice((h // 2) * LANES, (h // 2 + 1) * LANES)
            hv = slice(h * V_DIM, (h + 1) * V_DIM)
            gi, gf = g_off + h, g_off + N_HEADS + h
            h_out, c_new, m_new = _mlstm_chunk(
                q_ref[rows, pair], k_ref[rows, pair], v_ref[rows, hv],
                grow_ref[gi:gi + 1, rows], grow_ref[gf:gf + 1, rows],
                gcol_ref[rows, gi:gi + 1], gcol_ref[rows, gf:gf + 1],
                c_ref[h], m_ref[h:h + 1, 0:1], half_masks[h % 2], tri, tri_t)
            c_ref[h] = c_new
            m_ref[h:h + 1, :] = jnp.broadcast_to(m_new, (1, LANES))
            if reverse:
                ht = hf_ref[rows, hv] + h_out
                ht = _rms(ht, hng_ref[:, hv])
                out_ref[rows, hv] = (os_ref[rows, hv].astype(F32) * ht).astype(BF16)
            else:
                out_ref[rows, hv] = h_out


def mlstm(q, k, v, gcol, grow, seq_len, reverse, hf=None, osig=None, hng=None):
    n = q.shape[0]
    blk = MLSTM_BLOCK
    nb = n // blk
    if reverse:
        imap = lambda j: (nb - 1 - j, 0)
        imap_t = lambda j: (0, nb - 1 - j)
    else:
        imap = lambda j: (j, 0)
        imap_t = lambda j: (0, j)
    row = lambda w: pl.BlockSpec((blk, w), imap)
    in_specs = [row(QK_W), row(QK_W), row(V_W), row(LANES), pl.BlockSpec((GATE_ROWS, blk), imap_t)]
    args = [q, k, v, gcol, grow]
    if reverse:
        in_specs += [row(V_W), row(V_W), pl.BlockSpec((1, V_W), lambda j: (0, 0))]
        args += [hf, osig, hng]
    return pl.pallas_call(
        functools.partial(_mlstm_kernel, seq_len=seq_len, reverse=reverse),
        grid=(nb,),
        in_specs=in_specs,
        out_specs=row(V_W),
        out_shape=jax.ShapeDtypeStruct((n, V_W), BF16 if reverse else F32),
        scratch_shapes=[pltpu.VMEM((N_HEADS, LANES, 2 * LANES), F32), pltpu.VMEM((8, LANES), F32)],
        compiler_params=_cparams("arbitrary"),
        name="mlstm_bwd" if reverse else "mlstm_fwd",
    )(*args)


HALO = 16


def _outproj_kernel(gb_ref, gcu_ref, gprev_ref, gnext_ref, mo_ref, x_ref, cw_ref, wo_ref, g2_ref, wr_ref,
                    x1_ref, tok_ref, afft_ref, *, seq_len):
    i = pl.program_id(0)
    tm = ROW_TILE
    first = ((i * tm) % seq_len) == 0
    last = (((i + 1) * tm) % seq_len) == 0
    g = gcu_ref[...].astype(F32)
    prev_row = jnp.where(first, 0.0, gprev_ref[HALO - 1:HALO, :].astype(F32))
    next_row = jnp.where(last, 0.0, gnext_ref[0:1, :].astype(F32))
    rid = lax.broadcasted_iota(I32, g.shape, 0)
    dn = jnp.where(rid == 0, prev_row, pltpu.roll(g, 1, axis=0))
    up = jnp.where(rid == tm - 1, next_row, pltpu.roll(g, tm - 1, axis=0))
    conv = dn * cw_ref[0:1, :] + g * cw_ref[1:2, :] + up * cw_ref[2:3, :]
    co = (gb_ref[...].astype(F32) * conv).astype(BF16)
    y = jnp.dot(co, wo_ref[:CONV_W, :], preferred_element_type=F32)
    y = y + jnp.dot(mo_ref[...], wo_ref[CONV_W:, :], preferred_element_type=F32)
    x1 = x_ref[...] + y
    x1_ref[...] = x1
    tokens = _rms(x1, g2_ref[...])
    logits = jnp.dot(tokens, wr_ref[...], preferred_element_type=F32, precision=lax.Precision.HIGHEST)
    lane = lax.broadcasted_iota(I32, logits.shape, 1)
    logits = jnp.where(lane < N_EXPERTS, logits, -jnp.inf)
    ex = jnp.exp(logits - jnp.max(logits, axis=-1, keepdims=True))
    aff = ex / jnp.sum(ex, axis=-1, keepdims=True)
    tok_ref[:, :D_MODEL] = tokens
    row_id = (i * tm + lax.broadcasted_iota(I32, aff.shape, 0)).astype(F32)
    tok_ref[:, D_MODEL:] = jnp.where(lane == TOKEN_ID_LANE, row_id, aff)
    afft_ref[...] = aff.T[:N_EXPERTS]


def outproj(gb, gcu, mo, x, cw_p, w_out_b, g2, wr_p, seq_len):
    n = x.shape[0]
    tm = ROW_TILE
    hb = tm // HALO
    n_halo = n // HALO
    row = lambda w: pl.BlockSpec((tm, w), lambda i: (i, 0))
    full = lambda a: pl.BlockSpec(a.shape, lambda i: (0,) * a.ndim)
    prev = pl.BlockSpec((HALO, CONV_W), lambda i: (jnp.maximum(i * hb - 1, 0), 0))
    nxt = pl.BlockSpec((HALO, CONV_W), lambda i: (jnp.minimum((i + 1) * hb, n_halo - 1), 0))
    return pl.pallas_call(
        functools.partial(_outproj_kernel, seq_len=seq_len),
        grid=(n // tm,),
        in_specs=[row(CONV_W), row(CONV_W), prev, nxt, row(V_W), row(D_MODEL),
                  full(cw_p), full(w_out_b), full(g2), full(wr_p)],
        out_specs=[row(D_MODEL), row(TOK_EXT), pl.BlockSpec((N_EXPERTS, tm), lambda i: (0, i))],
        out_shape=[jax.ShapeDtypeStruct((n, D_MODEL), F32), jax.ShapeDtypeStruct((n, TOK_EXT), F32),
                   jax.ShapeDtypeStruct((N_EXPERTS, n), F32)],
        compiler_params=_cparams("parallel"),
        name="mixer_outproj_router",
    )(gb, gcu, gcu, gcu, mo, x, cw_p, w_out_b, g2, wr_p)


def _ffn_kernel(idx_ref, tok_hbm, wg_ref, wu_ref, wd_ref, ye_ref, xg_ref, wgb_ref, wub_ref, wdb_ref, sem):
    e = pl.program_id(0)
    s = pl.program_id(1)
    n_tiles = pl.num_programs(1)
    ts = SLOT_TILE
    t = e * n_tiles + s
    slot = t % 2

    def gather(tile, slot_):
        base = tile * ts

        def issue(i, carry):
            tok = idx_ref[base + i]
            pltpu.make_async_copy(tok_hbm.at[pl.ds(tok, 1)], xg_ref.at[slot_, pl.ds(i, 1)], sem.at[slot_]).start()
            return carry

        lax.fori_loop(0, ts, issue, 0, unroll=8)

    @pl.when(t == 0)
    def _():
        gather(0, 0)

    @pl.when(s == 0)
    def _():
        wgb_ref[...] = wg_ref[0].astype(BF16)
        wub_ref[...] = wu_ref[0].astype(BF16)
        wdb_ref[...] = wd_ref[0].astype(BF16)

    pltpu.make_async_copy(tok_hbm.at[pl.ds(0, ts)], xg_ref.at[slot], sem.at[slot]).wait()

    @pl.when(t + 1 < N_EXPERTS * n_tiles)
    def _():
        gather(t + 1, 1 - slot)

    x = xg_ref[slot]
    xb = x[:, :D_MODEL].astype(BF16)
    ext = x[:, D_MODEL:]
    lane = lax.broadcasted_iota(I32, ext.shape, 1)
    gate = jnp.sum(jnp.where(lane == e, ext, 0.0), axis=1, keepdims=True)
    hg = jnp.dot(xb, wgb_ref[...], preferred_element_type=F32)
    hu = jnp.dot(xb, wub_ref[...], preferred_element_type=F32)
    hid = (hg * jax.nn.sigmoid(hg) * hu).astype(BF16)
    ye_ref[:, :D_MODEL] = jnp.dot(hid, wdb_ref[...], preferred_element_type=F32) * gate
    ye_ref[:, D_MODEL:] = ext


def expert_ffn(idx_flat, tok_ext, wg, wu, wd, cap):
    n_tiles = cap // SLOT_TILE
    wspec = lambda: pl.BlockSpec((1, D_MODEL, EXPERT_FF), lambda e, s, *_: (e, 0, 0))
    grid_spec = pltpu.PrefetchScalarGridSpec(
        num_scalar_prefetch=1,
        grid=(N_EXPERTS, n_tiles),
        in_specs=[pl.BlockSpec(memory_space=pl.ANY), wspec(), wspec(),
                  pl.BlockSpec((1, EXPERT_FF, D_MODEL), lambda e, s, *_: (e, 0, 0))],
        out_specs=pl.BlockSpec((SLOT_TILE, TOK_EXT), lambda e, s, *_: (e * n_tiles + s, 0)),
        scratch_shapes=[pltpu.VMEM((2, SLOT_TILE, TOK_EXT), F32),
                        pltpu.VMEM((D_MODEL, EXPERT_FF), BF16), pltpu.VMEM((D_MODEL, EXPERT_FF), BF16),
                        pltpu.VMEM((EXPERT_FF, D_MODEL), BF16), pltpu.SemaphoreType.DMA((2,))],
    )
    return pl.pallas_call(
        _ffn_kernel,
        grid_spec=grid_spec,
        out_shape=jax.ShapeDtypeStruct((N_EXPERTS * cap, TOK_EXT), F32),
        compiler_params=_cparams("arbitrary", "arbitrary"),
        name="expert_ffn",
    )(idx_flat, tok_ext, wg, wu, wd)


WIN = 8
STACK_TILE = 256


def _stack_rows(tc):
    rows = N_EXPERTS * (tc + 2 * (WIN - 1))
    return -(-rows // STACK_TILE) * STACK_TILE


def _combine_kernel(off_ref, x1_ref, ye_hbm, *rest, cap, n_blocks, final):
    if final:
        fg_ref, out_ref, ys_ref, sem = rest
    else:
        out_ref, ys_ref, sem = rest
    j = pl.program_id(0)
    tc = COMBINE_TILE

    @pl.when(j == 0)
    def _():
        ys_ref[...] = jnp.zeros_like(ys_ref)

    base = jnp.int32(0)
    for e in range(N_EXPERTS):
        lo = off_ref[e * (n_blocks + 1) + j]
        hi = off_ref[e * (n_blocks + 1) + j + 1]
        start = (lo >> 3) << 3
        n_win = jnp.where(hi > lo, (hi - start + (WIN - 1)) >> 3, 0)

        def issue(w, carry, e=e, start=start, base=base):
            src = pl.multiple_of(e * cap + start + w * WIN, WIN)
            dst = pl.multiple_of(base + w * WIN, WIN)
            pltpu.make_async_copy(ye_hbm.at[pl.ds(src, WIN)], ys_ref.at[pl.ds(dst, WIN)], sem).start()
            return carry

        lax.fori_loop(0, n_win, issue, 0)
        base = base + n_win * WIN

    n_total = base >> 3
    for bit in range((_stack_rows(tc) // WIN).bit_length()):
        @pl.when((n_total & (1 << bit)) != 0)
        def _(bit=bit):
            rows = WIN << bit
            pltpu.make_async_copy(ye_hbm.at[pl.ds(0, rows)], ye_hbm.at[pl.ds(0, rows)], sem).wait()

    tok0 = (j * tc).astype(F32)
    lane_t = lax.broadcasted_iota(I32, (STACK_TILE, tc), 1).astype(F32)
    row_i = lax.broadcasted_iota(I32, (STACK_TILE, 1), 0)

    def accumulate(kt, acc):
        r0 = pl.multiple_of(kt * STACK_TILE, STACK_TILE)
        rows = ys_ref[pl.ds(r0, STACK_TILE), :]
        tok_local = rows[:, D_MODEL + TOKEN_ID_LANE:D_MODEL + TOKEN_ID_LANE + 1] - tok0
        tok_local = jnp.where(r0 + row_i < base, tok_local, -1.0)
        onehot = jnp.where(tok_local == lane_t, 1.0, 0.0).astype(BF16)
        return acc + lax.dot_general(onehot, rows[:, :D_MODEL].astype(BF16), (((0,), (0,)), ((), ())),
                                     preferred_element_type=F32)

    n_kt = (base + (STACK_TILE - 1)) // STACK_TILE
    out = lax.fori_loop(0, n_kt, accumulate, x1_ref[...])
    if final:
        out = _rms(out, fg_ref[...])
    out_ref[...] = out


def combine(off_flat, x1, ye, cap, final_g=None):
    n = x1.shape[0]
    tc = COMBINE_TILE
    nb = n // tc
    final = final_g is not None
    in_specs = [pl.BlockSpec((tc, D_MODEL), lambda j, *_: (j, 0)),
                pl.BlockSpec(memory_space=pl.ANY)]
    args = [x1, ye]
    if final:
        in_specs.append(pl.BlockSpec((1, D_MODEL), lambda j, *_: (0, 0)))
        args.append(final_g)
    grid_spec = pltpu.PrefetchScalarGridSpec(
        num_scalar_prefetch=1,
        grid=(nb,),
        in_specs=in_specs,
        out_specs=pl.BlockSpec((tc, D_MODEL), lambda j, *_: (j, 0)),
        scratch_shapes=[pltpu.VMEM((_stack_rows(tc), TOK_EXT), F32), pltpu.SemaphoreType.DMA(())],
    )
    return pl.pallas_call(
        functools.partial(_combine_kernel, cap=cap, n_blocks=nb, final=final),
        grid_spec=grid_spec,
        out_shape=jax.ShapeDtypeStruct((n, D_MODEL), F32),
        compiler_params=_cparams("arbitrary"),
        name="moe_combine",
    )(off_flat, *args)


def moe(x1, tok_ext, aff_t, wg, wu, wd, final_g=None):
    n = x1.shape[0]
    cap = n // CAPACITY_DIV
    idx, rank = route(aff_t)
    off = jnp.concatenate([rank[:, ::COMBINE_TILE], jnp.full((N_EXPERTS, 1), cap, I32)], axis=1)
    ye = expert_ffn(idx.reshape(-1), tok_ext, wg, wu, wd, cap)
    return combine(off.reshape(-1), x1, ye, cap, final_g)


def kernel(x_prompt, x_sample, norm1_g, w_in, conv_w, gate_bias, head_norm_g, w_out, norm2_g, w_router, w_gate, w_up, w_down, final_g):
    depth = w_in.shape[0]
    w_in_p = jnp.pad(w_in, ((0, 0), (0, 0), (0, D_IN_PAD - D_IN))).astype(BF16)
    bias_p = jnp.pad(gate_bias.reshape(depth, 1, 4 * N_HEADS), ((0, 0), (0, 0), (0, LANES - 4 * N_HEADS)))
    cw_p = jnp.pad(conv_w, ((0, 0), (0, 8 - conv_w.shape[1]), (0, 0)))
    wr_p = jnp.pad(w_router, ((0, 0), (0, 0), (0, LANES - N_EXPERTS)))
    w_out_b = w_out.astype(BF16)
    fg = final_g.reshape(1, D_MODEL)

    outs = []
    for x in (x_prompt, x_sample):
        bsz, seq, _ = x.shape
        xf = x.reshape(bsz * seq, D_MODEL)
        for l in range(depth):
            gb, gcu, q, k, v, osig, gcol, grow = inproj(xf, norm1_g[l].reshape(1, -1), w_in_p[l], bias_p[l])
            hf = mlstm(q, k, v, gcol, grow, seq, False)
            mo = mlstm(q, k, v, gcol, grow, seq, True, hf, osig, head_norm_g[l].reshape(1, -1))
            x1, tok_ext, aff_t = outproj(gb, gcu, mo, xf, cw_p[l], w_out_b[l], norm2_g[l].reshape(1, -1), wr_p[l], seq)
            xf = moe(x1, tok_ext, aff_t, w_gate[l], w_up[l], w_down[l], fg if l == depth - 1 else None)
        outs.append(xf.reshape(bsz, seq, D_MODEL))
    return tuple(outs)
```

```python
import functools

import jax
import jax.numpy as jnp
from jax import lax
from jax.experimental import pallas as pl
from jax.experimental.pallas import tpu as pltpu

F32 = jnp.float32
BF16 = jnp.bfloat16
I32 = jnp.int32

LANES = 128
N_EXPERTS = 16
CAPACITY_DIV = 8
DISP_BITS = 16
VALID_BIT = 24
TOKEN_ID_LANE = N_EXPERTS


def _threshold_kernel(aff_ref, thr_ref, need_ref, *, cap):
    bits = pltpu.bitcast(aff_ref[...], I32)
    cap_f = jnp.float32(cap)

    def count_ge(cand):
        return jnp.sum(jnp.where(bits >= cand, 1.0, 0.0), axis=1, keepdims=True)

    def body(i, thr):
        cand = thr | jnp.left_shift(jnp.int32(1), 30 - i)
        return jnp.where(count_ge(cand) >= cap_f, cand, thr)

    thr = lax.fori_loop(0, 31, body, jnp.zeros((N_EXPERTS, 1), I32))
    n_gt = jnp.sum(jnp.where(bits > thr, 1.0, 0.0), axis=1, keepdims=True)
    need = (cap_f - n_gt).astype(I32)
    thr_ref[...] = jnp.broadcast_to(thr, thr_ref.shape)
    need_ref[...] = jnp.broadcast_to(need, need_ref.shape)


def _lane_inclusive_scan(x, lane):
    for b in range(7):
        s = 1 << b
        x = x + jnp.where(lane >= s, pltpu.roll(x, s, axis=1), 0.0)
    return x


def _row_exclusive_scan(t, row, n_rows):
    inc = t
    s = 1
    while s < n_rows:
        inc = inc + jnp.where(row >= s, pltpu.roll(inc, s, axis=0), 0.0)
        s *= 2
    return inc - t


def _token_exclusive_scan(x, lane, row, n_rows):
    inc = _lane_inclusive_scan(x, lane)
    tot = jnp.broadcast_to(inc[:, LANES - 1:LANES], x.shape)
    return inc - x + _row_exclusive_scan(tot, row, n_rows)


def _compact_kernel(thr_ref, need_ref, aff_ref, idx_ref, rank_ref, *, n_rows, cap_rows):
    e = pl.program_id(0)
    shape = (n_rows, LANES)
    lane = lax.broadcasted_iota(I32, shape, 1)
    row = lax.broadcasted_iota(I32, shape, 0)
    bits = pltpu.bitcast(aff_ref[0], I32)
    thr = thr_ref[e]
    need = need_ref[e].astype(F32)
    eq = bits == thr
    pre_eq = _token_exclusive_scan(jnp.where(eq, 1.0, 0.0), lane, row, n_rows)
    sel = (bits > thr) | (eq & (pre_eq < need))
    sel_f = jnp.where(sel, 1.0, 0.0)
    rank = _token_exclusive_scan(sel_f, lane, row, n_rows).astype(I32)
    rank_ref[0] = rank

    pos = row * LANES + lane
    disp = pos - rank
    v = jnp.where(sel, disp | (1 << VALID_BIT), 0)
    n_bits = (n_rows * LANES - 1).bit_length()
    for b in range(n_bits):
        if b < 7:
            s = 1 << b
            r1 = pltpu.roll(v, LANES - s, axis=1)
            r2 = pltpu.roll(r1, n_rows - 1, axis=0)
            moved = jnp.where(lane < LANES - s, r1, r2)
        else:
            sr = 1 << (b - 7)
            moved = pltpu.roll(v, n_rows - sr, axis=0)
        take = ((moved >> VALID_BIT) & 1 == 1) & ((moved >> b) & 1 == 1)
        stay = ((v >> VALID_BIT) & 1 == 1) & ((v >> b) & 1 == 0)
        v = jnp.where(take, moved, jnp.where(stay, v, 0))
    idx_ref[0] = (pos + (v & ((1 << DISP_BITS) - 1)))[:cap_rows]


def route(aff_t):
    n_exp, n = aff_t.shape
    assert n <= (1 << DISP_BITS)
    cap = n // CAPACITY_DIV
    n_rows = n // LANES
    cap_rows = cap // LANES
    thr, need = pl.pallas_call(
        functools.partial(_threshold_kernel, cap=cap),
        out_shape=(jax.ShapeDtypeStruct((n_exp, LANES), I32), jax.ShapeDtypeStruct((n_exp, LANES), I32)),
        name="route_threshold",
    )(aff_t)
    grid_spec = pltpu.PrefetchScalarGridSpec(
        num_scalar_prefetch=2,
        grid=(n_exp,),
        in_specs=[pl.BlockSpec((1, n_rows, LANES), lambda e, *_: (e, 0, 0))],
        out_specs=[
            pl.BlockSpec((1, cap_rows, LANES), lambda e, *_: (e, 0, 0)),
            pl.BlockSpec((1, n_rows, LANES), lambda e, *_: (e, 0, 0)),
        ],
    )
    idx, rank = pl.pallas_call(
        functools.partial(_compact_kernel, n_rows=n_rows, cap_rows=cap_rows),
        grid_spec=grid_spec,
        out_shape=(
            jax.ShapeDtypeStruct((n_exp, cap_rows, LANES), I32),
            jax.ShapeDtypeStruct((n_exp, n_rows, LANES), I32),
        ),
        compiler_params=pltpu.CompilerParams(dimension_semantics=("parallel",)),
        name="route_compact",
    )(thr[:, 0], need[:, 0], aff_t.reshape(n_exp, n_rows, LANES))
    return idx.reshape(n_exp, cap), rank.reshape(n_exp, n)


D_MODEL = 1024
CONV_W = 512
N_HEADS = 4
V_DIM = 128
QK_DIM = 64
QK_W = N_HEADS * QK_DIM
V_W = N_HEADS * V_DIM
CHUNK = 128
EXPERT_FF = 1024
EPS = 1e-6
C_GB, C_GC, C_U, C_Q, C_K, C_V, C_O, C_G = 0, 512, 1024, 1536, 1792, 2048, 2560, 3072
D_IN = 3088
D_IN_PAD = 3200
TOK_EXT = D_MODEL + LANES

ROW_TILE = 512
MLSTM_BLOCK = 256
SLOT_TILE = 256
COMBINE_TILE = 256
VMEM_LIMIT = 56 * 1024 * 1024


def _cparams(*sem):
    return pltpu.CompilerParams(dimension_semantics=sem, vmem_limit_bytes=VMEM_LIMIT)


def _rms(x, g):
    return x * lax.rsqrt(jnp.mean(x * x, axis=-1, keepdims=True) + EPS) * g


GATE_GROUP = 8
GATE_ROWS = 6 * GATE_GROUP
G_A, G_PM, G_B = 0, 1, 2


def _chunk_scan(x, op, identity, reverse):
    width = x.shape[1]
    pos = lax.broadcasted_iota(I32, x.shape, 1) & (CHUNK - 1)
    for b in range(7):
        s = 1 << b
        if reverse:
            shifted = jnp.where(pos < CHUNK - s, pltpu.roll(x, width - s, axis=1), identity)
        else:
            shifted = jnp.where(pos >= s, pltpu.roll(x, s, axis=1), identity)
        x = op(x, shifted)
    return x


def _inproj_kernel(x_ref, g_ref, w_ref, bias_ref, gb_ref, gcu_ref, q_ref, k_ref, v_ref, os_ref, grow_ref):
    hn = _rms(x_ref[...], g_ref[...]).astype(BF16)

    def seg(a, b):
        return jnp.dot(hn, w_ref[:, a:b], preferred_element_type=F32)

    gates = seg(C_G, D_IN_PAD) + bias_ref[...]
    lane = lax.broadcasted_iota(I32, gates.shape, 1)
    log_sig = jnp.minimum(gates, 0.0) - jnp.log1p(jnp.exp(-jnp.abs(gates)))
    g16 = jnp.where((lane >> 2) & 1 == 1, log_sig, gates).T[:4 * N_HEADS]
    fwd = lax.broadcasted_iota(I32, g16.shape, 0) < 2 * N_HEADS
    cs = jnp.where(fwd, _chunk_scan(g16, jnp.add, 0.0, False), _chunk_scan(g16, jnp.add, 0.0, True))
    cs = pltpu.roll(cs, 3 * N_HEADS, axis=0)
    a = g16 - cs
    neg_inf = jnp.float32(-jnp.inf)
    pm = jnp.where(fwd, _chunk_scan(a, jnp.maximum, neg_inf, False), _chunk_scan(a, jnp.maximum, neg_inf, True))
    for d in range(2):
        grp = slice(d * GATE_GROUP, (d + 1) * GATE_GROUP)
        for g, val in ((G_A, a), (G_PM, pm), (G_B, cs)):
            r0 = (3 * d + g) * GATE_GROUP
            grow_ref[r0:r0 + GATE_GROUP, :] = val[grp]

    gb_ref[...] = seg(C_GB, C_GC).astype(BF16)
    gcu_ref[...] = (seg(C_GC, C_U) * seg(C_U, C_Q)).astype(BF16)
    q_ref[...] = (seg(C_Q, C_K) * (QK_DIM ** -0.5)).astype(BF16)
    k_ref[...] = seg(C_K, C_V).astype(BF16)
    v_ref[...] = seg(C_V, C_O).astype(BF16)
    os_ref[...] = jax.nn.sigmoid(seg(C_O, C_G)).astype(BF16)


def inproj(x, g1, w_in_p, bias_p):
    n = x.shape[0]
    tm = ROW_TILE
    row = lambda w: pl.BlockSpec((tm, w), lambda i: (i, 0))
    full = lambda a: pl.BlockSpec(a.shape, lambda i: (0,) * a.ndim)
    return pl.pallas_call(
        _inproj_kernel,
        grid=(n // tm,),
        in_specs=[row(D_MODEL), full(g1), full(w_in_p), full(bias_p)],
        out_specs=[row(CONV_W), row(CONV_W), row(QK_W), row(QK_W), row(V_W), row(V_W),
                   pl.BlockSpec((GATE_ROWS, tm), lambda i: (0, i))],
        out_shape=[
            jax.ShapeDtypeStruct((n, CONV_W), BF16), jax.ShapeDtypeStruct((n, CONV_W), BF16),
            jax.ShapeDtypeStruct((n, QK_W), BF16), jax.ShapeDtypeStruct((n, QK_W), BF16),
            jax.ShapeDtypeStruct((n, V_W), BF16), jax.ShapeDtypeStruct((n, V_W), BF16),
            jax.ShapeDtypeStruct((GATE_ROWS, n), F32),
        ],
        compiler_params=_cparams("parallel"),
        name="mixer_inproj",
    )(x, g1, w_in_p, bias_p)


def _mlstm_kernel(*refs, seq_len, reverse):
    if reverse:
        q_ref, k_ref, v_ref, grow_ref, hf_ref, os_ref, hng_ref, out_ref, c_ref, m_ref = refs
    else:
        q_ref, k_ref, v_ref, grow_ref, out_ref, c_ref, m_ref = refs
    j = pl.program_id(0)
    nb = pl.num_programs(0)
    blk = MLSTM_BLOCK
    if reverse:
        seq_start = (((nb - j) * blk) % seq_len) == 0
    else:
        seq_start = ((j * blk) % seq_len) == 0

    @pl.when(seq_start)
    def _():
        c_ref[...] = jnp.zeros_like(c_ref)
        m_ref[...] = jnp.zeros_like(m_ref)

    t_i = lax.broadcasted_iota(I32, (CHUNK, CHUNK), 0)
    s_i = lax.broadcasted_iota(I32, (CHUNK, CHUNK), 1)
    tri = (s_i >= t_i) if reverse else (s_i <= t_i)
    lane = lax.broadcasted_iota(I32, (CHUNK, LANES), 1)
    half_masks = [jnp.where((lane >> 6) == hh, 1.0, 0.0).astype(BF16) for hh in range(2)]
    d = 3 if reverse else 0

    def gate_rows(g, cols):
        r0 = (d + g) * GATE_GROUP
        return grow_ref[r0:r0 + N_HEADS, cols]

    n_chunks = blk // CHUNK
    order = list(range(n_chunks - 1, -1, -1) if reverse else range(n_chunks))
    end = 0 if reverse else CHUNK - 1
    m_old = m_ref[0:N_HEADS, :]
    m_in = {}
    for c in order:
        last = slice(c * CHUNK + end, c * CHUNK + end + 1)
        m_in[c] = m_old
        m_old = gate_rows(G_B, last) + jnp.maximum(m_old, gate_rows(G_PM, last))
    m_ref[0:N_HEADS, :] = m_old

    c_state = [c_ref[h] for h in range(N_HEADS)]
    for c in order:
        r0 = c * CHUNK
        rows = slice(r0, r0 + CHUNK)
        m_o = m_in[c]
        a = gate_rows(G_A, rows)
        mm = jnp.maximum(m_o, gate_rows(G_PM, rows))
        mm_last = jnp.maximum(m_o, gate_rows(G_PM, slice(r0 + end, r0 + end + 1)))
        sc = jnp.exp(m_o - mm)
        emt = jnp.exp(-(gate_rows(G_B, rows) + mm))
        w = jnp.exp(a - mm_last)
        decay = jnp.exp(m_o - mm_last)
        stack = jnp.concatenate([mm, sc, emt, w, jnp.zeros((LANES - 4 * N_HEADS, CHUNK), F32)], axis=0)
        cols = stack.T
        for h in range(N_HEADS):
            pair = slice((h // 2) * LANES, (h // 2 + 1) * LANES)
            hv = slice(h * V_DIM, (h + 1) * V_DIM)
            q2 = q_ref[rows, pair]
            km = k_ref[rows, pair] * half_masks[h % 2]
            vh = v_ref[rows, hv]
            c_old = c_state[h]
            mm_col, sc_col = cols[:, h:h + 1], cols[:, N_HEADS + h:N_HEADS + h + 1]
            emt_col, w_col = cols[:, 2 * N_HEADS + h:2 * N_HEADS + h + 1], cols[:, 3 * N_HEADS + h:3 * N_HEADS + h + 1]
            dmat = jnp.where(tri, jnp.exp(a[h:h + 1, :] - mm_col), 0.0)
            s_mat = lax.dot_general(q2, km, (((1,), (1,)), ((), ())), preferred_element_type=F32) * dmat
            qc = jnp.dot(q2, c_old.astype(BF16), preferred_element_type=F32)
            num = jnp.dot(s_mat.astype(BF16), vh, preferred_element_type=F32) + sc_col * qc[:, :V_DIM]
            den = jnp.sum(s_mat, axis=1, keepdims=True) + sc_col * qc[:, V_DIM:V_DIM + 1]
            h_out = num * (1.0 / jnp.maximum(jnp.abs(den), emt_col))
            vext = jnp.concatenate([vh.astype(F32) * w_col, jnp.where(lane == 0, w_col, 0.0)], axis=1).astype(BF16)
            kv = lax.dot_general(km, vext, (((0,), (0,)), ((), ())), preferred_element_type=F32)
            c_state[h] = decay[h:h + 1, 0:1] * c_old + kv
            if reverse:
                ht = hf_ref[rows, hv] + h_out
                ht = _rms(ht, hng_ref[:, hv])
                out_ref[rows, hv] = (os_ref[rows, hv].astype(F32) * ht).astype(BF16)
            else:
                out_ref[rows, hv] = h_out
    for h in range(N_HEADS):
        c_ref[h] = c_state[h]


def mlstm(q, k, v, grow, seq_len, reverse, hf=None, osig=None, hng=None):
    n = q.shape[0]
    blk = MLSTM_BLOCK
    nb = n // blk
    if reverse:
        imap = lambda j: (nb - 1 - j, 0)
        imap_t = lambda j: (0, nb - 1 - j)
    else:
        imap = lambda j: (j, 0)
        imap_t = lambda j: (0, j)
    row = lambda w: pl.BlockSpec((blk, w), imap)
    in_specs = [row(QK_W), row(QK_W), row(V_W), pl.BlockSpec((GATE_ROWS, blk), imap_t)]
    args = [q, k, v, grow]
    if reverse:
        in_specs += [row(V_W), row(V_W), pl.BlockSpec((1, V_W), lambda j: (0, 0))]
        args += [hf, osig, hng]
    return pl.pallas_call(
        functools.partial(_mlstm_kernel, seq_len=seq_len, reverse=reverse),
        grid=(nb,),
        in_specs=in_specs,
        out_specs=row(V_W),
        out_shape=jax.ShapeDtypeStruct((n, V_W), BF16 if reverse else F32),
        scratch_shapes=[pltpu.VMEM((N_HEADS, LANES, 2 * LANES), F32), pltpu.VMEM((8, LANES), F32)],
        compiler_params=_cparams("arbitrary"),
        name="mlstm_bwd" if reverse else "mlstm_fwd",
    )(*args)


HALO = 16


def _outproj_kernel(gb_ref, gcu_ref, gprev_ref, gnext_ref, mo_ref, x_ref, cw_ref, wo_ref, g2_ref, wr_ref,
                    x1_ref, tok_ref, afft_ref, *, seq_len):
    i = pl.program_id(0)
    tm = ROW_TILE
    first = ((i * tm) % seq_len) == 0
    last = (((i + 1) * tm) % seq_len) == 0
    g = gcu_ref[...].astype(F32)
    prev_row = jnp.where(first, 0.0, gprev_ref[HALO - 1:HALO, :].astype(F32))
    next_row = jnp.where(last, 0.0, gnext_ref[0:1, :].astype(F32))
    rid = lax.broadcasted_iota(I32, g.shape, 0)
    dn = jnp.where(rid == 0, prev_row, pltpu.roll(g, 1, axis=0))
    up = jnp.where(rid == tm - 1, next_row, pltpu.roll(g, tm - 1, axis=0))
    conv = dn * cw_ref[0:1, :] + g * cw_ref[1:2, :] + up * cw_ref[2:3, :]
    co = (gb_ref[...].astype(F32) * conv).astype(BF16)
    y = jnp.dot(co, wo_ref[:CONV_W, :], preferred_element_type=F32)
    y = y + jnp.dot(mo_ref[...], wo_ref[CONV_W:, :], preferred_element_type=F32)
    x1 = x_ref[...] + y
    x1_ref[...] = x1
    tokens = _rms(x1, g2_ref[...])
    t_hi = tokens.astype(BF16)
    t_lo = (tokens - t_hi.astype(F32)).astype(BF16)
    p_hi = jnp.dot(t_hi, wr_ref[...], preferred_element_type=F32)
    logits = p_hi[:, :LANES] + p_hi[:, LANES:] + jnp.dot(t_lo, wr_ref[:, :LANES], preferred_element_type=F32)
    lane = lax.broadcasted_iota(I32, logits.shape, 1)
    logits = jnp.where(lane < N_EXPERTS, logits, -jnp.inf)
    ex = jnp.exp(logits - jnp.max(logits, axis=-1, keepdims=True))
    aff = ex / jnp.sum(ex, axis=-1, keepdims=True)
    tok_ref[:, :D_MODEL] = tokens
    row_id = (i * tm + lax.broadcasted_iota(I32, aff.shape, 0)).astype(F32)
    tok_ref[:, D_MODEL:] = jnp.where(lane == TOKEN_ID_LANE, row_id, aff)
    afft_ref[...] = aff.T[:N_EXPERTS]


def outproj(gb, gcu, mo, x, cw_p, w_out_b, g2, wr_p, seq_len):
    n = x.shape[0]
    tm = ROW_TILE
    hb = tm // HALO
    n_halo = n // HALO
    row = lambda w: pl.BlockSpec((tm, w), lambda i: (i, 0))
    full = lambda a: pl.BlockSpec(a.shape, lambda i: (0,) * a.ndim)
    prev = pl.BlockSpec((HALO, CONV_W), lambda i: (jnp.maximum(i * hb - 1, 0), 0))
    nxt = pl.BlockSpec((HALO, CONV_W), lambda i: (jnp.minimum((i + 1) * hb, n_halo - 1), 0))
    return pl.pallas_call(
        functools.partial(_outproj_kernel, seq_len=seq_len),
        grid=(n // tm,),
        in_specs=[row(CONV_W), row(CONV_W), prev, nxt, row(V_W), row(D_MODEL),
                  full(cw_p), full(w_out_b), full(g2), full(wr_p)],
        out_specs=[row(D_MODEL), row(TOK_EXT), pl.BlockSpec((N_EXPERTS, tm), lambda i: (0, i))],
        out_shape=[jax.ShapeDtypeStruct((n, D_MODEL), F32), jax.ShapeDtypeStruct((n, TOK_EXT), F32),
                   jax.ShapeDtypeStruct((N_EXPERTS, n), F32)],
        compiler_params=_cparams("parallel"),
        name="mixer_outproj_router",
    )(gb, gcu, gcu, gcu, mo, x, cw_p, w_out_b, g2, wr_p)


def _ffn_kernel(idx_ref, tok_hbm, wg_ref, wu_ref, wd_ref, ye_ref, xg_ref, wgb_ref, wub_ref, wdb_ref, sem):
    e = pl.program_id(0)
    s = pl.program_id(1)
    n_tiles = pl.num_programs(1)
    ts = SLOT_TILE
    t = e * n_tiles + s
    slot = t % 2

    def row_copy(tile, slot_, i):
        tok = idx_ref[tile * ts + i]
        return pltpu.make_async_copy(tok_hbm.at[pl.ds(tok, 1)], xg_ref.at[slot_, pl.ds(i, 1)], sem.at[slot_])

    @pl.when(t == 0)
    def _():
        def issue(i, carry):
            row_copy(0, 0, i).start()
            return carry

        lax.fori_loop(0, ts, issue, 0, unroll=8)

    @pl.when(s == 0)
    def _():
        wgb_ref[...] = wg_ref[0, 0].astype(BF16)
        wub_ref[...] = wu_ref[0, 0].astype(BF16)
        wdb_ref[...] = wd_ref[0, 0].astype(BF16)

    pltpu.make_async_copy(tok_hbm.at[pl.ds(0, ts)], xg_ref.at[slot], sem.at[slot]).wait()

    @pl.when(t + 1 < N_EXPERTS * n_tiles)
    def _():
        for i in range(ts):
            row_copy(t + 1, 1 - slot, i).start()

    x = xg_ref[slot]
    xb = x[:, :D_MODEL].astype(BF16)
    ext = x[:, D_MODEL:]
    lane = lax.broadcasted_iota(I32, ext.shape, 1)
    gate = jnp.sum(jnp.where(lane == e, ext, 0.0), axis=1, keepdims=True)
    hg = jnp.dot(xb, wgb_ref[...], preferred_element_type=F32)
    hu = jnp.dot(xb, wub_ref[...], preferred_element_type=F32)
    hid = (hg * jax.nn.sigmoid(hg) * hu).astype(BF16)
    ye_ref[:, :D_MODEL] = jnp.dot(hid, wdb_ref[...], preferred_element_type=F32) * gate
    ye_ref[:, D_MODEL:] = ext


def expert_ffn(idx_flat, tok_ext, wg, wu, wd, cap, layer):
    n_tiles = cap // SLOT_TILE
    wspec = lambda: pl.BlockSpec((1, 1, D_MODEL, EXPERT_FF), lambda e, s, *_: (layer, e, 0, 0))
    grid_spec = pltpu.PrefetchScalarGridSpec(
        num_scalar_prefetch=1,
        grid=(N_EXPERTS, n_tiles),
        in_specs=[pl.BlockSpec(memory_space=pl.ANY), wspec(), wspec(),
                  pl.BlockSpec((1, 1, EXPERT_FF, D_MODEL), lambda e, s, *_: (layer, e, 0, 0))],
        out_specs=pl.BlockSpec((SLOT_TILE, TOK_EXT), lambda e, s, *_: (e * n_tiles + s, 0)),
        scratch_shapes=[pltpu.VMEM((2, SLOT_TILE, TOK_EXT), F32),
                        pltpu.VMEM((D_MODEL, EXPERT_FF), BF16), pltpu.VMEM((D_MODEL, EXPERT_FF), BF16),
                        pltpu.VMEM((EXPERT_FF, D_MODEL), BF16), pltpu.SemaphoreType.DMA((2,))],
    )
    return pl.pallas_call(
        _ffn_kernel,
        grid_spec=grid_spec,
        out_shape=jax.ShapeDtypeStruct((N_EXPERTS * cap, TOK_EXT), F32),
        compiler_params=_cparams("arbitrary", "arbitrary"),
        name="expert_ffn",
    )(idx_flat, tok_ext, wg, wu, wd)


WIN = 8
STACK_TILE = 256


def _stack_rows(tc):
    rows = N_EXPERTS * (tc + 2 * (WIN - 1))
    return -(-rows // STACK_TILE) * STACK_TILE


def _combine_kernel(off_ref, x1_ref, ye_hbm, *rest, cap, n_blocks, final):
    if final:
        fg_ref, out_ref, ys_ref, sem = rest
    else:
        out_ref, ys_ref, sem = rest
    j = pl.program_id(0)
    tc = COMBINE_TILE
    buf = j % 2

    def windows(tile, e):
        lo = off_ref[e * (n_blocks + 1) + tile]
        hi = off_ref[e * (n_blocks + 1) + tile + 1]
        start = (lo >> 3) << 3
        return start, jnp.where(hi > lo, (hi - start + (WIN - 1)) >> 3, 0)

    def stacked_rows(tile):
        total = jnp.int32(0)
        for e in range(N_EXPERTS):
            total = total + windows(tile, e)[1] * WIN
        return total

    def fetch(tile, buf_):
        base = jnp.int32(0)
        for e in range(N_EXPERTS):
            start, n_win = windows(tile, e)

            def issue(w, carry, e=e, start=start, base=base):
                src = pl.multiple_of(e * cap + start + w * WIN, WIN)
                dst = pl.multiple_of(base + w * WIN, WIN)
                pltpu.make_async_copy(ye_hbm.at[pl.ds(src, WIN)], ys_ref.at[buf_, pl.ds(dst, WIN)], sem.at[buf_]).start()
                return carry

            lax.fori_loop(0, n_win, issue, 0)
            base = base + n_win * WIN

    @pl.when(j == 0)
    def _():
        ys_ref[...] = jnp.zeros_like(ys_ref)
        fetch(0, 0)

    base = stacked_rows(j)
    n_total = base >> 3
    for bit in range((_stack_rows(tc) // WIN).bit_length()):
        @pl.when((n_total & (1 << bit)) != 0)
        def _(bit=bit):
            rows = WIN << bit
            pltpu.make_async_copy(ye_hbm.at[pl.ds(0, rows)], ye_hbm.at[pl.ds(0, rows)], sem.at[buf]).wait()

    @pl.when(j + 1 < n_blocks)
    def _():
        fetch(j + 1, 1 - buf)

    tok0 = (j * tc).astype(F32)
    lane_t = lax.broadcasted_iota(I32, (STACK_TILE, tc), 1).astype(F32)
    row_i = lax.broadcasted_iota(I32, (STACK_TILE, 1), 0)

    def accumulate(kt, acc):
        r0 = pl.multiple_of(kt * STACK_TILE, STACK_TILE)
        rows = ys_ref[buf, pl.ds(r0, STACK_TILE), :]
        tok_local = rows[:, D_MODEL + TOKEN_ID_LANE:D_MODEL + TOKEN_ID_LANE + 1] - tok0
        tok_local = jnp.where(r0 + row_i < base, tok_local, -1.0)
        onehot = jnp.where(tok_local == lane_t, 1.0, 0.0).astype(BF16)
        return acc + lax.dot_general(onehot, rows[:, :D_MODEL].astype(BF16), (((0,), (0,)), ((), ())),
                                     preferred_element_type=F32)

    n_kt = (base + (STACK_TILE - 1)) // STACK_TILE
    out = lax.fori_loop(0, n_kt, accumulate, x1_ref[...])
    if final:
        out = _rms(out, fg_ref[...])
    out_ref[...] = out


def combine(off_flat, x1, ye, cap, final_g=None):
    n = x1.shape[0]
    tc = COMBINE_TILE
    nb = n // tc
    final = final_g is not None
    in_specs = [pl.BlockSpec((tc, D_MODEL), lambda j, *_: (j, 0)),
                pl.BlockSpec(memory_space=pl.ANY)]
    args = [x1, ye]
    if final:
        in_specs.append(pl.BlockSpec((1, D_MODEL), lambda j, *_: (0, 0)))
        args.append(final_g)
    grid_spec = pltpu.PrefetchScalarGridSpec(
        num_scalar_prefetch=1,
        grid=(nb,),
        in_specs=in_specs,
        out_specs=pl.BlockSpec((tc, D_MODEL), lambda j, *_: (j, 0)),
        scratch_shapes=[pltpu.VMEM((2, _stack_rows(tc), TOK_EXT), F32), pltpu.SemaphoreType.DMA((2,))],
    )
    return pl.pallas_call(
        functools.partial(_combine_kernel, cap=cap, n_blocks=nb, final=final),
        grid_spec=grid_spec,
        out_shape=jax.ShapeDtypeStruct((n, D_MODEL), F32),
        compiler_params=_cparams("arbitrary"),
        name="moe_combine",
    )(off_flat, *args)


def moe(x1, tok_ext, aff_t, wg, wu, wd, layer, final_g=None):
    n = x1.shape[0]
    cap = n // CAPACITY_DIV
    idx, rank = route(aff_t)
    off = jnp.concatenate([rank[:, ::COMBINE_TILE], jnp.full((N_EXPERTS, 1), cap, I32)], axis=1)
    ye = expert_ffn(idx.reshape(-1), tok_ext, wg, wu, wd, cap, layer)
    return combine(off.reshape(-1), x1, ye, cap, final_g)


def kernel(x_prompt, x_sample, norm1_g, w_in, conv_w, gate_bias, head_norm_g, w_out, norm2_g, w_router, w_gate, w_up, w_down, final_g):
    depth = w_in.shape[0]
    w_in_p = jnp.pad(w_in, ((0, 0), (0, 0), (0, D_IN_PAD - D_IN))).astype(BF16)
    bias_p = jnp.pad(gate_bias.reshape(depth, 1, 4 * N_HEADS), ((0, 0), (0, 0), (0, LANES - 4 * N_HEADS)))
    cw_p = jnp.pad(conv_w, ((0, 0), (0, 8 - conv_w.shape[1]), (0, 0)))
    wr_f = jnp.pad(w_router, ((0, 0), (0, 0), (0, LANES - N_EXPERTS)))
    wr_hi = wr_f.astype(BF16)
    wr_p = jnp.concatenate([wr_hi, (wr_f - wr_hi.astype(F32)).astype(BF16)], axis=-1)
    w_out_b = w_out.astype(BF16)
    fg = final_g.reshape(1, D_MODEL)

    outs = []
    for x in (x_prompt, x_sample):
        bsz, seq, _ = x.shape
        xf = x.reshape(bsz * seq, D_MODEL)
        for l in range(depth):
            gb, gcu, q, k, v, osig, grow = inproj(xf, norm1_g[l].reshape(1, -1), w_in_p[l], bias_p[l])
            hf = mlstm(q, k, v, grow, seq, False)
            mo = mlstm(q, k, v, grow, seq, True, hf, osig, head_norm_g[l].reshape(1, -1))
            x1, tok_ext, aff_t = outproj(gb, gcu, mo, xf, cw_p[l], w_out_b[l], norm2_g[l].reshape(1, -1), wr_p[l], seq)
            xf = moe(x1, tok_ext, aff_t, w_gate, w_up, w_down, l, fg if l == depth - 1 else None)
        outs.append(xf.reshape(bsz, seq, D_MODEL))
    return tuple(outs)
```

```python
import functools

import jax
import jax.numpy as jnp
from jax import lax
from jax.experimental import pallas as pl
from jax.experimental.pallas import tpu as pltpu

F32 = jnp.float32
BF16 = jnp.bfloat16
I32 = jnp.int32

LANES = 128
N_EXPERTS = 16
CAPACITY_DIV = 8
DISP_BITS = 16
VALID_BIT = 24
TOKEN_ID_LANE = N_EXPERTS


def _threshold_kernel(aff_ref, thr_ref, need_ref, *, cap):
    bits = pltpu.bitcast(aff_ref[...], I32)
    cap_f = jnp.float32(cap)

    def count_ge(cand):
        return jnp.sum(jnp.where(bits >= cand, 1.0, 0.0), axis=1, keepdims=True)

    def body(i, thr):
        cand = thr | jnp.left_shift(jnp.int32(1), 30 - i)
        return jnp.where(count_ge(cand) >= cap_f, cand, thr)

    thr = lax.fori_loop(0, 31, body, jnp.zeros((N_EXPERTS, 1), I32))
    n_gt = jnp.sum(jnp.where(bits > thr, 1.0, 0.0), axis=1, keepdims=True)
    need = (cap_f - n_gt).astype(I32)
    thr_ref[...] = jnp.broadcast_to(thr, thr_ref.shape)
    need_ref[...] = jnp.broadcast_to(need, need_ref.shape)


def _lane_inclusive_scan(x, lane):
    for b in range(7):
        s = 1 << b
        x = x + jnp.where(lane >= s, pltpu.roll(x, s, axis=1), 0.0)
    return x


def _row_exclusive_scan(t, row, n_rows):
    inc = t
    s = 1
    while s < n_rows:
        inc = inc + jnp.where(row >= s, pltpu.roll(inc, s, axis=0), 0.0)
        s *= 2
    return inc - t


def _token_exclusive_scan(x, lane, row, n_rows):
    inc = _lane_inclusive_scan(x, lane)
    tot = jnp.broadcast_to(inc[:, LANES - 1:LANES], x.shape)
    return inc - x + _row_exclusive_scan(tot, row, n_rows)


def _compact_kernel(thr_ref, need_ref, aff_ref, idx_ref, rank_ref, *, n_rows, cap_rows):
    e = pl.program_id(0)
    shape = (n_rows, LANES)
    lane = lax.broadcasted_iota(I32, shape, 1)
    row = lax.broadcasted_iota(I32, shape, 0)
    bits = pltpu.bitcast(aff_ref[0], I32)
    thr = thr_ref[e]
    need = need_ref[e].astype(F32)
    eq = bits == thr
    pre_eq = _token_exclusive_scan(jnp.where(eq, 1.0, 0.0), lane, row, n_rows)
    sel = (bits > thr) | (eq & (pre_eq < need))
    sel_f = jnp.where(sel, 1.0, 0.0)
    rank = _token_exclusive_scan(sel_f, lane, row, n_rows).astype(I32)
    rank_ref[0] = rank

    pos = row * LANES + lane
    disp = pos - rank
    v = jnp.where(sel, disp | (1 << VALID_BIT), 0)
    n_bits = (n_rows * LANES - 1).bit_length()
    for b in range(n_bits):
        if b < 7:
            s = 1 << b
            r1 = pltpu.roll(v, LANES - s, axis=1)
            r2 = pltpu.roll(r1, n_rows - 1, axis=0)
            moved = jnp.where(lane < LANES - s, r1, r2)
        else:
            sr = 1 << (b - 7)
            moved = pltpu.roll(v, n_rows - sr, axis=0)
        take = ((moved >> VALID_BIT) & 1 == 1) & ((moved >> b) & 1 == 1)
        stay = ((v >> VALID_BIT) & 1 == 1) & ((v >> b) & 1 == 0)
        v = jnp.where(take, moved, jnp.where(stay, v, 0))
    idx_ref[0] = (pos + (v & ((1 << DISP_BITS) - 1)))[:cap_rows]


def route(aff_t):
    n_exp, n = aff_t.shape
    assert n <= (1 << DISP_BITS)
    cap = n // CAPACITY_DIV
    n_rows = n // LANES
    cap_rows = cap // LANES
    thr, need = pl.pallas_call(
        functools.partial(_threshold_kernel, cap=cap),
        out_shape=(jax.ShapeDtypeStruct((n_exp, LANES), I32), jax.ShapeDtypeStruct((n_exp, LANES), I32)),
        name="route_threshold",
    )(aff_t)
    grid_spec = pltpu.PrefetchScalarGridSpec(
        num_scalar_prefetch=2,
        grid=(n_exp,),
        in_specs=[pl.BlockSpec((1, n_rows, LANES), lambda e, *_: (e, 0, 0))],
        out_specs=[
            pl.BlockSpec((1, cap_rows, LANES), lambda e, *_: (e, 0, 0)),
            pl.BlockSpec((1, n_rows, LANES), lambda e, *_: (e, 0, 0)),
        ],
    )
    idx, rank = pl.pallas_call(
        functools.partial(_compact_kernel, n_rows=n_rows, cap_rows=cap_rows),
        grid_spec=grid_spec,
        out_shape=(
            jax.ShapeDtypeStruct((n_exp, cap_rows, LANES), I32),
            jax.ShapeDtypeStruct((n_exp, n_rows, LANES), I32),
        ),
        compiler_params=pltpu.CompilerParams(dimension_semantics=("parallel",)),
        name="route_compact",
    )(thr[:, 0], need[:, 0], aff_t.reshape(n_exp, n_rows, LANES))
    return idx.reshape(n_exp, cap), rank.reshape(n_exp, n)


D_MODEL = 1024
CONV_W = 512
N_HEADS = 4
V_DIM = 128
QK_DIM = 64
QK_W = N_HEADS * QK_DIM
V_W = N_HEADS * V_DIM
CHUNK = 128
EXPERT_FF = 1024
EPS = 1e-6
C_GB, C_GC, C_U, C_Q, C_K, C_V, C_O, C_G = 0, 512, 1024, 1536, 1792, 2048, 2560, 3072
D_IN = 3088
D_IN_PAD = 3200
TOK_EXT = D_MODEL + LANES

ROW_TILE = 512
MLSTM_BLOCK = 256
SLOT_TILE = 512
COMBINE_TILE = 256
VMEM_LIMIT = 56 * 1024 * 1024


def _cparams(*sem):
    return pltpu.CompilerParams(dimension_semantics=sem, vmem_limit_bytes=VMEM_LIMIT)


def _rms(x, g):
    return x * lax.rsqrt(jnp.mean(x * x, axis=-1, keepdims=True) + EPS) * g


GATE_GROUP = 8
GATE_ROWS = 6 * GATE_GROUP
G_A, G_PM, G_B = 0, 1, 2


def _chunk_scan(x, op, identity, reverse):
    width = x.shape[1]
    pos = lax.broadcasted_iota(I32, x.shape, 1) & (CHUNK - 1)
    for b in range(7):
        s = 1 << b
        if reverse:
            shifted = jnp.where(pos < CHUNK - s, pltpu.roll(x, width - s, axis=1), identity)
        else:
            shifted = jnp.where(pos >= s, pltpu.roll(x, s, axis=1), identity)
        x = op(x, shifted)
    return x


def _inproj_kernel(x_ref, g_ref, w_ref, bias_ref, gb_ref, gcu_ref, q_ref, k_ref, v_ref, os_ref, grow_ref):
    hn = _rms(x_ref[...], g_ref[...]).astype(BF16)

    def seg(a, b):
        return jnp.dot(hn, w_ref[:, a:b], preferred_element_type=F32)

    gates = seg(C_G, D_IN_PAD) + bias_ref[...]
    lane = lax.broadcasted_iota(I32, gates.shape, 1)
    log_sig = jnp.minimum(gates, 0.0) - jnp.log1p(jnp.exp(-jnp.abs(gates)))
    g16 = jnp.where((lane >> 2) & 1 == 1, log_sig, gates).T[:4 * N_HEADS]
    fwd = lax.broadcasted_iota(I32, g16.shape, 0) < 2 * N_HEADS
    cs = jnp.where(fwd, _chunk_scan(g16, jnp.add, 0.0, False), _chunk_scan(g16, jnp.add, 0.0, True))
    cs = pltpu.roll(cs, 3 * N_HEADS, axis=0)
    a = g16 - cs
    neg_inf = jnp.float32(-jnp.inf)
    pm = jnp.where(fwd, _chunk_scan(a, jnp.maximum, neg_inf, False), _chunk_scan(a, jnp.maximum, neg_inf, True))
    for d in range(2):
        grp = slice(d * GATE_GROUP, (d + 1) * GATE_GROUP)
        for g, val in ((G_A, a), (G_PM, pm), (G_B, cs)):
            r0 = (3 * d + g) * GATE_GROUP
            grow_ref[r0:r0 + GATE_GROUP, :] = val[grp]

    gb_ref[...] = seg(C_GB, C_GC).astype(BF16)
    gcu_ref[...] = (seg(C_GC, C_U) * seg(C_U, C_Q)).astype(BF16)
    q_ref[...] = (seg(C_Q, C_K) * (QK_DIM ** -0.5)).astype(BF16)
    k_ref[...] = seg(C_K, C_V).astype(BF16)
    v_ref[...] = seg(C_V, C_O).astype(BF16)
    os_ref[...] = jax.nn.sigmoid(seg(C_O, C_G)).astype(BF16)


def inproj(x, g1, w_in_p, bias_p):
    n = x.shape[0]
    tm = ROW_TILE
    row = lambda w: pl.BlockSpec((tm, w), lambda i: (i, 0))
    full = lambda a: pl.BlockSpec(a.shape, lambda i: (0,) * a.ndim)
    return pl.pallas_call(
        _inproj_kernel,
        grid=(n // tm,),
        in_specs=[row(D_MODEL), full(g1), full(w_in_p), full(bias_p)],
        out_specs=[row(CONV_W), row(CONV_W), row(QK_W), row(QK_W), row(V_W), row(V_W),
                   pl.BlockSpec((GATE_ROWS, tm), lambda i: (0, i))],
        out_shape=[
            jax.ShapeDtypeStruct((n, CONV_W), BF16), jax.ShapeDtypeStruct((n, CONV_W), BF16),
            jax.ShapeDtypeStruct((n, QK_W), BF16), jax.ShapeDtypeStruct((n, QK_W), BF16),
            jax.ShapeDtypeStruct((n, V_W), BF16), jax.ShapeDtypeStruct((n, V_W), BF16),
            jax.ShapeDtypeStruct((GATE_ROWS, n), F32),
        ],
        compiler_params=_cparams("parallel"),
        name="mixer_inproj",
    )(x, g1, w_in_p, bias_p)


def _mlstm_kernel(q_ref, k_ref, v_ref, grow_ref, out_ref, c_ref, m_ref, *, seq_len, reverse):
    j = pl.program_id(0)
    nb = pl.num_programs(0)
    blk = MLSTM_BLOCK
    if reverse:
        seq_start = (((nb - j) * blk) % seq_len) == 0
    else:
        seq_start = ((j * blk) % seq_len) == 0

    @pl.when(seq_start)
    def _():
        c_ref[...] = jnp.zeros_like(c_ref)
        m_ref[...] = jnp.zeros_like(m_ref)

    t_i = lax.broadcasted_iota(I32, (CHUNK, CHUNK), 0)
    s_i = lax.broadcasted_iota(I32, (CHUNK, CHUNK), 1)
    tri = (s_i >= t_i) if reverse else (s_i <= t_i)
    lane = lax.broadcasted_iota(I32, (CHUNK, LANES), 1)
    half_masks = [jnp.where((lane >> 6) == hh, 1.0, 0.0).astype(BF16) for hh in range(2)]
    d = 3 if reverse else 0

    def gate_rows(g, cols):
        r0 = (d + g) * GATE_GROUP
        return grow_ref[r0:r0 + N_HEADS, cols]

    n_chunks = blk // CHUNK
    order = list(range(n_chunks - 1, -1, -1) if reverse else range(n_chunks))
    end = 0 if reverse else CHUNK - 1
    m_old = m_ref[0:N_HEADS, :]
    m_in = {}
    for c in order:
        last = slice(c * CHUNK + end, c * CHUNK + end + 1)
        m_in[c] = m_old
        m_old = gate_rows(G_B, last) + jnp.maximum(m_old, gate_rows(G_PM, last))
    m_ref[0:N_HEADS, :] = m_old

    c_state = [c_ref[h] for h in range(N_HEADS)]
    for c in order:
        r0 = c * CHUNK
        rows = slice(r0, r0 + CHUNK)
        m_o = m_in[c]
        a = gate_rows(G_A, rows)
        mm = jnp.maximum(m_o, gate_rows(G_PM, rows))
        mm_last = jnp.maximum(m_o, gate_rows(G_PM, slice(r0 + end, r0 + end + 1)))
        sc = jnp.exp(m_o - mm)
        emt = jnp.exp(-(gate_rows(G_B, rows) + mm))
        w = jnp.exp(a - mm_last)
        decay = jnp.exp(m_o - mm_last)
        stack = jnp.concatenate([mm, sc, emt, w, jnp.zeros((LANES - 4 * N_HEADS, CHUNK), F32)], axis=0)
        cols = stack.T
        for h in range(N_HEADS):
            pair = slice((h // 2) * LANES, (h // 2 + 1) * LANES)
            hv = slice(h * V_DIM, (h + 1) * V_DIM)
            q2 = q_ref[rows, pair]
            km = k_ref[rows, pair] * half_masks[h % 2]
            vh = v_ref[rows, hv]
            c_old = c_state[h]
            mm_col, sc_col = cols[:, h:h + 1], cols[:, N_HEADS + h:N_HEADS + h + 1]
            emt_col, w_col = cols[:, 2 * N_HEADS + h:2 * N_HEADS + h + 1], cols[:, 3 * N_HEADS + h:3 * N_HEADS + h + 1]
            dmat = jnp.where(tri, jnp.exp(a[h:h + 1, :] - mm_col), 0.0)
            s_mat = lax.dot_general(q2, km, (((1,), (1,)), ((), ())), preferred_element_type=F32) * dmat
            qc = jnp.dot(q2, c_old.astype(BF16), preferred_element_type=F32)
            num = jnp.dot(s_mat.astype(BF16), vh, preferred_element_type=F32) + sc_col * qc[:, :V_DIM]
            den = jnp.sum(s_mat, axis=1, keepdims=True) + sc_col * qc[:, V_DIM:V_DIM + 1]
            h_out = num * (1.0 / jnp.maximum(jnp.abs(den), emt_col))
            vext = jnp.concatenate([vh.astype(F32) * w_col, jnp.where(lane == 0, w_col, 0.0)], axis=1).astype(BF16)
            kv = lax.dot_general(km, vext, (((0,), (0,)), ((), ())), preferred_element_type=F32)
            c_state[h] = decay[h:h + 1, 0:1] * c_old + kv
            out_ref[rows, hv] = h_out
    for h in range(N_HEADS):
        c_ref[h] = c_state[h]


def mlstm(q, k, v, grow, seq_len, reverse):
    n = q.shape[0]
    blk = MLSTM_BLOCK
    nb = n // blk
    if reverse:
        imap = lambda j: (nb - 1 - j, 0)
        imap_t = lambda j: (0, nb - 1 - j)
    else:
        imap = lambda j: (j, 0)
        imap_t = lambda j: (0, j)
    row = lambda w: pl.BlockSpec((blk, w), imap)
    return pl.pallas_call(
        functools.partial(_mlstm_kernel, seq_len=seq_len, reverse=reverse),
        grid=(nb,),
        in_specs=[row(QK_W), row(QK_W), row(V_W), pl.BlockSpec((GATE_ROWS, blk), imap_t)],
        out_specs=row(V_W),
        out_shape=jax.ShapeDtypeStruct((n, V_W), F32),
        scratch_shapes=[pltpu.VMEM((N_HEADS, LANES, 2 * LANES), F32), pltpu.VMEM((8, LANES), F32)],
        compiler_params=_cparams("arbitrary"),
        name="mlstm_bwd" if reverse else "mlstm_fwd",
    )(q, k, v, grow)


HALO = 16


def _outproj_kernel(gb_ref, gcu_ref, gprev_ref, gnext_ref, hf_ref, hb_ref, os_ref, hng_ref, x_ref, cw_ref, wo_ref,
                    g2_ref, wr_ref, x1_ref, tok_ref, afft_ref, *, seq_len):
    i = pl.program_id(0)
    tm = ROW_TILE
    first = ((i * tm) % seq_len) == 0
    last = (((i + 1) * tm) % seq_len) == 0
    g = gcu_ref[...].astype(F32)
    prev_row = jnp.where(first, 0.0, gprev_ref[HALO - 1:HALO, :].astype(F32))
    next_row = jnp.where(last, 0.0, gnext_ref[0:1, :].astype(F32))
    rid = lax.broadcasted_iota(I32, g.shape, 0)
    dn = jnp.where(rid == 0, prev_row, pltpu.roll(g, 1, axis=0))
    up = jnp.where(rid == tm - 1, next_row, pltpu.roll(g, tm - 1, axis=0))
    conv = dn * cw_ref[0:1, :] + g * cw_ref[1:2, :] + up * cw_ref[2:3, :]
    co = (gb_ref[...].astype(F32) * conv).astype(BF16)
    y = jnp.dot(co, wo_ref[:CONV_W, :], preferred_element_type=F32)
    heads = []
    for h in range(N_HEADS):
        hv = slice(h * V_DIM, (h + 1) * V_DIM)
        ht = _rms(hf_ref[:, hv] + hb_ref[:, hv], hng_ref[:, hv])
        heads.append((os_ref[:, hv].astype(F32) * ht).astype(BF16))
    mo = jnp.concatenate(heads, axis=1)
    y = y + jnp.dot(mo, wo_ref[CONV_W:, :], preferred_element_type=F32)
    x1 = x_ref[...] + y
    x1_ref[...] = x1
    tokens = _rms(x1, g2_ref[...])
    t_hi = tokens.astype(BF16)
    t_lo = (tokens - t_hi.astype(F32)).astype(BF16)
    p_hi = jnp.dot(t_hi, wr_ref[...], preferred_element_type=F32)
    logits = p_hi[:, :LANES] + p_hi[:, LANES:] + jnp.dot(t_lo, wr_ref[:, :LANES], preferred_element_type=F32)
    lane = lax.broadcasted_iota(I32, logits.shape, 1)
    logits = jnp.where(lane < N_EXPERTS, logits, -jnp.inf)
    ex = jnp.exp(logits - jnp.max(logits, axis=-1, keepdims=True))
    aff = ex / jnp.sum(ex, axis=-1, keepdims=True)
    tok_ref[:, :D_MODEL] = tokens
    row_id = (i * tm + lax.broadcasted_iota(I32, aff.shape, 0)).astype(F32)
    tok_ref[:, D_MODEL:] = jnp.where(lane == TOKEN_ID_LANE, row_id, aff)
    afft_ref[...] = aff.T[:N_EXPERTS]


def outproj(gb, gcu, hf, hb, osig, hng, x, cw_p, w_out_b, g2, wr_p, seq_len):
    n = x.shape[0]
    tm = ROW_TILE
    halos = tm // HALO
    n_halo = n // HALO
    row = lambda w: pl.BlockSpec((tm, w), lambda i: (i, 0))
    full = lambda a: pl.BlockSpec(a.shape, lambda i: (0,) * a.ndim)
    prev = pl.BlockSpec((HALO, CONV_W), lambda i: (jnp.maximum(i * halos - 1, 0), 0))
    nxt = pl.BlockSpec((HALO, CONV_W), lambda i: (jnp.minimum((i + 1) * halos, n_halo - 1), 0))
    return pl.pallas_call(
        functools.partial(_outproj_kernel, seq_len=seq_len),
        grid=(n // tm,),
        in_specs=[row(CONV_W), row(CONV_W), prev, nxt, row(V_W), row(V_W), row(V_W), full(hng), row(D_MODEL),
                  full(cw_p), full(w_out_b), full(g2), full(wr_p)],
        out_specs=[row(D_MODEL), row(TOK_EXT), pl.BlockSpec((N_EXPERTS, tm), lambda i: (0, i))],
        out_shape=[jax.ShapeDtypeStruct((n, D_MODEL), F32), jax.ShapeDtypeStruct((n, TOK_EXT), F32),
                   jax.ShapeDtypeStruct((N_EXPERTS, n), F32)],
        compiler_params=_cparams("parallel"),
        name="mixer_outproj_router",
    )(gb, gcu, gcu, gcu, hf, hb, osig, hng, x, cw_p, w_out_b, g2, wr_p)


def _ffn_kernel(idx_ref, tok_hbm, wg_ref, wu_ref, wd_ref, ye_ref, xg_ref, wgb_ref, wub_ref, wdb_ref, sem):
    e = pl.program_id(0)
    s = pl.program_id(1)
    n_tiles = pl.num_programs(1)
    ts = SLOT_TILE
    t = e * n_tiles + s
    slot = t % 2

    def row_copy(tile, slot_, i):
        tok = idx_ref[tile * ts + i]
        return pltpu.make_async_copy(tok_hbm.at[pl.ds(tok, 1)], xg_ref.at[slot_, pl.ds(i, 1)], sem.at[slot_])

    @pl.when(t == 0)
    def _():
        def issue(i, carry):
            row_copy(0, 0, i).start()
            return carry

        lax.fori_loop(0, ts, issue, 0, unroll=8)

    @pl.when(s == 0)
    def _():
        wgb_ref[...] = wg_ref[0, 0].astype(BF16)
        wub_ref[...] = wu_ref[0, 0].astype(BF16)
        wdb_ref[...] = wd_ref[0, 0].astype(BF16)

    pltpu.make_async_copy(tok_hbm.at[pl.ds(0, ts)], xg_ref.at[slot], sem.at[slot]).wait()

    x = xg_ref[slot]
    xb = x[:, :D_MODEL].astype(BF16)
    ext = x[:, D_MODEL:]
    lane = lax.broadcasted_iota(I32, ext.shape, 1)
    gate = jnp.sum(jnp.where(lane == e, ext, 0.0), axis=1, keepdims=True)

    last = N_EXPERTS * n_tiles - 1
    nxt = jnp.minimum(t + 1, last)
    for i in range(ts):
        row_copy(nxt, 1 - slot, i).start()

    hg = jnp.dot(xb, wgb_ref[...], preferred_element_type=F32)
    hu = jnp.dot(xb, wub_ref[...], preferred_element_type=F32)
    hid = (hg * jax.nn.sigmoid(hg) * hu).astype(BF16)
    ye_ref[:, :D_MODEL] = jnp.dot(hid, wdb_ref[...], preferred_element_type=F32) * gate
    ye_ref[:, D_MODEL:] = ext

    @pl.when(t == last)
    def _():
        pltpu.make_async_copy(tok_hbm.at[pl.ds(0, ts)], xg_ref.at[1 - slot], sem.at[1 - slot]).wait()


def expert_ffn(idx_flat, tok_ext, wg, wu, wd, cap, layer):
    n_tiles = cap // SLOT_TILE
    wspec = lambda: pl.BlockSpec((1, 1, D_MODEL, EXPERT_FF), lambda e, s, *_: (layer, e, 0, 0))
    grid_spec = pltpu.PrefetchScalarGridSpec(
        num_scalar_prefetch=1,
        grid=(N_EXPERTS, n_tiles),
        in_specs=[pl.BlockSpec(memory_space=pl.ANY), wspec(), wspec(),
                  pl.BlockSpec((1, 1, EXPERT_FF, D_MODEL), lambda e, s, *_: (layer, e, 0, 0))],
        out_specs=pl.BlockSpec((SLOT_TILE, TOK_EXT), lambda e, s, *_: (e * n_tiles + s, 0)),
        scratch_shapes=[pltpu.VMEM((2, SLOT_TILE, TOK_EXT), F32),
                        pltpu.VMEM((D_MODEL, EXPERT_FF), BF16), pltpu.VMEM((D_MODEL, EXPERT_FF), BF16),
                        pltpu.VMEM((EXPERT_FF, D_MODEL), BF16), pltpu.SemaphoreType.DMA((2,))],
    )
    return pl.pallas_call(
        _ffn_kernel,
        grid_spec=grid_spec,
        out_shape=jax.ShapeDtypeStruct((N_EXPERTS * cap, TOK_EXT), F32),
        compiler_params=_cparams("arbitrary", "arbitrary"),
        name="expert_ffn",
    )(idx_flat, tok_ext, wg, wu, wd)


WIN = 8
STACK_TILE = 256


def _stack_rows(tc):
    rows = N_EXPERTS * (tc + 2 * (WIN - 1))
    return -(-rows // STACK_TILE) * STACK_TILE


def _combine_kernel(off_ref, x1_ref, ye_hbm, *rest, cap, n_blocks, final):
    if final:
        fg_ref, out_ref, ys_ref, sem = rest
    else:
        out_ref, ys_ref, sem = rest
    j = pl.program_id(0)
    tc = COMBINE_TILE
    buf = j % 2

    def windows(tile, e):
        lo = off_ref[e * (n_blocks + 1) + tile]
        hi = off_ref[e * (n_blocks + 1) + tile + 1]
        start = (lo >> 3) << 3
        return start, jnp.where(hi > lo, (hi - start + (WIN - 1)) >> 3, 0)

    def stacked_rows(tile):
        total = jnp.int32(0)
        for e in range(N_EXPERTS):
            total = total + windows(tile, e)[1] * WIN
        return total

    def fetch(tile, buf_):
        base = jnp.int32(0)
        for e in range(N_EXPERTS):
            start, n_win = windows(tile, e)

            def issue(w, carry, e=e, start=start, base=base):
                src = pl.multiple_of(e * cap + start + w * WIN, WIN)
                dst = pl.multiple_of(base + w * WIN, WIN)
                pltpu.make_async_copy(ye_hbm.at[pl.ds(src, WIN)], ys_ref.at[buf_, pl.ds(dst, WIN)], sem.at[buf_]).start()
                return carry

            lax.fori_loop(0, n_win, issue, 0)
            base = base + n_win * WIN

    @pl.when(j == 0)
    def _():
        ys_ref[...] = jnp.zeros_like(ys_ref)
        fetch(0, 0)

    base = stacked_rows(j)
    n_total = base >> 3
    for bit in range((_stack_rows(tc) // WIN).bit_length()):
        @pl.when((n_total & (1 << bit)) != 0)
        def _(bit=bit):
            rows = WIN << bit
            pltpu.make_async_copy(ye_hbm.at[pl.ds(0, rows)], ye_hbm.at[pl.ds(0, rows)], sem.at[buf]).wait()

    @pl.when(j + 1 < n_blocks)
    def _():
        fetch(j + 1, 1 - buf)

    tok0 = (j * tc).astype(F32)
    lane_t = lax.broadcasted_iota(I32, (STACK_TILE, tc), 1).astype(F32)
    row_i = lax.broadcasted_iota(I32, (STACK_TILE, 1), 0)

    def accumulate(kt, acc):
        r0 = pl.multiple_of(kt * STACK_TILE, STACK_TILE)
        rows = ys_ref[buf, pl.ds(r0, STACK_TILE), :]
        tok_local = rows[:, D_MODEL + TOKEN_ID_LANE:D_MODEL + TOKEN_ID_LANE + 1] - tok0
        tok_local = jnp.where(r0 + row_i < base, tok_local, -1.0)
        onehot = jnp.where(tok_local == lane_t, 1.0, 0.0).astype(BF16)
        out_ref[...] += lax.dot_general(onehot, rows[:, :D_MODEL].astype(BF16), (((0,), (0,)), ((), ())),
                                        preferred_element_type=F32)
        return acc

    n_kt = (base + (STACK_TILE - 1)) // STACK_TILE
    out_ref[...] = x1_ref[...]
    lax.fori_loop(0, n_kt, accumulate, 0)
    if final:
        out_ref[...] = _rms(out_ref[...], fg_ref[...])


def combine(off_flat, x1, ye, cap, final_g=None):
    n = x1.shape[0]
    tc = COMBINE_TILE
    nb = n // tc
    final = final_g is not None
    in_specs = [pl.BlockSpec((tc, D_MODEL), lambda j, *_: (j, 0)),
                pl.BlockSpec(memory_space=pl.ANY)]
    args = [x1, ye]
    if final:
        in_specs.append(pl.BlockSpec((1, D_MODEL), lambda j, *_: (0, 0)))
        args.append(final_g)
    grid_spec = pltpu.PrefetchScalarGridSpec(
        num_scalar_prefetch=1,
        grid=(nb,),
        in_specs=in_specs,
        out_specs=pl.BlockSpec((tc, D_MODEL), lambda j, *_: (j, 0)),
        scratch_shapes=[pltpu.VMEM((2, _stack_rows(tc), TOK_EXT), F32), pltpu.SemaphoreType.DMA((2,))],
    )
    return pl.pallas_call(
        functools.partial(_combine_kernel, cap=cap, n_blocks=nb, final=final),
        grid_spec=grid_spec,
        out_shape=jax.ShapeDtypeStruct((n, D_MODEL), F32),
        compiler_params=_cparams("arbitrary"),
        name="moe_combine",
    )(off_flat, *args)


def moe(x1, tok_ext, aff_t, wg, wu, wd, layer, final_g=None):
    n = x1.shape[0]
    cap = n // CAPACITY_DIV
    idx, rank = route(aff_t)
    off = jnp.concatenate([rank[:, ::COMBINE_TILE], jnp.full((N_EXPERTS, 1), cap, I32)], axis=1)
    ye = expert_ffn(idx.reshape(-1), tok_ext, wg, wu, wd, cap, layer)
    return combine(off.reshape(-1), x1, ye, cap, final_g)


def kernel(x_prompt, x_sample, norm1_g, w_in, conv_w, gate_bias, head_norm_g, w_out, norm2_g, w_router, w_gate, w_up, w_down, final_g):
    depth = w_in.shape[0]
    w_in_p = jnp.pad(w_in, ((0, 0), (0, 0), (0, D_IN_PAD - D_IN))).astype(BF16)
    bias_p = jnp.pad(gate_bias.reshape(depth, 1, 4 * N_HEADS), ((0, 0), (0, 0), (0, LANES - 4 * N_HEADS)))
    cw_p = jnp.pad(conv_w, ((0, 0), (0, 8 - conv_w.shape[1]), (0, 0)))
    wr_f = jnp.pad(w_router, ((0, 0), (0, 0), (0, LANES - N_EXPERTS)))
    wr_hi = wr_f.astype(BF16)
    wr_p = jnp.concatenate([wr_hi, (wr_f - wr_hi.astype(F32)).astype(BF16)], axis=-1)
    w_out_b = w_out.astype(BF16)
    fg = final_g.reshape(1, D_MODEL)

    outs = []
    for x in (x_prompt, x_sample):
        bsz, seq, _ = x.shape
        xf = x.reshape(bsz * seq, D_MODEL)
        for l in range(depth):
            gb, gcu, q, k, v, osig, grow = inproj(xf, norm1_g[l].reshape(1, -1), w_in_p[l], bias_p[l])
            hf = mlstm(q, k, v, grow, seq, False)
            hb = mlstm(q, k, v, grow, seq, True)
            x1, tok_ext, aff_t = outproj(gb, gcu, hf, hb, osig, head_norm_g[l].reshape(1, -1), xf, cw_p[l], w_out_b[l],
                                         norm2_g[l].reshape(1, -1), wr_p[l], seq)
            xf = moe(x1, tok_ext, aff_t, w_gate, w_up, w_down, l, fg if l == depth - 1 else None)
        outs.append(xf.reshape(bsz, seq, D_MODEL))
    return tuple(outs)
```

```python
import functools

import jax
import jax.numpy as jnp
from jax import lax
from jax.experimental import pallas as pl
from jax.experimental.pallas import tpu as pltpu

F32 = jnp.float32
BF16 = jnp.bfloat16
I32 = jnp.int32

LANES = 128
N_EXPERTS = 16
CAPACITY_DIV = 8
DISP_BITS = 16
VALID_BIT = 24
TOKEN_ID_LANE = N_EXPERTS


def _threshold_kernel(aff_ref, thr_ref, need_ref, *, cap):
    bits = pltpu.bitcast(aff_ref[...], I32)
    cap_f = jnp.float32(cap)

    def count_ge(cand):
        return jnp.sum(jnp.where(bits >= cand, 1.0, 0.0), axis=1, keepdims=True)

    def body(i, thr):
        cand = thr | jnp.left_shift(jnp.int32(1), 30 - i)
        return jnp.where(count_ge(cand) >= cap_f, cand, thr)

    thr = lax.fori_loop(0, 31, body, jnp.zeros((N_EXPERTS, 1), I32))
    n_gt = jnp.sum(jnp.where(bits > thr, 1.0, 0.0), axis=1, keepdims=True)
    need = (cap_f - n_gt).astype(I32)
    thr_ref[...] = jnp.broadcast_to(thr, thr_ref.shape)
    need_ref[...] = jnp.broadcast_to(need, need_ref.shape)


def _lane_inclusive_scan(x, lane):
    for b in range(7):
        s = 1 << b
        x = x + jnp.where(lane >= s, pltpu.roll(x, s, axis=1), 0.0)
    return x


def _row_exclusive_scan(t, row, n_rows):
    inc = t
    s = 1
    while s < n_rows:
        inc = inc + jnp.where(row >= s, pltpu.roll(inc, s, axis=0), 0.0)
        s *= 2
    return inc - t


def _token_exclusive_scan(x, lane, row, n_rows):
    inc = _lane_inclusive_scan(x, lane)
    tot = jnp.broadcast_to(inc[:, LANES - 1:LANES], x.shape)
    return inc - x + _row_exclusive_scan(tot, row, n_rows)


def _compact_kernel(thr_ref, need_ref, aff_ref, idx_ref, rank_ref, *, n_rows, cap_rows):
    e = pl.program_id(0)
    shape = (n_rows, LANES)
    lane = lax.broadcasted_iota(I32, shape, 1)
    row = lax.broadcasted_iota(I32, shape, 0)
    bits = pltpu.bitcast(aff_ref[0], I32)
    thr = thr_ref[e]
    need = need_ref[e].astype(F32)
    eq = bits == thr
    pre_eq = _token_exclusive_scan(jnp.where(eq, 1.0, 0.0), lane, row, n_rows)
    sel = (bits > thr) | (eq & (pre_eq < need))
    sel_f = jnp.where(sel, 1.0, 0.0)
    rank = _token_exclusive_scan(sel_f, lane, row, n_rows).astype(I32)
    rank_ref[0] = rank

    pos = row * LANES + lane
    disp = pos - rank
    v = jnp.where(sel, disp | (1 << VALID_BIT), 0)
    n_bits = (n_rows * LANES - 1).bit_length()
    for b in range(n_bits):
        if b < 7:
            s = 1 << b
            r1 = pltpu.roll(v, LANES - s, axis=1)
            r2 = pltpu.roll(r1, n_rows - 1, axis=0)
            moved = jnp.where(lane < LANES - s, r1, r2)
        else:
            sr = 1 << (b - 7)
            moved = pltpu.roll(v, n_rows - sr, axis=0)
        take = ((moved >> VALID_BIT) & 1 == 1) & ((moved >> b) & 1 == 1)
        stay = ((v >> VALID_BIT) & 1 == 1) & ((v >> b) & 1 == 0)
        v = jnp.where(take, moved, jnp.where(stay, v, 0))
    idx_ref[0] = (pos + (v & ((1 << DISP_BITS) - 1)))[:cap_rows]


def route(aff_t):
    n_exp, n = aff_t.shape
    assert n <= (1 << DISP_BITS)
    cap = n // CAPACITY_DIV
    n_rows = n // LANES
    cap_rows = cap // LANES
    thr, need = pl.pallas_call(
        functools.partial(_threshold_kernel, cap=cap),
        out_shape=(jax.ShapeDtypeStruct((n_exp, LANES), I32), jax.ShapeDtypeStruct((n_exp, LANES), I32)),
        name="route_threshold",
    )(aff_t)
    grid_spec = pltpu.PrefetchScalarGridSpec(
        num_scalar_prefetch=2,
        grid=(n_exp,),
        in_specs=[pl.BlockSpec((1, n_rows, LANES), lambda e, *_: (e, 0, 0))],
        out_specs=[
            pl.BlockSpec((1, cap_rows, LANES), lambda e, *_: (e, 0, 0)),
            pl.BlockSpec((1, n_rows, LANES), lambda e, *_: (e, 0, 0)),
        ],
    )
    idx, rank = pl.pallas_call(
        functools.partial(_compact_kernel, n_rows=n_rows, cap_rows=cap_rows),
        grid_spec=grid_spec,
        out_shape=(
            jax.ShapeDtypeStruct((n_exp, cap_rows, LANES), I32),
            jax.ShapeDtypeStruct((n_exp, n_rows, LANES), I32),
        ),
        compiler_params=pltpu.CompilerParams(dimension_semantics=("parallel",)),
        name="route_compact",
    )(thr[:, 0], need[:, 0], aff_t.reshape(n_exp, n_rows, LANES))
    return idx.reshape(n_exp, cap), rank.reshape(n_exp, n)


D_MODEL = 1024
CONV_W = 512
N_HEADS = 4
V_DIM = 128
QK_DIM = 64
QK_W = N_HEADS * QK_DIM
V_W = N_HEADS * V_DIM
CHUNK = 128
EXPERT_FF = 1024
EPS = 1e-6
C_GB, C_GC, C_U, C_Q, C_K, C_V, C_O, C_G = 0, 512, 1024, 1536, 1792, 2048, 2560, 3072
D_IN = 3088
D_IN_PAD = 3200
TOK_EXT = D_MODEL + LANES

ROW_TILE = 512
MLSTM_BLOCK = 256
SLOT_TILE = 512
COMBINE_TILE = 256
VMEM_LIMIT = 56 * 1024 * 1024


def _cparams(*sem):
    return pltpu.CompilerParams(dimension_semantics=sem, vmem_limit_bytes=VMEM_LIMIT)


def _rms(x, g):
    return x * lax.rsqrt(jnp.mean(x * x, axis=-1, keepdims=True) + EPS) * g


GATE_GROUP = 8
GATE_ROWS = 6 * GATE_GROUP
G_A, G_PM, G_B = 0, 1, 2


def _chunk_scan(x, op, identity, reverse):
    width = x.shape[1]
    pos = lax.broadcasted_iota(I32, x.shape, 1) & (CHUNK - 1)
    for b in range(7):
        s = 1 << b
        if reverse:
            shifted = jnp.where(pos < CHUNK - s, pltpu.roll(x, width - s, axis=1), identity)
        else:
            shifted = jnp.where(pos >= s, pltpu.roll(x, s, axis=1), identity)
        x = op(x, shifted)
    return x


def _inproj_kernel(x_ref, g_ref, w_ref, bias_ref, gb_ref, gcu_ref, q_ref, k_ref, v_ref, os_ref, grow_ref):
    hn = _rms(x_ref[...], g_ref[...]).astype(BF16)

    def seg(a, b):
        return jnp.dot(hn, w_ref[:, a:b], preferred_element_type=F32)

    gates = seg(C_G, D_IN_PAD) + bias_ref[...]
    lane = lax.broadcasted_iota(I32, gates.shape, 1)
    log_sig = jnp.minimum(gates, 0.0) - jnp.log1p(jnp.exp(-jnp.abs(gates)))
    g16 = jnp.where((lane >> 2) & 1 == 1, log_sig, gates).T[:4 * N_HEADS]
    fwd = lax.broadcasted_iota(I32, g16.shape, 0) < 2 * N_HEADS
    cs = jnp.where(fwd, _chunk_scan(g16, jnp.add, 0.0, False), _chunk_scan(g16, jnp.add, 0.0, True))
    cs = pltpu.roll(cs, 3 * N_HEADS, axis=0)
    a = g16 - cs
    neg_inf = jnp.float32(-jnp.inf)
    pm = jnp.where(fwd, _chunk_scan(a, jnp.maximum, neg_inf, False), _chunk_scan(a, jnp.maximum, neg_inf, True))
    for d in range(2):
        grp = slice(d * GATE_GROUP, (d + 1) * GATE_GROUP)
        for g, val in ((G_A, a), (G_PM, pm), (G_B, cs)):
            r0 = (3 * d + g) * GATE_GROUP
            grow_ref[r0:r0 + GATE_GROUP, :] = val[grp]

    gb_ref[...] = seg(C_GB, C_GC).astype(BF16)
    gcu_ref[...] = (seg(C_GC, C_U) * seg(C_U, C_Q)).astype(BF16)
    q_ref[...] = (seg(C_Q, C_K) * (QK_DIM ** -0.5)).astype(BF16)
    k_ref[...] = seg(C_K, C_V).astype(BF16)
    v_ref[...] = seg(C_V, C_O).astype(BF16)
    os_ref[...] = jax.nn.sigmoid(seg(C_O, C_G)).astype(BF16)


def inproj(x, g1, w_in_p, bias_p):
    n = x.shape[0]
    tm = ROW_TILE
    row = lambda w: pl.BlockSpec((tm, w), lambda i: (i, 0))
    full = lambda a: pl.BlockSpec(a.shape, lambda i: (0,) * a.ndim)
    return pl.pallas_call(
        _inproj_kernel,
        grid=(n // tm,),
        in_specs=[row(D_MODEL), full(g1), full(w_in_p), full(bias_p)],
        out_specs=[row(CONV_W), row(CONV_W), row(QK_W), row(QK_W), row(V_W), row(V_W),
                   pl.BlockSpec((GATE_ROWS, tm), lambda i: (0, i))],
        out_shape=[
            jax.ShapeDtypeStruct((n, CONV_W), BF16), jax.ShapeDtypeStruct((n, CONV_W), BF16),
            jax.ShapeDtypeStruct((n, QK_W), BF16), jax.ShapeDtypeStruct((n, QK_W), BF16),
            jax.ShapeDtypeStruct((n, V_W), BF16), jax.ShapeDtypeStruct((n, V_W), BF16),
            jax.ShapeDtypeStruct((GATE_ROWS, n), F32),
        ],
        compiler_params=_cparams("parallel"),
        name="mixer_inproj",
    )(x, g1, w_in_p, bias_p)


def _mlstm_kernel(qf_ref, kf_ref, vf_ref, gf_ref, qb_ref, kb_ref, vb_ref, gb_ref, hf_ref, hb_ref,
                  cf_ref, mf_ref, cb_ref, mb_ref, *, seq_len):
    j = pl.program_id(0)

    @pl.when(((j * MLSTM_BLOCK) % seq_len) == 0)
    def _():
        for ref in (cf_ref, mf_ref, cb_ref, mb_ref):
            ref[...] = jnp.zeros_like(ref)

    _mlstm_block(qf_ref, kf_ref, vf_ref, gf_ref, hf_ref, cf_ref, mf_ref, reverse=False)
    _mlstm_block(qb_ref, kb_ref, vb_ref, gb_ref, hb_ref, cb_ref, mb_ref, reverse=True)


def _mlstm_block(q_ref, k_ref, v_ref, grow_ref, out_ref, c_ref, m_ref, *, reverse):
    blk = MLSTM_BLOCK
    t_i = lax.broadcasted_iota(I32, (CHUNK, CHUNK), 0)
    s_i = lax.broadcasted_iota(I32, (CHUNK, CHUNK), 1)
    tri = (s_i >= t_i) if reverse else (s_i <= t_i)
    lane = lax.broadcasted_iota(I32, (CHUNK, LANES), 1)
    half_masks = [jnp.where((lane >> 6) == hh, 1.0, 0.0).astype(BF16) for hh in range(2)]
    d = 3 if reverse else 0

    def gate_rows(g, cols):
        r0 = (d + g) * GATE_GROUP
        return grow_ref[r0:r0 + N_HEADS, cols]

    n_chunks = blk // CHUNK
    order = list(range(n_chunks - 1, -1, -1) if reverse else range(n_chunks))
    end = 0 if reverse else CHUNK - 1
    m_old = m_ref[0:N_HEADS, :]
    m_in = {}
    for c in order:
        last = slice(c * CHUNK + end, c * CHUNK + end + 1)
        m_in[c] = m_old
        m_old = gate_rows(G_B, last) + jnp.maximum(m_old, gate_rows(G_PM, last))
    m_ref[0:N_HEADS, :] = m_old

    c_state = [c_ref[h] for h in range(N_HEADS)]
    for c in order:
        r0 = c * CHUNK
        rows = slice(r0, r0 + CHUNK)
        m_o = m_in[c]
        a = gate_rows(G_A, rows)
        mm = jnp.maximum(m_o, gate_rows(G_PM, rows))
        mm_last = jnp.maximum(m_o, gate_rows(G_PM, slice(r0 + end, r0 + end + 1)))
        sc = jnp.exp(m_o - mm)
        emt = jnp.exp(-(gate_rows(G_B, rows) + mm))
        w = jnp.exp(a - mm_last)
        decay = jnp.exp(m_o - mm_last)
        stack = jnp.concatenate([mm, sc, emt, w, jnp.zeros((LANES - 4 * N_HEADS, CHUNK), F32)], axis=0)
        cols = stack.T
        for h in range(N_HEADS):
            pair = slice((h // 2) * LANES, (h // 2 + 1) * LANES)
            hv = slice(h * V_DIM, (h + 1) * V_DIM)
            q2 = q_ref[rows, pair]
            km = k_ref[rows, pair] * half_masks[h % 2]
            vh = v_ref[rows, hv]
            c_old = c_state[h]
            mm_col, sc_col = cols[:, h:h + 1], cols[:, N_HEADS + h:N_HEADS + h + 1]
            emt_col, w_col = cols[:, 2 * N_HEADS + h:2 * N_HEADS + h + 1], cols[:, 3 * N_HEADS + h:3 * N_HEADS + h + 1]
            dmat = jnp.where(tri, jnp.exp(a[h:h + 1, :] - mm_col), 0.0)
            s_mat = lax.dot_general(q2, km, (((1,), (1,)), ((), ())), preferred_element_type=F32) * dmat
            qc = jnp.dot(q2, c_old.astype(BF16), preferred_element_type=F32)
            num = jnp.dot(s_mat.astype(BF16), vh, preferred_element_type=F32) + sc_col * qc[:, :V_DIM]
            den = jnp.sum(s_mat, axis=1, keepdims=True) + sc_col * qc[:, V_DIM:V_DIM + 1]
            h_out = num * (1.0 / jnp.maximum(jnp.abs(den), emt_col))
            vext = jnp.concatenate([vh.astype(F32) * w_col, jnp.where(lane == 0, w_col, 0.0)], axis=1).astype(BF16)
            kv = lax.dot_general(km, vext, (((0,), (0,)), ((), ())), preferred_element_type=F32)
            c_state[h] = decay[h:h + 1, 0:1] * c_old + kv
            out_ref[rows, hv] = h_out
    for h in range(N_HEADS):
        c_ref[h] = c_state[h]


def mlstm(q, k, v, grow, seq_len):
    n = q.shape[0]
    blk = MLSTM_BLOCK
    nb = n // blk
    assert n % seq_len == 0 and seq_len % blk == 0
    fwd = lambda w: pl.BlockSpec((blk, w), lambda j: (j, 0))
    bwd = lambda w: pl.BlockSpec((blk, w), lambda j: (nb - 1 - j, 0))
    state = [pltpu.VMEM((N_HEADS, LANES, 2 * LANES), F32), pltpu.VMEM((8, LANES), F32)]
    return pl.pallas_call(
        functools.partial(_mlstm_kernel, seq_len=seq_len),
        grid=(nb,),
        in_specs=[fwd(QK_W), fwd(QK_W), fwd(V_W), pl.BlockSpec((GATE_ROWS, blk), lambda j: (0, j)),
                  bwd(QK_W), bwd(QK_W), bwd(V_W), pl.BlockSpec((GATE_ROWS, blk), lambda j: (0, nb - 1 - j))],
        out_specs=[fwd(V_W), bwd(V_W)],
        out_shape=[jax.ShapeDtypeStruct((n, V_W), F32), jax.ShapeDtypeStruct((n, V_W), F32)],
        scratch_shapes=state + state,
        compiler_params=_cparams("arbitrary"),
        name="mlstm",
    )(q, k, v, grow, q, k, v, grow)


HALO = 16


def _outproj_kernel(gb_ref, gcu_ref, gprev_ref, gnext_ref, hf_ref, hb_ref, os_ref, hng_ref, x_ref, cw_ref, wo_ref,
                    g2_ref, wr_ref, x1_ref, tok_ref, afft_ref, *, seq_len):
    i = pl.program_id(0)
    tm = ROW_TILE
    first = ((i * tm) % seq_len) == 0
    last = (((i + 1) * tm) % seq_len) == 0
    g = gcu_ref[...].astype(F32)
    prev_row = jnp.where(first, 0.0, gprev_ref[HALO - 1:HALO, :].astype(F32))
    next_row = jnp.where(last, 0.0, gnext_ref[0:1, :].astype(F32))
    rid = lax.broadcasted_iota(I32, g.shape, 0)
    dn = jnp.where(rid == 0, prev_row, pltpu.roll(g, 1, axis=0))
    up = jnp.where(rid == tm - 1, next_row, pltpu.roll(g, tm - 1, axis=0))
    conv = dn * cw_ref[0:1, :] + g * cw_ref[1:2, :] + up * cw_ref[2:3, :]
    co = (gb_ref[...].astype(F32) * conv).astype(BF16)
    y = jnp.dot(co, wo_ref[:CONV_W, :], preferred_element_type=F32)
    heads = []
    for h in range(N_HEADS):
        hv = slice(h * V_DIM, (h + 1) * V_DIM)
        ht = _rms(hf_ref[:, hv] + hb_ref[:, hv], hng_ref[:, hv])
        heads.append((os_ref[:, hv].astype(F32) * ht).astype(BF16))
    mo = jnp.concatenate(heads, axis=1)
    y = y + jnp.dot(mo, wo_ref[CONV_W:, :], preferred_element_type=F32)
    x1 = x_ref[...] + y
    x1_ref[...] = x1
    tokens = _rms(x1, g2_ref[...])
    t_hi = tokens.astype(BF16)
    t_lo = (tokens - t_hi.astype(F32)).astype(BF16)
    p_hi = jnp.dot(t_hi, wr_ref[...], preferred_element_type=F32)
    logits = p_hi[:, :LANES] + p_hi[:, LANES:] + jnp.dot(t_lo, wr_ref[:, :LANES], preferred_element_type=F32)
    lane = lax.broadcasted_iota(I32, logits.shape, 1)
    logits = jnp.where(lane < N_EXPERTS, logits, -jnp.inf)
    ex = jnp.exp(logits - jnp.max(logits, axis=-1, keepdims=True))
    aff = ex / jnp.sum(ex, axis=-1, keepdims=True)
    tok_ref[:, :D_MODEL] = tokens
    row_id = (i * tm + lax.broadcasted_iota(I32, aff.shape, 0)).astype(F32)
    tok_ref[:, D_MODEL:] = jnp.where(lane == TOKEN_ID_LANE, row_id, aff)
    afft_ref[...] = aff.T[:N_EXPERTS]


def outproj(gb, gcu, hf, hb, osig, hng, x, cw_p, w_out_b, g2, wr_p, seq_len):
    n = x.shape[0]
    tm = ROW_TILE
    halos = tm // HALO
    n_halo = n // HALO
    row = lambda w: pl.BlockSpec((tm, w), lambda i: (i, 0))
    full = lambda a: pl.BlockSpec(a.shape, lambda i: (0,) * a.ndim)
    prev = pl.BlockSpec((HALO, CONV_W), lambda i: (jnp.maximum(i * halos - 1, 0), 0))
    nxt = pl.BlockSpec((HALO, CONV_W), lambda i: (jnp.minimum((i + 1) * halos, n_halo - 1), 0))
    return pl.pallas_call(
        functools.partial(_outproj_kernel, seq_len=seq_len),
        grid=(n // tm,),
        in_specs=[row(CONV_W), row(CONV_W), prev, nxt, row(V_W), row(V_W), row(V_W), full(hng), row(D_MODEL),
                  full(cw_p), full(w_out_b), full(g2), full(wr_p)],
        out_specs=[row(D_MODEL), row(TOK_EXT), pl.BlockSpec((N_EXPERTS, tm), lambda i: (0, i))],
        out_shape=[jax.ShapeDtypeStruct((n, D_MODEL), F32), jax.ShapeDtypeStruct((n, TOK_EXT), F32),
                   jax.ShapeDtypeStruct((N_EXPERTS, n), F32)],
        compiler_params=_cparams("parallel"),
        name="mixer_outproj_router",
    )(gb, gcu, gcu, gcu, hf, hb, osig, hng, x, cw_p, w_out_b, g2, wr_p)


def _ffn_kernel(idx_ref, tok_hbm, wg_ref, wu_ref, wd_ref, ye_ref, xg_ref, wgb_ref, wub_ref, wdb_ref, sem):
    e = pl.program_id(0)
    s = pl.program_id(1)
    n_tiles = pl.num_programs(1)
    ts = SLOT_TILE
    t = e * n_tiles + s
    slot = t % 2

    def row_copy(tile, slot_, i):
        tok = idx_ref[tile * ts + i]
        return pltpu.make_async_copy(tok_hbm.at[pl.ds(tok, 1)], xg_ref.at[slot_, pl.ds(i, 1)], sem.at[slot_])

    @pl.when(t == 0)
    def _():
        def issue(i, carry):
            row_copy(0, 0, i).start()
            return carry

        lax.fori_loop(0, ts, issue, 0, unroll=8)

    @pl.when(s == 0)
    def _():
        wgb_ref[...] = wg_ref[0, 0].astype(BF16)
        wub_ref[...] = wu_ref[0, 0].astype(BF16)
        wdb_ref[...] = wd_ref[0, 0].astype(BF16)

    pltpu.make_async_copy(tok_hbm.at[pl.ds(0, ts)], xg_ref.at[slot], sem.at[slot]).wait()

    x = xg_ref[slot]
    xb = x[:, :D_MODEL].astype(BF16)
    ext = x[:, D_MODEL:]
    lane = lax.broadcasted_iota(I32, ext.shape, 1)
    gate = jnp.sum(jnp.where(lane == e, ext, 0.0), axis=1, keepdims=True)

    last = N_EXPERTS * n_tiles - 1
    nxt = jnp.minimum(t + 1, last)
    for i in range(ts):
        row_copy(nxt, 1 - slot, i).start(priority=i % 2)

    hg = jnp.dot(xb, wgb_ref[...], preferred_element_type=F32)
    hu = jnp.dot(xb, wub_ref[...], preferred_element_type=F32)
    hid = (hg * jax.nn.sigmoid(hg) * hu).astype(BF16)
    ye_ref[:, :D_MODEL] = jnp.dot(hid, wdb_ref[...], preferred_element_type=F32) * gate
    ye_ref[:, D_MODEL:] = ext

    @pl.when(t == last)
    def _():
        pltpu.make_async_copy(tok_hbm.at[pl.ds(0, ts)], xg_ref.at[1 - slot], sem.at[1 - slot]).wait()


def expert_ffn(idx_flat, tok_ext, wg, wu, wd, cap, layer):
    n_tiles = cap // SLOT_TILE
    wspec = lambda: pl.BlockSpec((1, 1, D_MODEL, EXPERT_FF), lambda e, s, *_: (layer, e, 0, 0))
    grid_spec = pltpu.PrefetchScalarGridSpec(
        num_scalar_prefetch=1,
        grid=(N_EXPERTS, n_tiles),
        in_specs=[pl.BlockSpec(memory_space=pl.ANY), wspec(), wspec(),
                  pl.BlockSpec((1, 1, EXPERT_FF, D_MODEL), lambda e, s, *_: (layer, e, 0, 0))],
        out_specs=pl.BlockSpec((SLOT_TILE, TOK_EXT), lambda e, s, *_: (e * n_tiles + s, 0)),
        scratch_shapes=[pltpu.VMEM((2, SLOT_TILE, TOK_EXT), F32),
                        pltpu.VMEM((D_MODEL, EXPERT_FF), BF16), pltpu.VMEM((D_MODEL, EXPERT_FF), BF16),
                        pltpu.VMEM((EXPERT_FF, D_MODEL), BF16), pltpu.SemaphoreType.DMA((2,))],
    )
    return pl.pallas_call(
        _ffn_kernel,
        grid_spec=grid_spec,
        out_shape=jax.ShapeDtypeStruct((N_EXPERTS * cap, TOK_EXT), F32),
        compiler_params=_cparams("arbitrary", "arbitrary"),
        name="expert_ffn",
    )(idx_flat, tok_ext, wg, wu, wd)


WIN = 8
STACK_TILE = 256


def _stack_rows(tc):
    rows = N_EXPERTS * (tc + 2 * (WIN - 1))
    return -(-rows // STACK_TILE) * STACK_TILE


def _combine_kernel(off_ref, x1_ref, ye_hbm, *rest, cap, n_blocks, final):
    if final:
        fg_ref, out_ref, ys_ref, sem = rest
    else:
        out_ref, ys_ref, sem = rest
    j = pl.program_id(0)
    tc = COMBINE_TILE
    buf = j % 2

    def windows(tile, e):
        lo = off_ref[e * (n_blocks + 1) + tile]
        hi = off_ref[e * (n_blocks + 1) + tile + 1]
        start = (lo >> 3) << 3
        return start, jnp.where(hi > lo, (hi - start + (WIN - 1)) >> 3, 0)

    def stacked_rows(tile):
        total = jnp.int32(0)
        for e in range(N_EXPERTS):
            total = total + windows(tile, e)[1] * WIN
        return total

    def fetch(tile, buf_):
        base = jnp.int32(0)
        for e in range(N_EXPERTS):
            start, n_win = windows(tile, e)

            def issue(w, carry, e=e, start=start, base=base):
                src = pl.multiple_of(e * cap + start + w * WIN, WIN)
                dst = pl.multiple_of(base + w * WIN, WIN)
                pltpu.make_async_copy(ye_hbm.at[pl.ds(src, WIN)], ys_ref.at[buf_, pl.ds(dst, WIN)], sem.at[buf_]).start()
                return carry

            lax.fori_loop(0, n_win, issue, 0)
            base = base + n_win * WIN

    @pl.when(j == 0)
    def _():
        ys_ref[...] = jnp.zeros_like(ys_ref)
        fetch(0, 0)

    base = stacked_rows(j)
    n_total = base >> 3
    for bit in range((_stack_rows(tc) // WIN).bit_length()):
        @pl.when((n_total & (1 << bit)) != 0)
        def _(bit=bit):
            rows = WIN << bit
            pltpu.make_async_copy(ye_hbm.at[pl.ds(0, rows)], ye_hbm.at[pl.ds(0, rows)], sem.at[buf]).wait()

    @pl.when(j + 1 < n_blocks)
    def _():
        fetch(j + 1, 1 - buf)

    tok0 = (j * tc).astype(F32)
    lane_t = lax.broadcasted_iota(I32, (STACK_TILE, tc), 1).astype(F32)
    row_i = lax.broadcasted_iota(I32, (STACK_TILE, 1), 0)

    def accumulate(kt, acc):
        r0 = pl.multiple_of(kt * STACK_TILE, STACK_TILE)
        rows = ys_ref[buf, pl.ds(r0, STACK_TILE), :]
        tok_local = rows[:, D_MODEL + TOKEN_ID_LANE:D_MODEL + TOKEN_ID_LANE + 1] - tok0
        tok_local = jnp.where(r0 + row_i < base, tok_local, -1.0)
        onehot = jnp.where(tok_local == lane_t, 1.0, 0.0).astype(BF16)
        out_ref[...] += lax.dot_general(onehot, rows[:, :D_MODEL].astype(BF16), (((0,), (0,)), ((), ())),
                                        preferred_element_type=F32)
        return acc

    n_kt = (base + (STACK_TILE - 1)) // STACK_TILE
    out_ref[...] = x1_ref[...]
    lax.fori_loop(0, n_kt, accumulate, 0)
    if final:
        out_ref[...] = _rms(out_ref[...], fg_ref[...])


def combine(off_flat, x1, ye, cap, final_g=None):
    n = x1.shape[0]
    tc = COMBINE_TILE
    nb = n // tc
    final = final_g is not None
    in_specs = [pl.BlockSpec((tc, D_MODEL), lambda j, *_: (j, 0)),
                pl.BlockSpec(memory_space=pl.ANY)]
    args = [x1, ye]
    if final:
        in_specs.append(pl.BlockSpec((1, D_MODEL), lambda j, *_: (0, 0)))
        args.append(final_g)
    grid_spec = pltpu.PrefetchScalarGridSpec(
        num_scalar_prefetch=1,
        grid=(nb,),
        in_specs=in_specs,
        out_specs=pl.BlockSpec((tc, D_MODEL), lambda j, *_: (j, 0)),
        scratch_shapes=[pltpu.VMEM((2, _stack_rows(tc), TOK_EXT), F32), pltpu.SemaphoreType.DMA((2,))],
    )
    return pl.pallas_call(
        functools.partial(_combine_kernel, cap=cap, n_blocks=nb, final=final),
        grid_spec=grid_spec,
        out_shape=jax.ShapeDtypeStruct((n, D_MODEL), F32),
        compiler_params=_cparams("arbitrary"),
        name="moe_combine",
    )(off_flat, *args)


def moe(x1, tok_ext, aff_t, wg, wu, wd, layer, final_g=None):
    n = x1.shape[0]
    cap = n // CAPACITY_DIV
    idx, rank = route(aff_t)
    off = jnp.concatenate([rank[:, ::COMBINE_TILE], jnp.full((N_EXPERTS, 1), cap, I32)], axis=1)
    ye = expert_ffn(idx.reshape(-1), tok_ext, wg, wu, wd, cap, layer)
    return combine(off.reshape(-1), x1, ye, cap, final_g)


def kernel(x_prompt, x_sample, norm1_g, w_in, conv_w, gate_bias, head_norm_g, w_out, norm2_g, w_router, w_gate, w_up, w_down, final_g):
    depth = w_in.shape[0]
    w_in_p = jnp.pad(w_in, ((0, 0), (0, 0), (0, D_IN_PAD - D_IN))).astype(BF16)
    bias_p = jnp.pad(gate_bias.reshape(depth, 1, 4 * N_HEADS), ((0, 0), (0, 0), (0, LANES - 4 * N_HEADS)))
    cw_p = jnp.pad(conv_w, ((0, 0), (0, 8 - conv_w.shape[1]), (0, 0)))
    wr_f = jnp.pad(w_router, ((0, 0), (0, 0), (0, LANES - N_EXPERTS)))
    wr_hi = wr_f.astype(BF16)
    wr_p = jnp.concatenate([wr_hi, (wr_f - wr_hi.astype(F32)).astype(BF16)], axis=-1)
    w_out_b = w_out.astype(BF16)
    fg = final_g.reshape(1, D_MODEL)

    outs = []
    for x in (x_prompt, x_sample):
        bsz, seq, _ = x.shape
        xf = x.reshape(bsz * seq, D_MODEL)
        for l in range(depth):
            gb, gcu, q, k, v, osig, grow = inproj(xf, norm1_g[l].reshape(1, -1), w_in_p[l], bias_p[l])
            hf, hb = mlstm(q, k, v, grow, seq)
            x1, tok_ext, aff_t = outproj(gb, gcu, hf, hb, osig, head_norm_g[l].reshape(1, -1), xf, cw_p[l], w_out_b[l],
                                         norm2_g[l].reshape(1, -1), wr_p[l], seq)
            xf = moe(x1, tok_ext, aff_t, w_gate, w_up, w_down, l, fg if l == depth - 1 else None)
        outs.append(xf.reshape(bsz, seq, D_MODEL))
    return tuple(outs)
```

```python
import functools

import jax
import jax.numpy as jnp
from jax import lax
from jax.experimental import pallas as pl
from jax.experimental.pallas import tpu as pltpu

F32 = jnp.float32
BF16 = jnp.bfloat16
I32 = jnp.int32

LANES = 128
N_EXPERTS = 16
CAPACITY_DIV = 8
DISP_BITS = 16
VALID_BIT = 24
TOKEN_ID_LANE = N_EXPERTS


def _threshold_kernel(aff_ref, thr_ref, need_ref, *, cap):
    bits = pltpu.bitcast(aff_ref[...], I32)
    cap_f = jnp.float32(cap)

    def count_ge(cand):
        return jnp.sum(jnp.where(bits >= cand, 1.0, 0.0), axis=1, keepdims=True)

    def body(i, thr):
        cand = thr | jnp.left_shift(jnp.int32(1), 30 - i)
        return jnp.where(count_ge(cand) >= cap_f, cand, thr)

    thr = lax.fori_loop(0, 31, body, jnp.zeros((N_EXPERTS, 1), I32))
    n_gt = jnp.sum(jnp.where(bits > thr, 1.0, 0.0), axis=1, keepdims=True)
    need = (cap_f - n_gt).astype(I32)
    thr_ref[...] = jnp.broadcast_to(thr, thr_ref.shape)
    need_ref[...] = jnp.broadcast_to(need, need_ref.shape)


def _lane_inclusive_scan(x, lane):
    for b in range(7):
        s = 1 << b
        x = x + jnp.where(lane >= s, pltpu.roll(x, s, axis=1), 0.0)
    return x


def _row_exclusive_scan(t, row, n_rows):
    inc = t
    s = 1
    while s < n_rows:
        inc = inc + jnp.where(row >= s, pltpu.roll(inc, s, axis=0), 0.0)
        s *= 2
    return inc - t


def _token_exclusive_scan(x, lane, row, n_rows):
    inc = _lane_inclusive_scan(x, lane)
    tot = jnp.broadcast_to(inc[:, LANES - 1:LANES], x.shape)
    return inc - x + _row_exclusive_scan(tot, row, n_rows)


def _compact_kernel(thr_ref, need_ref, aff_ref, idx_ref, rank_ref, *, n_rows, cap_rows):
    e = pl.program_id(0)
    shape = (n_rows, LANES)
    lane = lax.broadcasted_iota(I32, shape, 1)
    row = lax.broadcasted_iota(I32, shape, 0)
    bits = pltpu.bitcast(aff_ref[0], I32)
    thr = thr_ref[e]
    need = need_ref[e].astype(F32)
    eq = bits == thr
    pre_eq = _token_exclusive_scan(jnp.where(eq, 1.0, 0.0), lane, row, n_rows)
    sel = (bits > thr) | (eq & (pre_eq < need))
    sel_f = jnp.where(sel, 1.0, 0.0)
    rank = _token_exclusive_scan(sel_f, lane, row, n_rows).astype(I32)
    rank_ref[0] = rank

    pos = row * LANES + lane
    disp = pos - rank
    v = jnp.where(sel, disp | (1 << VALID_BIT), 0)
    n_bits = (n_rows * LANES - 1).bit_length()
    for b in range(n_bits):
        if b < 7:
            s = 1 << b
            r1 = pltpu.roll(v, LANES - s, axis=1)
            r2 = pltpu.roll(r1, n_rows - 1, axis=0)
            moved = jnp.where(lane < LANES - s, r1, r2)
        else:
            sr = 1 << (b - 7)
            moved = pltpu.roll(v, n_rows - sr, axis=0)
        take = ((moved >> VALID_BIT) & 1 == 1) & ((moved >> b) & 1 == 1)
        stay = ((v >> VALID_BIT) & 1 == 1) & ((v >> b) & 1 == 0)
        v = jnp.where(take, moved, jnp.where(stay, v, 0))
    idx_ref[0] = (pos + (v & ((1 << DISP_BITS) - 1)))[:cap_rows]


def route(aff_t):
    n_exp, n = aff_t.shape
    assert n <= (1 << DISP_BITS)
    cap = n // CAPACITY_DIV
    n_rows = n // LANES
    cap_rows = cap // LANES
    thr, need = pl.pallas_call(
        functools.partial(_threshold_kernel, cap=cap),
        out_shape=(jax.ShapeDtypeStruct((n_exp, LANES), I32), jax.ShapeDtypeStruct((n_exp, LANES), I32)),
        name="route_threshold",
    )(aff_t)
    grid_spec = pltpu.PrefetchScalarGridSpec(
        num_scalar_prefetch=2,
        grid=(n_exp,),
        in_specs=[pl.BlockSpec((1, n_rows, LANES), lambda e, *_: (e, 0, 0))],
        out_specs=[
            pl.BlockSpec((1, cap_rows, LANES), lambda e, *_: (e, 0, 0)),
            pl.BlockSpec((1, n_rows, LANES), lambda e, *_: (e, 0, 0)),
        ],
    )
    idx, rank = pl.pallas_call(
        functools.partial(_compact_kernel, n_rows=n_rows, cap_rows=cap_rows),
        grid_spec=grid_spec,
        out_shape=(
            jax.ShapeDtypeStruct((n_exp, cap_rows, LANES), I32),
            jax.ShapeDtypeStruct((n_exp, n_rows, LANES), I32),
        ),
        compiler_params=pltpu.CompilerParams(dimension_semantics=("parallel",)),
        name="route_compact",
    )(thr[:, 0], need[:, 0], aff_t.reshape(n_exp, n_rows, LANES))
    return idx.reshape(n_exp, cap), rank.reshape(n_exp, n)


D_MODEL = 1024
CONV_W = 512
N_HEADS = 4
V_DIM = 128
QK_DIM = 64
QK_W = N_HEADS * QK_DIM
V_W = N_HEADS * V_DIM
CHUNK = 128
EXPERT_FF = 1024
EPS = 1e-6
C_GB, C_GC, C_U, C_Q, C_K, C_V, C_O, C_G = 0, 512, 1024, 1536, 1792, 2048, 2560, 3072
D_IN = 3088
D_IN_PAD = 3200
TOK_EXT = D_MODEL + LANES

ROW_TILE = 512
MLSTM_BLOCK = 256
SLOT_TILE = 512
COMBINE_TILE = 256
VMEM_LIMIT = 56 * 1024 * 1024


def _cparams(*sem):
    return pltpu.CompilerParams(dimension_semantics=sem, vmem_limit_bytes=VMEM_LIMIT)


def _rms(x, g):
    return x * lax.rsqrt(jnp.mean(x * x, axis=-1, keepdims=True) + EPS) * g


GATE_GROUP = 8
GATE_ROWS = 6 * GATE_GROUP
G_A, G_PM, G_B = 0, 1, 2


def _chunk_scan(x, op, identity, reverse):
    width = x.shape[1]
    pos = lax.broadcasted_iota(I32, x.shape, 1) & (CHUNK - 1)
    for b in range(7):
        s = 1 << b
        if reverse:
            shifted = jnp.where(pos < CHUNK - s, pltpu.roll(x, width - s, axis=1), identity)
        else:
            shifted = jnp.where(pos >= s, pltpu.roll(x, s, axis=1), identity)
        x = op(x, shifted)
    return x


def _inproj_kernel(x_ref, g_ref, w_ref, bias_ref, gb_ref, gcu_ref, q_ref, k_ref, v_ref, os_ref, grow_ref):
    hn = _rms(x_ref[...], g_ref[...]).astype(BF16)

    def seg(a, b):
        return jnp.dot(hn, w_ref[:, a:b], preferred_element_type=F32)

    gates = seg(C_G, D_IN_PAD) + bias_ref[...]
    lane = lax.broadcasted_iota(I32, gates.shape, 1)
    log_sig = jnp.minimum(gates, 0.0) - jnp.log1p(jnp.exp(-jnp.abs(gates)))
    g16 = jnp.where((lane >> 2) & 1 == 1, log_sig, gates).T[:4 * N_HEADS]
    fwd = lax.broadcasted_iota(I32, g16.shape, 0) < 2 * N_HEADS
    cs = jnp.where(fwd, _chunk_scan(g16, jnp.add, 0.0, False), _chunk_scan(g16, jnp.add, 0.0, True))
    cs = pltpu.roll(cs, 3 * N_HEADS, axis=0)
    a = g16 - cs
    neg_inf = jnp.float32(-jnp.inf)
    pm = jnp.where(fwd, _chunk_scan(a, jnp.maximum, neg_inf, False), _chunk_scan(a, jnp.maximum, neg_inf, True))
    for d in range(2):
        grp = slice(d * GATE_GROUP, (d + 1) * GATE_GROUP)
        for g, val in ((G_A, a), (G_PM, pm), (G_B, cs)):
            r0 = (3 * d + g) * GATE_GROUP
            grow_ref[r0:r0 + GATE_GROUP, :] = val[grp]

    gb_ref[...] = seg(C_GB, C_GC).astype(BF16)
    gcu_ref[...] = (seg(C_GC, C_U) * seg(C_U, C_Q)).astype(BF16)
    q_ref[...] = (seg(C_Q, C_K) * (QK_DIM ** -0.5)).astype(BF16)
    k_ref[...] = seg(C_K, C_V).astype(BF16)
    v_ref[...] = seg(C_V, C_O).astype(BF16)
    os_ref[...] = jax.nn.sigmoid(seg(C_O, C_G)).astype(BF16)


def inproj(x, g1, w_in_p, bias_p):
    n = x.shape[0]
    tm = ROW_TILE
    row = lambda w: pl.BlockSpec((tm, w), lambda i: (i, 0))
    full = lambda a: pl.BlockSpec(a.shape, lambda i: (0,) * a.ndim)
    return pl.pallas_call(
        _inproj_kernel,
        grid=(n // tm,),
        in_specs=[row(D_MODEL), full(g1), full(w_in_p), full(bias_p)],
        out_specs=[row(CONV_W), row(CONV_W), row(QK_W), row(QK_W), row(V_W), row(V_W),
                   pl.BlockSpec((GATE_ROWS, tm), lambda i: (0, i))],
        out_shape=[
            jax.ShapeDtypeStruct((n, CONV_W), BF16), jax.ShapeDtypeStruct((n, CONV_W), BF16),
            jax.ShapeDtypeStruct((n, QK_W), BF16), jax.ShapeDtypeStruct((n, QK_W), BF16),
            jax.ShapeDtypeStruct((n, V_W), BF16), jax.ShapeDtypeStruct((n, V_W), BF16),
            jax.ShapeDtypeStruct((GATE_ROWS, n), F32),
        ],
        compiler_params=_cparams("parallel"),
        name="mixer_inproj",
    )(x, g1, w_in_p, bias_p)


def _mlstm_kernel(qf_ref, kf_ref, vf_ref, gf_ref, qb_ref, kb_ref, vb_ref, gb_ref, hf_ref, hb_ref,
                  cf_ref, mf_ref, cb_ref, mb_ref, *, seq_len):
    j = pl.program_id(0)

    @pl.when(((j * MLSTM_BLOCK) % seq_len) == 0)
    def _():
        for ref in (cf_ref, mf_ref, cb_ref, mb_ref):
            ref[...] = jnp.zeros_like(ref)

    _mlstm_block(qf_ref, kf_ref, vf_ref, gf_ref, hf_ref, cf_ref, mf_ref, reverse=False)
    _mlstm_block(qb_ref, kb_ref, vb_ref, gb_ref, hb_ref, cb_ref, mb_ref, reverse=True)


def _mlstm_block(q_ref, k_ref, v_ref, grow_ref, out_ref, c_ref, m_ref, *, reverse):
    blk = MLSTM_BLOCK
    t_i = lax.broadcasted_iota(I32, (CHUNK, CHUNK), 0)
    s_i = lax.broadcasted_iota(I32, (CHUNK, CHUNK), 1)
    tri = (s_i >= t_i) if reverse else (s_i <= t_i)
    lane = lax.broadcasted_iota(I32, (CHUNK, LANES), 1)
    half_masks = [jnp.where((lane >> 6) == hh, 1.0, 0.0).astype(BF16) for hh in range(2)]
    ones_col = jnp.where(lane == 0, 1.0, 0.0).astype(BF16)
    d = 3 if reverse else 0

    def gate_rows(g, cols):
        r0 = (d + g) * GATE_GROUP
        return grow_ref[r0:r0 + N_HEADS, cols]

    n_chunks = blk // CHUNK
    order = list(range(n_chunks - 1, -1, -1) if reverse else range(n_chunks))
    end = 0 if reverse else CHUNK - 1
    m_old = m_ref[0:N_HEADS, :]
    m_in = {}
    for c in order:
        last = slice(c * CHUNK + end, c * CHUNK + end + 1)
        m_in[c] = m_old
        m_old = gate_rows(G_B, last) + jnp.maximum(m_old, gate_rows(G_PM, last))
    m_ref[0:N_HEADS, :] = m_old

    c_state = [c_ref[h] for h in range(N_HEADS)]
    for c in order:
        r0 = c * CHUNK
        rows = slice(r0, r0 + CHUNK)
        m_o = m_in[c]
        a = gate_rows(G_A, rows)
        mm = jnp.maximum(m_o, gate_rows(G_PM, rows))
        mm_last = jnp.maximum(m_o, gate_rows(G_PM, slice(r0 + end, r0 + end + 1)))
        sc = jnp.exp(m_o - mm)
        emt = jnp.exp(-(gate_rows(G_B, rows) + mm))
        w = jnp.exp(a - mm_last)
        decay = jnp.exp(m_o - mm_last)
        stack = jnp.concatenate([mm, sc, emt, jnp.zeros((LANES - 3 * N_HEADS, CHUNK), F32)], axis=0)
        cols = stack.T
        for h in range(N_HEADS):
            pair = slice((h // 2) * LANES, (h // 2 + 1) * LANES)
            hv = slice(h * V_DIM, (h + 1) * V_DIM)
            q2 = q_ref[rows, pair]
            km = k_ref[rows, pair] * half_masks[h % 2]
            vh = v_ref[rows, hv]
            c_old = c_state[h]
            mm_col, sc_col = cols[:, h:h + 1], cols[:, N_HEADS + h:N_HEADS + h + 1]
            emt_col = cols[:, 2 * N_HEADS + h:2 * N_HEADS + h + 1]
            dmat = jnp.where(tri, jnp.exp(a[h:h + 1, :] - mm_col), 0.0)
            s_mat = lax.dot_general(q2, km, (((1,), (1,)), ((), ())), preferred_element_type=F32) * dmat
            qc = jnp.dot(q2, c_old.astype(BF16), preferred_element_type=F32)
            num = jnp.dot(s_mat.astype(BF16), vh, preferred_element_type=F32) + sc_col * qc[:, :V_DIM]
            den = jnp.sum(s_mat, axis=1, keepdims=True) + sc_col * qc[:, V_DIM:V_DIM + 1]
            h_out = num * (1.0 / jnp.maximum(jnp.abs(den), emt_col))
            kmt = (km.astype(F32).T * w[h:h + 1, :]).astype(BF16)
            kv = jnp.dot(kmt, jnp.concatenate([vh, ones_col], axis=1), preferred_element_type=F32)
            c_state[h] = decay[h:h + 1, 0:1] * c_old + kv
            out_ref[rows, hv] = h_out
    for h in range(N_HEADS):
        c_ref[h] = c_state[h]


def mlstm(q, k, v, grow, seq_len):
    n = q.shape[0]
    blk = MLSTM_BLOCK
    nb = n // blk
    assert n % seq_len == 0 and seq_len % blk == 0
    fwd = lambda w: pl.BlockSpec((blk, w), lambda j: (j, 0))
    bwd = lambda w: pl.BlockSpec((blk, w), lambda j: (nb - 1 - j, 0))
    state = [pltpu.VMEM((N_HEADS, LANES, 2 * LANES), F32), pltpu.VMEM((8, LANES), F32)]
    return pl.pallas_call(
        functools.partial(_mlstm_kernel, seq_len=seq_len),
        grid=(nb,),
        in_specs=[fwd(QK_W), fwd(QK_W), fwd(V_W), pl.BlockSpec((GATE_ROWS, blk), lambda j: (0, j)),
                  bwd(QK_W), bwd(QK_W), bwd(V_W), pl.BlockSpec((GATE_ROWS, blk), lambda j: (0, nb - 1 - j))],
        out_specs=[fwd(V_W), bwd(V_W)],
        out_shape=[jax.ShapeDtypeStruct((n, V_W), F32), jax.ShapeDtypeStruct((n, V_W), F32)],
        scratch_shapes=state + state,
        compiler_params=_cparams("arbitrary"),
        name="mlstm",
    )(q, k, v, grow, q, k, v, grow)


HALO = 16


def _outproj_kernel(gb_ref, gcu_ref, gprev_ref, gnext_ref, hf_ref, hb_ref, os_ref, hng_ref, x_ref, cw_ref, wo_ref,
                    g2_ref, wr_ref, x1e_ref, afft_ref, *, seq_len):
    i = pl.program_id(0)
    tm = ROW_TILE
    first = ((i * tm) % seq_len) == 0
    last = (((i + 1) * tm) % seq_len) == 0
    g = gcu_ref[...].astype(F32)
    prev_row = jnp.where(first, 0.0, gprev_ref[HALO - 1:HALO, :].astype(F32))
    next_row = jnp.where(last, 0.0, gnext_ref[0:1, :].astype(F32))
    rid = lax.broadcasted_iota(I32, g.shape, 0)
    dn = jnp.where(rid == 0, prev_row, pltpu.roll(g, 1, axis=0))
    up = jnp.where(rid == tm - 1, next_row, pltpu.roll(g, tm - 1, axis=0))
    conv = dn * cw_ref[0:1, :] + g * cw_ref[1:2, :] + up * cw_ref[2:3, :]
    co = (gb_ref[...].astype(F32) * conv).astype(BF16)
    y = jnp.dot(co, wo_ref[:CONV_W, :], preferred_element_type=F32)
    heads = []
    for h in range(N_HEADS):
        hv = slice(h * V_DIM, (h + 1) * V_DIM)
        ht = _rms(hf_ref[:, hv] + hb_ref[:, hv], hng_ref[:, hv])
        heads.append((os_ref[:, hv].astype(F32) * ht).astype(BF16))
    mo = jnp.concatenate(heads, axis=1)
    y = y + jnp.dot(mo, wo_ref[CONV_W:, :], preferred_element_type=F32)
    x1 = x_ref[...] + y
    tokens = _rms(x1, g2_ref[...])
    t_hi = tokens.astype(BF16)
    t_lo = (tokens - t_hi.astype(F32)).astype(BF16)
    p_hi = jnp.dot(t_hi, wr_ref[...], preferred_element_type=F32)
    logits = p_hi[:, :LANES] + p_hi[:, LANES:] + jnp.dot(t_lo, wr_ref[:, :LANES], preferred_element_type=F32)
    lane = lax.broadcasted_iota(I32, logits.shape, 1)
    logits = jnp.where(lane < N_EXPERTS, logits, -jnp.inf)
    ex = jnp.exp(logits - jnp.max(logits, axis=-1, keepdims=True))
    aff = ex / jnp.sum(ex, axis=-1, keepdims=True)
    x1e_ref[:, :D_MODEL] = x1
    row_id = (i * tm + lax.broadcasted_iota(I32, aff.shape, 0)).astype(F32)
    x1e_ref[:, D_MODEL:] = jnp.where(lane == TOKEN_ID_LANE, row_id, aff)
    afft_ref[...] = aff.T[:N_EXPERTS]


def outproj(gb, gcu, hf, hb, osig, hng, x, cw_p, w_out_b, g2, wr_p, seq_len):
    n = x.shape[0]
    tm = ROW_TILE
    halos = tm // HALO
    n_halo = n // HALO
    row = lambda w: pl.BlockSpec((tm, w), lambda i: (i, 0))
    full = lambda a: pl.BlockSpec(a.shape, lambda i: (0,) * a.ndim)
    prev = pl.BlockSpec((HALO, CONV_W), lambda i: (jnp.maximum(i * halos - 1, 0), 0))
    nxt = pl.BlockSpec((HALO, CONV_W), lambda i: (jnp.minimum((i + 1) * halos, n_halo - 1), 0))
    return pl.pallas_call(
        functools.partial(_outproj_kernel, seq_len=seq_len),
        grid=(n // tm,),
        in_specs=[row(CONV_W), row(CONV_W), prev, nxt, row(V_W), row(V_W), row(V_W), full(hng), row(D_MODEL),
                  full(cw_p), full(w_out_b), full(g2), full(wr_p)],
        out_specs=[row(TOK_EXT), pl.BlockSpec((N_EXPERTS, tm), lambda i: (0, i))],
        out_shape=[jax.ShapeDtypeStruct((n, TOK_EXT), F32), jax.ShapeDtypeStruct((N_EXPERTS, n), F32)],
        compiler_params=_cparams("parallel"),
        name="mixer_outproj_router",
    )(gb, gcu, gcu, gcu, hf, hb, osig, hng, x, cw_p, w_out_b, g2, wr_p)


def _ffn_kernel(idx_ref, tok_hbm, g2_ref, wg_ref, wu_ref, wd_ref, ye_ref, xg_ref, wgb_ref, wub_ref, wdb_ref, sem):
    e = pl.program_id(0)
    s = pl.program_id(1)
    n_tiles = pl.num_programs(1)
    ts = SLOT_TILE
    t = e * n_tiles + s
    slot = t % 2

    def row_copy(tile, slot_, i):
        tok = idx_ref[tile * ts + i]
        return pltpu.make_async_copy(tok_hbm.at[pl.ds(tok, 1)], xg_ref.at[slot_, pl.ds(i, 1)], sem.at[slot_])

    @pl.when(t == 0)
    def _():
        def issue(i, carry):
            row_copy(0, 0, i).start()
            return carry

        lax.fori_loop(0, ts, issue, 0, unroll=8)

    @pl.when(s == 0)
    def _():
        wgb_ref[...] = wg_ref[0, 0].astype(BF16)
        wub_ref[...] = wu_ref[0, 0].astype(BF16)
        wdb_ref[...] = wd_ref[0, 0].astype(BF16)

    pltpu.make_async_copy(tok_hbm.at[pl.ds(0, ts)], xg_ref.at[slot], sem.at[slot]).wait()

    x = xg_ref[slot]
    xb = _rms(x[:, :D_MODEL], g2_ref[...]).astype(BF16)
    ext = x[:, D_MODEL:]
    lane = lax.broadcasted_iota(I32, ext.shape, 1)
    gate = jnp.sum(jnp.where(lane == e, ext, 0.0), axis=1, keepdims=True)

    last = N_EXPERTS * n_tiles - 1
    nxt = jnp.minimum(t + 1, last)
    for i in range(ts):
        row_copy(nxt, 1 - slot, i).start(priority=i % 2)

    hg = jnp.dot(xb, wgb_ref[...], preferred_element_type=F32)
    hu = jnp.dot(xb, wub_ref[...], preferred_element_type=F32)
    hid = (hg * jax.nn.sigmoid(hg) * hu).astype(BF16)
    ye_ref[:, :D_MODEL] = jnp.dot(hid, wdb_ref[...], preferred_element_type=F32) * gate
    ye_ref[:, D_MODEL:] = ext

    @pl.when(t == last)
    def _():
        pltpu.make_async_copy(tok_hbm.at[pl.ds(0, ts)], xg_ref.at[1 - slot], sem.at[1 - slot]).wait()


def expert_ffn(idx_flat, x1e, g2, wg, wu, wd, cap, layer):
    n_tiles = cap // SLOT_TILE
    wspec = lambda: pl.BlockSpec((1, 1, D_MODEL, EXPERT_FF), lambda e, s, *_: (layer, e, 0, 0))
    grid_spec = pltpu.PrefetchScalarGridSpec(
        num_scalar_prefetch=1,
        grid=(N_EXPERTS, n_tiles),
        in_specs=[pl.BlockSpec(memory_space=pl.ANY), pl.BlockSpec((1, D_MODEL), lambda e, s, *_: (0, 0)), wspec(), wspec(),
                  pl.BlockSpec((1, 1, EXPERT_FF, D_MODEL), lambda e, s, *_: (layer, e, 0, 0))],
        out_specs=pl.BlockSpec((SLOT_TILE, TOK_EXT), lambda e, s, *_: (e * n_tiles + s, 0)),
        scratch_shapes=[pltpu.VMEM((2, SLOT_TILE, TOK_EXT), F32),
                        pltpu.VMEM((D_MODEL, EXPERT_FF), BF16), pltpu.VMEM((D_MODEL, EXPERT_FF), BF16),
                        pltpu.VMEM((EXPERT_FF, D_MODEL), BF16), pltpu.SemaphoreType.DMA((2,))],
    )
    return pl.pallas_call(
        _ffn_kernel,
        grid_spec=grid_spec,
        out_shape=jax.ShapeDtypeStruct((N_EXPERTS * cap, TOK_EXT), F32),
        compiler_params=_cparams("arbitrary", "arbitrary"),
        name="expert_ffn",
    )(idx_flat, x1e, g2, wg, wu, wd)


WIN = 8
STACK_TILE = 256


def _stack_rows(tc):
    rows = N_EXPERTS * (tc + 2 * (WIN - 1))
    return -(-rows // STACK_TILE) * STACK_TILE


def _combine_kernel(off_ref, x1_ref, ye_hbm, *rest, cap, n_blocks, final):
    if final:
        fg_ref, out_ref, ys_ref, sem = rest
    else:
        out_ref, ys_ref, sem = rest
    j = pl.program_id(0)
    tc = COMBINE_TILE
    buf = j % 2

    def windows(tile, e):
        lo = off_ref[e * (n_blocks + 1) + tile]
        hi = off_ref[e * (n_blocks + 1) + tile + 1]
        start = (lo >> 3) << 3
        return start, jnp.where(hi > lo, (hi - start + (WIN - 1)) >> 3, 0)

    def stacked_rows(tile):
        total = jnp.int32(0)
        for e in range(N_EXPERTS):
            total = total + windows(tile, e)[1] * WIN
        return total

    def fetch(tile, buf_):
        base = jnp.int32(0)
        for e in range(N_EXPERTS):
            start, n_win = windows(tile, e)

            def issue(w, carry, e=e, start=start, base=base):
                src = pl.multiple_of(e * cap + start + w * WIN, WIN)
                dst = pl.multiple_of(base + w * WIN, WIN)
                pltpu.make_async_copy(ye_hbm.at[pl.ds(src, WIN)], ys_ref.at[buf_, pl.ds(dst, WIN)], sem.at[buf_]).start()
                return carry

            lax.fori_loop(0, n_win, issue, 0)
            base = base + n_win * WIN

    @pl.when(j == 0)
    def _():
        ys_ref[...] = jnp.zeros_like(ys_ref)
        fetch(0, 0)

    base = stacked_rows(j)
    n_total = base >> 3
    for bit in range((_stack_rows(tc) // WIN).bit_length()):
        @pl.when((n_total & (1 << bit)) != 0)
        def _(bit=bit):
            rows = WIN << bit
            pltpu.make_async_copy(ye_hbm.at[pl.ds(0, rows)], ye_hbm.at[pl.ds(0, rows)], sem.at[buf]).wait()

    @pl.when(j + 1 < n_blocks)
    def _():
        fetch(j + 1, 1 - buf)

    tok0 = (j * tc).astype(F32)
    lane_t = lax.broadcasted_iota(I32, (STACK_TILE, tc), 1).astype(F32)
    row_i = lax.broadcasted_iota(I32, (STACK_TILE, 1), 0)

    def accumulate(kt, acc):
        r0 = pl.multiple_of(kt * STACK_TILE, STACK_TILE)
        rows = ys_ref[buf, pl.ds(r0, STACK_TILE), :]
        tok_local = rows[:, D_MODEL + TOKEN_ID_LANE:D_MODEL + TOKEN_ID_LANE + 1] - tok0
        tok_local = jnp.where(r0 + row_i < base, tok_local, -1.0)
        onehot = jnp.where(tok_local == lane_t, 1.0, 0.0).astype(BF16)
        out_ref[...] += lax.dot_general(onehot, rows[:, :D_MODEL].astype(BF16), (((0,), (0,)), ((), ())),
                                        preferred_element_type=F32)
        return acc

    n_kt = (base + (STACK_TILE - 1)) // STACK_TILE
    out_ref[...] = x1_ref[...]
    lax.fori_loop(0, n_kt, accumulate, 0)
    if final:
        out_ref[...] = _rms(out_ref[...], fg_ref[...])


def combine(off_flat, x1e, ye, cap, final_g=None):
    n = x1e.shape[0]
    tc = COMBINE_TILE
    nb = n // tc
    final = final_g is not None
    in_specs = [pl.BlockSpec((tc, D_MODEL), lambda j, *_: (j, 0)),
                pl.BlockSpec(memory_space=pl.ANY)]
    args = [x1e, ye]
    if final:
        in_specs.append(pl.BlockSpec((1, D_MODEL), lambda j, *_: (0, 0)))
        args.append(final_g)
    grid_spec = pltpu.PrefetchScalarGridSpec(
        num_scalar_prefetch=1,
        grid=(nb,),
        in_specs=in_specs,
        out_specs=pl.BlockSpec((tc, D_MODEL), lambda j, *_: (j, 0)),
        scratch_shapes=[pltpu.VMEM((2, _stack_rows(tc), TOK_EXT), F32), pltpu.SemaphoreType.DMA((2,))],
    )
    return pl.pallas_call(
        functools.partial(_combine_kernel, cap=cap, n_blocks=nb, final=final),
        grid_spec=grid_spec,
        out_shape=jax.ShapeDtypeStruct((n, D_MODEL), F32),
        compiler_params=_cparams("arbitrary"),
        name="moe_combine",
    )(off_flat, *args)


def moe(x1e, aff_t, g2, wg, wu, wd, layer, final_g=None):
    n = x1e.shape[0]
    cap = n // CAPACITY_DIV
    idx, rank = route(aff_t)
    off = jnp.concatenate([rank[:, ::COMBINE_TILE], jnp.full((N_EXPERTS, 1), cap, I32)], axis=1)
    ye = expert_ffn(idx.reshape(-1), x1e, g2, wg, wu, wd, cap, layer)
    return combine(off.reshape(-1), x1e, ye, cap, final_g)


def kernel(x_prompt, x_sample, norm1_g, w_in, conv_w, gate_bias, head_norm_g, w_out, norm2_g, w_router, w_gate, w_up, w_down, final_g):
    depth = w_in.shape[0]
    w_in_p = jnp.pad(w_in, ((0, 0), (0, 0), (0, D_IN_PAD - D_IN))).astype(BF16)
    bias_p = jnp.pad(gate_bias.reshape(depth, 1, 4 * N_HEADS), ((0, 0), (0, 0), (0, LANES - 4 * N_HEADS)))
    cw_p = jnp.pad(conv_w, ((0, 0), (0, 8 - conv_w.shape[1]), (0, 0)))
    wr_f = jnp.pad(w_router, ((0, 0), (0, 0), (0, LANES - N_EXPERTS)))
    wr_hi = wr_f.astype(BF16)
    wr_p = jnp.concatenate([wr_hi, (wr_f - wr_hi.astype(F32)).astype(BF16)], axis=-1)
    w_out_b = w_out.astype(BF16)
    fg = final_g.reshape(1, D_MODEL)

    outs = []
    for x in (x_prompt, x_sample):
        bsz, seq, _ = x.shape
        xf = x.reshape(bsz * seq, D_MODEL)
        for l in range(depth):
            gb, gcu, q, k, v, osig, grow = inproj(xf, norm1_g[l].reshape(1, -1), w_in_p[l], bias_p[l])
            hf, hb = mlstm(q, k, v, grow, seq)
            g2 = norm2_g[l].reshape(1, -1)
            x1e, aff_t = outproj(gb, gcu, hf, hb, osig, head_norm_g[l].reshape(1, -1), xf, cw_p[l], w_out_b[l],
                                 g2, wr_p[l], seq)
            xf = moe(x1e, aff_t, g2, w_gate, w_up, w_down, l, fg if l == depth - 1 else None)
        outs.append(xf.reshape(bsz, seq, D_MODEL))
    return tuple(outs)
```

```python
import functools

import jax
import jax.numpy as jnp
from jax import lax
from jax.experimental import pallas as pl
from jax.experimental.pallas import tpu as pltpu

F32 = jnp.float32
BF16 = jnp.bfloat16
I32 = jnp.int32

LANES = 128
N_EXPERTS = 16
CAPACITY_DIV = 8
DISP_BITS = 16
VALID_BIT = 24
TOKEN_ID_LANE = N_EXPERTS


def _threshold_kernel(aff_ref, thr_ref, need_ref, *, cap):
    bits = pltpu.bitcast(aff_ref[...], I32)
    cap_f = jnp.float32(cap)

    def count_ge(cand):
        return jnp.sum(jnp.where(bits >= cand, 1.0, 0.0), axis=1, keepdims=True)

    def body(i, thr):
        cand = thr | jnp.left_shift(jnp.int32(1), 30 - i)
        return jnp.where(count_ge(cand) >= cap_f, cand, thr)

    thr = lax.fori_loop(0, 31, body, jnp.zeros((N_EXPERTS, 1), I32))
    n_gt = jnp.sum(jnp.where(bits > thr, 1.0, 0.0), axis=1, keepdims=True)
    need = (cap_f - n_gt).astype(I32)
    thr_ref[...] = jnp.broadcast_to(thr, thr_ref.shape)
    need_ref[...] = jnp.broadcast_to(need, need_ref.shape)


def _lane_inclusive_scan(x, lane):
    for b in range(7):
        s = 1 << b
        x = x + jnp.where(lane >= s, pltpu.roll(x, s, axis=1), 0.0)
    return x


def _row_exclusive_scan(t, row, n_rows):
    inc = t
    s = 1
    while s < n_rows:
        inc = inc + jnp.where(row >= s, pltpu.roll(inc, s, axis=0), 0.0)
        s *= 2
    return inc - t


def _token_exclusive_scan(x, lane, row, n_rows):
    inc = _lane_inclusive_scan(x, lane)
    tot = jnp.broadcast_to(inc[:, LANES - 1:LANES], x.shape)
    return inc - x + _row_exclusive_scan(tot, row, n_rows)


def _compact_kernel(thr_ref, need_ref, aff_ref, idx_ref, rank_ref, *, n_rows, cap_rows):
    e = pl.program_id(0)
    shape = (n_rows, LANES)
    lane = lax.broadcasted_iota(I32, shape, 1)
    row = lax.broadcasted_iota(I32, shape, 0)
    bits = pltpu.bitcast(aff_ref[0], I32)
    thr = thr_ref[e]
    need = need_ref[e].astype(F32)
    eq = bits == thr
    pre_eq = _token_exclusive_scan(jnp.where(eq, 1.0, 0.0), lane, row, n_rows)
    sel = (bits > thr) | (eq & (pre_eq < need))
    sel_f = jnp.where(sel, 1.0, 0.0)
    rank = _token_exclusive_scan(sel_f, lane, row, n_rows).astype(I32)
    rank_ref[0] = rank

    pos = row * LANES + lane
    disp = pos - rank
    v = jnp.where(sel, disp | (1 << VALID_BIT), 0)
    n_bits = (n_rows * LANES - 1).bit_length()
    for b in range(n_bits):
        if b < 7:
            s = 1 << b
            r1 = pltpu.roll(v, LANES - s, axis=1)
            r2 = pltpu.roll(r1, n_rows - 1, axis=0)
            moved = jnp.where(lane < LANES - s, r1, r2)
        else:
            sr = 1 << (b - 7)
            moved = pltpu.roll(v, n_rows - sr, axis=0)
        take = ((moved >> VALID_BIT) & 1 == 1) & ((moved >> b) & 1 == 1)
        stay = ((v >> VALID_BIT) & 1 == 1) & ((v >> b) & 1 == 0)
        v = jnp.where(take, moved, jnp.where(stay, v, 0))
    idx_ref[0] = (pos + (v & ((1 << DISP_BITS) - 1)))[:cap_rows]


def route(aff_t):
    n_exp, n = aff_t.shape
    assert n <= (1 << DISP_BITS)
    cap = n // CAPACITY_DIV
    n_rows = n // LANES
    cap_rows = cap // LANES
    thr, need = pl.pallas_call(
        functools.partial(_threshold_kernel, cap=cap),
        out_shape=(jax.ShapeDtypeStruct((n_exp, LANES), I32), jax.ShapeDtypeStruct((n_exp, LANES), I32)),
        name="route_threshold",
    )(aff_t)
    grid_spec = pltpu.PrefetchScalarGridSpec(
        num_scalar_prefetch=2,
        grid=(n_exp,),
        in_specs=[pl.BlockSpec((1, n_rows, LANES), lambda e, *_: (e, 0, 0))],
        out_specs=[
            pl.BlockSpec((1, cap_rows, LANES), lambda e, *_: (e, 0, 0)),
            pl.BlockSpec((1, n_rows, LANES), lambda e, *_: (e, 0, 0)),
        ],
    )
    idx, rank = pl.pallas_call(
        functools.partial(_compact_kernel, n_rows=n_rows, cap_rows=cap_rows),
        grid_spec=grid_spec,
        out_shape=(
            jax.ShapeDtypeStruct((n_exp, cap_rows, LANES), I32),
            jax.ShapeDtypeStruct((n_exp, n_rows, LANES), I32),
        ),
        compiler_params=pltpu.CompilerParams(dimension_semantics=("parallel",)),
        name="route_compact",
    )(thr[:, 0], need[:, 0], aff_t.reshape(n_exp, n_rows, LANES))
    return idx.reshape(n_exp, cap), rank.reshape(n_exp, n)


D_MODEL = 1024
CONV_W = 512
N_HEADS = 4
V_DIM = 128
QK_DIM = 64
QK_W = N_HEADS * QK_DIM
V_W = N_HEADS * V_DIM
CHUNK = 128
EXPERT_FF = 1024
EPS = 1e-6
C_GB, C_GC, C_U, C_Q, C_K, C_V, C_O, C_G = 0, 512, 1024, 1536, 1792, 2048, 2560, 3072
D_IN = 3088
D_IN_PAD = 3200
TOK_EXT = D_MODEL + LANES

ROW_TILE = 512
INPROJ_TILE = 1024
MLSTM_BLOCK = 256
SLOT_TILE = 512
COMBINE_TILE = 256
VMEM_LIMIT = 56 * 1024 * 1024


def _cparams(*sem):
    return pltpu.CompilerParams(dimension_semantics=sem, vmem_limit_bytes=VMEM_LIMIT)


def _rms(x, g):
    return x * lax.rsqrt(jnp.mean(x * x, axis=-1, keepdims=True) + EPS) * g


GATE_GROUP = 8
GATE_ROWS = 6 * GATE_GROUP
G_A, G_PM, G_B = 0, 1, 2


def _chunk_scan(x, op, identity, reverse):
    width = x.shape[1]
    pos = lax.broadcasted_iota(I32, x.shape, 1) & (CHUNK - 1)
    for b in range(7):
        s = 1 << b
        if reverse:
            shifted = jnp.where(pos < CHUNK - s, pltpu.roll(x, width - s, axis=1), identity)
        else:
            shifted = jnp.where(pos >= s, pltpu.roll(x, s, axis=1), identity)
        x = op(x, shifted)
    return x


def _inproj_kernel(x_ref, g_ref, w_ref, bias_ref, gb_ref, gcu_ref, q_ref, k_ref, v_ref, os_ref, grow_ref):
    hn = _rms(x_ref[...], g_ref[...]).astype(BF16)

    def seg(a, b):
        return jnp.dot(hn, w_ref[:, a:b], preferred_element_type=F32)

    gates = seg(C_G, D_IN_PAD) + bias_ref[...]
    lane = lax.broadcasted_iota(I32, gates.shape, 1)
    log_sig = jnp.minimum(gates, 0.0) - jnp.log1p(jnp.exp(-jnp.abs(gates)))
    g16 = jnp.where((lane >> 2) & 1 == 1, log_sig, gates).T[:4 * N_HEADS]
    fwd = lax.broadcasted_iota(I32, g16.shape, 0) < 2 * N_HEADS
    cs = jnp.where(fwd, _chunk_scan(g16, jnp.add, 0.0, False), _chunk_scan(g16, jnp.add, 0.0, True))
    cs = pltpu.roll(cs, 3 * N_HEADS, axis=0)
    a = g16 - cs
    neg_inf = jnp.float32(-jnp.inf)
    pm = jnp.where(fwd, _chunk_scan(a, jnp.maximum, neg_inf, False), _chunk_scan(a, jnp.maximum, neg_inf, True))
    for d in range(2):
        grp = slice(d * GATE_GROUP, (d + 1) * GATE_GROUP)
        for g, val in ((G_A, a), (G_PM, pm), (G_B, cs)):
            r0 = (3 * d + g) * GATE_GROUP
            grow_ref[r0:r0 + GATE_GROUP, :] = val[grp]

    gb_ref[...] = seg(C_GB, C_GC).astype(BF16)
    gcu_ref[...] = (seg(C_GC, C_U) * seg(C_U, C_Q)).astype(BF16)
    q_ref[...] = (seg(C_Q, C_K) * (QK_DIM ** -0.5)).astype(BF16)
    k_ref[...] = seg(C_K, C_V).astype(BF16)
    v_ref[...] = seg(C_V, C_O).astype(BF16)
    os_ref[...] = jax.nn.sigmoid(seg(C_O, C_G)).astype(BF16)


def inproj(x, g1, w_in_p, bias_p):
    n = x.shape[0]
    tm = INPROJ_TILE
    row = lambda w: pl.BlockSpec((tm, w), lambda i: (i, 0))
    full = lambda a: pl.BlockSpec(a.shape, lambda i: (0,) * a.ndim)
    return pl.pallas_call(
        _inproj_kernel,
        grid=(n // tm,),
        in_specs=[row(D_MODEL), full(g1), full(w_in_p), full(bias_p)],
        out_specs=[row(CONV_W), row(CONV_W), row(QK_W), row(QK_W), row(V_W), row(V_W),
                   pl.BlockSpec((GATE_ROWS, tm), lambda i: (0, i))],
        out_shape=[
            jax.ShapeDtypeStruct((n, CONV_W), BF16), jax.ShapeDtypeStruct((n, CONV_W), BF16),
            jax.ShapeDtypeStruct((n, QK_W), BF16), jax.ShapeDtypeStruct((n, QK_W), BF16),
            jax.ShapeDtypeStruct((n, V_W), BF16), jax.ShapeDtypeStruct((n, V_W), BF16),
            jax.ShapeDtypeStruct((GATE_ROWS, n), F32),
        ],
        compiler_params=_cparams("parallel"),
        name="mixer_inproj",
    )(x, g1, w_in_p, bias_p)


def _mlstm_kernel(qf_ref, kf_ref, vf_ref, gf_ref, qb_ref, kb_ref, vb_ref, gb_ref, hf_ref, hb_ref,
                  cf_ref, mf_ref, cb_ref, mb_ref, *, seq_len):
    j = pl.program_id(0)

    @pl.when(((j * MLSTM_BLOCK) % seq_len) == 0)
    def _():
        for ref in (cf_ref, mf_ref, cb_ref, mb_ref):
            ref[...] = jnp.zeros_like(ref)

    _mlstm_block(qf_ref, kf_ref, vf_ref, gf_ref, hf_ref, cf_ref, mf_ref, reverse=False)
    _mlstm_block(qb_ref, kb_ref, vb_ref, gb_ref, hb_ref, cb_ref, mb_ref, reverse=True)


def _mlstm_block(q_ref, k_ref, v_ref, grow_ref, out_ref, c_ref, m_ref, *, reverse):
    blk = MLSTM_BLOCK
    t_i = lax.broadcasted_iota(I32, (CHUNK, CHUNK), 0)
    s_i = lax.broadcasted_iota(I32, (CHUNK, CHUNK), 1)
    tri = (s_i >= t_i) if reverse else (s_i <= t_i)
    lane = lax.broadcasted_iota(I32, (CHUNK, LANES), 1)
    half_masks = [jnp.where((lane >> 6) == hh, 1.0, 0.0).astype(BF16) for hh in range(2)]
    ones_col = jnp.where(lane == 0, 1.0, 0.0).astype(BF16)
    d = 3 if reverse else 0

    def gate_rows(g, cols):
        r0 = (d + g) * GATE_GROUP
        return grow_ref[r0:r0 + N_HEADS, cols]

    n_chunks = blk // CHUNK
    order = list(range(n_chunks - 1, -1, -1) if reverse else range(n_chunks))
    end = 0 if reverse else CHUNK - 1
    m_old = m_ref[0:N_HEADS, :]
    m_in = {}
    for c in order:
        last = slice(c * CHUNK + end, c * CHUNK + end + 1)
        m_in[c] = m_old
        m_old = gate_rows(G_B, last) + jnp.maximum(m_old, gate_rows(G_PM, last))
    m_ref[0:N_HEADS, :] = m_old

    c_state = [c_ref[h] for h in range(N_HEADS)]
    for c in order:
        r0 = c * CHUNK
        rows = slice(r0, r0 + CHUNK)
        m_o = m_in[c]
        a = gate_rows(G_A, rows)
        mm = jnp.maximum(m_o, gate_rows(G_PM, rows))
        mm_last = jnp.maximum(m_o, gate_rows(G_PM, slice(r0 + end, r0 + end + 1)))
        sc = jnp.exp(m_o - mm)
        emt = jnp.exp(-(gate_rows(G_B, rows) + mm))
        w = jnp.exp(a - mm_last)
        decay = jnp.exp(m_o - mm_last)
        stack = jnp.concatenate([mm, sc, emt, jnp.zeros((LANES - 3 * N_HEADS, CHUNK), F32)], axis=0)
        cols = stack.T
        for h in range(N_HEADS):
            pair = slice((h // 2) * LANES, (h // 2 + 1) * LANES)
            hv = slice(h * V_DIM, (h + 1) * V_DIM)
            q2 = q_ref[rows, pair]
            km = k_ref[rows, pair] * half_masks[h % 2]
            vh = v_ref[rows, hv]
            c_old = c_state[h]
            mm_col, sc_col = cols[:, h:h + 1], cols[:, N_HEADS + h:N_HEADS + h + 1]
            emt_col = cols[:, 2 * N_HEADS + h:2 * N_HEADS + h + 1]
            dmat = jnp.where(tri, jnp.exp(a[h:h + 1, :] - mm_col), 0.0)
            s_mat = lax.dot_general(q2, km, (((1,), (1,)), ((), ())), preferred_element_type=F32) * dmat
            qc = jnp.dot(q2, c_old.astype(BF16), preferred_element_type=F32)
            num = jnp.dot(s_mat.astype(BF16), vh, preferred_element_type=F32) + sc_col * qc[:, :V_DIM]
            den = jnp.sum(s_mat, axis=1, keepdims=True) + sc_col * qc[:, V_DIM:V_DIM + 1]
            h_out = num * (1.0 / jnp.maximum(jnp.abs(den), emt_col))
            kmt = (km.astype(F32).T * w[h:h + 1, :]).astype(BF16)
            kv = jnp.dot(kmt, jnp.concatenate([vh, ones_col], axis=1), preferred_element_type=F32)
            c_state[h] = decay[h:h + 1, 0:1] * c_old + kv
            out_ref[rows, hv] = h_out.astype(BF16)
    for h in range(N_HEADS):
        c_ref[h] = c_state[h]


def mlstm(q, k, v, grow, seq_len):
    n = q.shape[0]
    blk = MLSTM_BLOCK
    nb = n // blk
    assert n % seq_len == 0 and seq_len % blk == 0
    fwd = lambda w: pl.BlockSpec((blk, w), lambda j: (j, 0))
    bwd = lambda w: pl.BlockSpec((blk, w), lambda j: (nb - 1 - j, 0))
    state = [pltpu.VMEM((N_HEADS, LANES, 2 * LANES), F32), pltpu.VMEM((8, LANES), F32)]
    return pl.pallas_call(
        functools.partial(_mlstm_kernel, seq_len=seq_len),
        grid=(nb,),
        in_specs=[fwd(QK_W), fwd(QK_W), fwd(V_W), pl.BlockSpec((GATE_ROWS, blk), lambda j: (0, j)),
                  bwd(QK_W), bwd(QK_W), bwd(V_W), pl.BlockSpec((GATE_ROWS, blk), lambda j: (0, nb - 1 - j))],
        out_specs=[fwd(V_W), bwd(V_W)],
        out_shape=[jax.ShapeDtypeStruct((n, V_W), BF16), jax.ShapeDtypeStruct((n, V_W), BF16)],
        scratch_shapes=state + state,
        compiler_params=_cparams("arbitrary"),
        name="mlstm",
    )(q, k, v, grow, q, k, v, grow)


HALO = 16


def _outproj_kernel(gb_ref, gcu_ref, gprev_ref, gnext_ref, hf_ref, hb_ref, os_ref, hng_ref, x_ref, cw_ref, wo_ref,
                    g2_ref, wr_ref, x1e_ref, afft_ref, *, seq_len):
    i = pl.program_id(0)
    tm = ROW_TILE
    first = ((i * tm) % seq_len) == 0
    last = (((i + 1) * tm) % seq_len) == 0
    g = gcu_ref[...].astype(F32)
    prev_row = jnp.where(first, 0.0, gprev_ref[HALO - 1:HALO, :].astype(F32))
    next_row = jnp.where(last, 0.0, gnext_ref[0:1, :].astype(F32))
    rid = lax.broadcasted_iota(I32, g.shape, 0)
    dn = jnp.where(rid == 0, prev_row, pltpu.roll(g, 1, axis=0))
    up = jnp.where(rid == tm - 1, next_row, pltpu.roll(g, tm - 1, axis=0))
    conv = dn * cw_ref[0:1, :] + g * cw_ref[1:2, :] + up * cw_ref[2:3, :]
    co = (gb_ref[...].astype(F32) * conv).astype(BF16)
    y = jnp.dot(co, wo_ref[:CONV_W, :], preferred_element_type=F32)
    heads = []
    for h in range(N_HEADS):
        hv = slice(h * V_DIM, (h + 1) * V_DIM)
        ht = _rms(hf_ref[:, hv].astype(F32) + hb_ref[:, hv].astype(F32), hng_ref[:, hv])
        heads.append((os_ref[:, hv].astype(F32) * ht).astype(BF16))
    mo = jnp.concatenate(heads, axis=1)
    y = y + jnp.dot(mo, wo_ref[CONV_W:, :], preferred_element_type=F32)
    x1 = x_ref[...] + y
    tokens = _rms(x1, g2_ref[...])
    t_hi = tokens.astype(BF16)
    t_lo = (tokens - t_hi.astype(F32)).astype(BF16)
    p_hi = jnp.dot(t_hi, wr_ref[...], preferred_element_type=F32)
    logits = p_hi[:, :LANES] + p_hi[:, LANES:] + jnp.dot(t_lo, wr_ref[:, :LANES], preferred_element_type=F32)
    lane = lax.broadcasted_iota(I32, logits.shape, 1)
    logits = jnp.where(lane < N_EXPERTS, logits, -jnp.inf)
    ex = jnp.exp(logits - jnp.max(logits, axis=-1, keepdims=True))
    aff = ex / jnp.sum(ex, axis=-1, keepdims=True)
    x1e_ref[:, :D_MODEL] = x1
    row_id = (i * tm + lax.broadcasted_iota(I32, aff.shape, 0)).astype(F32)
    x1e_ref[:, D_MODEL:] = jnp.where(lane == TOKEN_ID_LANE, row_id, aff)
    afft_ref[...] = aff.T[:N_EXPERTS]


def outproj(gb, gcu, hf, hb, osig, hng, x, cw_p, w_out_b, g2, wr_p, seq_len):
    n = x.shape[0]
    tm = ROW_TILE
    halos = tm // HALO
    n_halo = n // HALO
    row = lambda w: pl.BlockSpec((tm, w), lambda i: (i, 0))
    full = lambda a: pl.BlockSpec(a.shape, lambda i: (0,) * a.ndim)
    prev = pl.BlockSpec((HALO, CONV_W), lambda i: (jnp.maximum(i * halos - 1, 0), 0))
    nxt = pl.BlockSpec((HALO, CONV_W), lambda i: (jnp.minimum((i + 1) * halos, n_halo - 1), 0))
    return pl.pallas_call(
        functools.partial(_outproj_kernel, seq_len=seq_len),
        grid=(n // tm,),
        in_specs=[row(CONV_W), row(CONV_W), prev, nxt, row(V_W), row(V_W), row(V_W), full(hng), row(D_MODEL),
                  full(cw_p), full(w_out_b), full(g2), full(wr_p)],
        out_specs=[row(TOK_EXT), pl.BlockSpec((N_EXPERTS, tm), lambda i: (0, i))],
        out_shape=[jax.ShapeDtypeStruct((n, TOK_EXT), F32), jax.ShapeDtypeStruct((N_EXPERTS, n), F32)],
        compiler_params=_cparams("parallel"),
        name="mixer_outproj_router",
    )(gb, gcu, gcu, gcu, hf, hb, osig, hng, x, cw_p, w_out_b, g2, wr_p)


def _ffn_kernel(idx_ref, tok_hbm, g2_ref, wg_ref, wu_ref, wd_ref, ye_ref, xg_ref, wgb_ref, wub_ref, wdb_ref, sem):
    e = pl.program_id(0)
    s = pl.program_id(1)
    n_steps = pl.num_programs(1)
    ts = SLOT_TILE
    t = e * n_steps + s
    last = N_EXPERTS * n_steps - 1

    def row_copy(tile, buf, i):
        tok = idx_ref[tile * ts + i]
        return pltpu.make_async_copy(tok_hbm.at[pl.ds(tok, 1)], xg_ref.at[buf, pl.ds(i, 1)], sem.at[buf])

    def wait_rows(buf):
        pltpu.make_async_copy(tok_hbm.at[pl.ds(0, ts)], xg_ref.at[buf], sem.at[buf]).wait()

    @pl.when(t == 0)
    def _():
        def issue(i, carry):
            row_copy(0, 0, i).start()
            return carry

        lax.fori_loop(0, ts, issue, 0, unroll=8)

    @pl.when(s == 0)
    def _():
        wgb_ref[...] = wg_ref[0, 0].astype(BF16)
        wub_ref[...] = wu_ref[0, 0].astype(BF16)
        wdb_ref[...] = wd_ref[0, 0].astype(BF16)

    def half(buf, next_tile, rows):
        wait_rows(buf)
        x = xg_ref[buf]
        xb = _rms(x[:, :D_MODEL], g2_ref[...]).astype(BF16)
        ext = x[:, D_MODEL:]
        lane = lax.broadcasted_iota(I32, ext.shape, 1)
        gate = jnp.sum(jnp.where(lane == e, ext, 0.0), axis=1, keepdims=True)
        for i in range(ts):
            row_copy(next_tile, 1 - buf, i).start(priority=i % 2)
        hg = jnp.dot(xb, wgb_ref[...], preferred_element_type=F32)
        hu = jnp.dot(xb, wub_ref[...], preferred_element_type=F32)
        hid = (hg * jax.nn.sigmoid(hg) * hu).astype(BF16)
        ye_ref[rows, :D_MODEL] = jnp.dot(hid, wdb_ref[...], preferred_element_type=F32) * gate
        ye_ref[rows, D_MODEL:] = ext

    half(0, 2 * t + 1, slice(0, ts))
    half(1, jnp.minimum(2 * t + 2, 2 * last), slice(ts, 2 * ts))

    @pl.when(t == last)
    def _():
        wait_rows(0)


def expert_ffn(idx_flat, x1e, g2, wg, wu, wd, cap, layer):
    assert cap % (2 * SLOT_TILE) == 0
    n_steps = cap // (2 * SLOT_TILE)
    wspec = lambda: pl.BlockSpec((1, 1, D_MODEL, EXPERT_FF), lambda e, s, *_: (layer, e, 0, 0))
    grid_spec = pltpu.PrefetchScalarGridSpec(
        num_scalar_prefetch=1,
        grid=(N_EXPERTS, n_steps),
        in_specs=[pl.BlockSpec(memory_space=pl.ANY), pl.BlockSpec((1, D_MODEL), lambda e, s, *_: (0, 0)), wspec(), wspec(),
                  pl.BlockSpec((1, 1, EXPERT_FF, D_MODEL), lambda e, s, *_: (layer, e, 0, 0))],
        out_specs=pl.BlockSpec((2 * SLOT_TILE, TOK_EXT), lambda e, s, *_: (e * n_steps + s, 0)),
        scratch_shapes=[pltpu.VMEM((2, SLOT_TILE, TOK_EXT), F32),
                        pltpu.VMEM((D_MODEL, EXPERT_FF), BF16), pltpu.VMEM((D_MODEL, EXPERT_FF), BF16),
                        pltpu.VMEM((EXPERT_FF, D_MODEL), BF16), pltpu.SemaphoreType.DMA((2,))],
    )
    return pl.pallas_call(
        _ffn_kernel,
        grid_spec=grid_spec,
        out_shape=jax.ShapeDtypeStruct((N_EXPERTS * cap, TOK_EXT), F32),
        compiler_params=_cparams("arbitrary", "arbitrary"),
        name="expert_ffn",
    )(idx_flat, x1e, g2, wg, wu, wd)


WIN = 8
STACK_TILE = 256


def _stack_rows(tc):
    rows = N_EXPERTS * (tc + 2 * (WIN - 1))
    return -(-rows // STACK_TILE) * STACK_TILE


def _combine_kernel(off_ref, x1_ref, ye_hbm, *rest, cap, n_blocks, final):
    if final:
        fg_ref, out_ref, ys_ref, sem = rest
    else:
        out_ref, ys_ref, sem = rest
    j = pl.program_id(0)
    tc = COMBINE_TILE
    buf = j % 2

    def windows(tile, e):
        lo = off_ref[e * (n_blocks + 1) + tile]
        hi = off_ref[e * (n_blocks + 1) + tile + 1]
        start = (lo >> 3) << 3
        return start, jnp.where(hi > lo, (hi - start + (WIN - 1)) >> 3, 0)

    def stacked_rows(tile):
        total = jnp.int32(0)
        for e in range(N_EXPERTS):
            total = total + windows(tile, e)[1] * WIN
        return total

    def fetch(tile, buf_):
        base = jnp.int32(0)
        for e in range(N_EXPERTS):
            start, n_win = windows(tile, e)

            def issue(w, carry, e=e, start=start, base=base):
                src = pl.multiple_of(e * cap + start + w * WIN, WIN)
                dst = pl.multiple_of(base + w * WIN, WIN)
                pltpu.make_async_copy(ye_hbm.at[pl.ds(src, WIN)], ys_ref.at[buf_, pl.ds(dst, WIN)], sem.at[buf_]).start()
                return carry

            lax.fori_loop(0, n_win, issue, 0)
            base = base + n_win * WIN

    @pl.when(j == 0)
    def _():
        ys_ref[...] = jnp.zeros_like(ys_ref)
        fetch(0, 0)

    base = stacked_rows(j)
    n_total = base >> 3
    for bit in range((_stack_rows(tc) // WIN).bit_length()):
        @pl.when((n_total & (1 << bit)) != 0)
        def _(bit=bit):
            rows = WIN << bit
            pltpu.make_async_copy(ye_hbm.at[pl.ds(0, rows)], ye_hbm.at[pl.ds(0, rows)], sem.at[buf]).wait()

    @pl.when(j + 1 < n_blocks)
    def _():
        fetch(j + 1, 1 - buf)

    tok0 = (j * tc).astype(F32)
    lane_t = lax.broadcasted_iota(I32, (STACK_TILE, tc), 1).astype(F32)
    row_i = lax.broadcasted_iota(I32, (STACK_TILE, 1), 0)

    def accumulate(kt, acc):
        r0 = pl.multiple_of(kt * STACK_TILE, STACK_TILE)
        rows = ys_ref[buf, pl.ds(r0, STACK_TILE), :]
        tok_local = rows[:, D_MODEL + TOKEN_ID_LANE:D_MODEL + TOKEN_ID_LANE + 1] - tok0
        tok_local = jnp.where(r0 + row_i < base, tok_local, -1.0)
        onehot = jnp.where(tok_local == lane_t, 1.0, 0.0).astype(BF16)
        out_ref[...] += lax.dot_general(onehot, rows[:, :D_MODEL].astype(BF16), (((0,), (0,)), ((), ())),
                                        preferred_element_type=F32)
        return acc

    n_kt = (base + (STACK_TILE - 1)) // STACK_TILE
    out_ref[...] = x1_ref[...]
    lax.fori_loop(0, n_kt, accumulate, 0)
    if final:
        out_ref[...] = _rms(out_ref[...], fg_ref[...])


def combine(off_flat, x1e, ye, cap, final_g=None):
    n = x1e.shape[0]
    tc = COMBINE_TILE
    nb = n // tc
    final = final_g is not None
    in_specs = [pl.BlockSpec((tc, D_MODEL), lambda j, *_: (j, 0)),
                pl.BlockSpec(memory_space=pl.ANY)]
    args = [x1e, ye]
    if final:
        in_specs.append(pl.BlockSpec((1, D_MODEL), lambda j, *_: (0, 0)))
        args.append(final_g)
    grid_spec = pltpu.PrefetchScalarGridSpec(
        num_scalar_prefetch=1,
        grid=(nb,),
        in_specs=in_specs,
        out_specs=pl.BlockSpec((tc, D_MODEL), lambda j, *_: (j, 0)),
        scratch_shapes=[pltpu.VMEM((2, _stack_rows(tc), TOK_EXT), F32), pltpu.SemaphoreType.DMA((2,))],
    )
    return pl.pallas_call(
        functools.partial(_combine_kernel, cap=cap, n_blocks=nb, final=final),
        grid_spec=grid_spec,
        out_shape=jax.ShapeDtypeStruct((n, D_MODEL), F32),
        compiler_params=_cparams("arbitrary"),
        name="moe_combine",
    )(off_flat, *args)


def moe(x1e, aff_t, g2, wg, wu, wd, layer, final_g=None):
    n = x1e.shape[0]
    cap = n // CAPACITY_DIV
    idx, rank = route(aff_t)
    off = jnp.concatenate([rank[:, ::COMBINE_TILE], jnp.full((N_EXPERTS, 1), cap, I32)], axis=1)
    ye = expert_ffn(idx.reshape(-1), x1e, g2, wg, wu, wd, cap, layer)
    return combine(off.reshape(-1), x1e, ye, cap, final_g)


def kernel(x_prompt, x_sample, norm1_g, w_in, conv_w, gate_bias, head_norm_g, w_out, norm2_g, w_router, w_gate, w_up, w_down, final_g):
    depth = w_in.shape[0]
    w_in_p = jnp.pad(w_in, ((0, 0), (0, 0), (0, D_IN_PAD - D_IN))).astype(BF16)
    bias_p = jnp.pad(gate_bias.reshape(depth, 1, 4 * N_HEADS), ((0, 0), (0, 0), (0, LANES - 4 * N_HEADS)))
    cw_p = jnp.pad(conv_w, ((0, 0), (0, 8 - conv_w.shape[1]), (0, 0)))
    wr_f = jnp.pad(w_router, ((0, 0), (0, 0), (0, LANES - N_EXPERTS)))
    wr_hi = wr_f.astype(BF16)
    wr_p = jnp.concatenate([wr_hi, (wr_f - wr_hi.astype(F32)).astype(BF16)], axis=-1)
    w_out_b = w_out.astype(BF16)
    fg = final_g.reshape(1, D_MODEL)

    outs = []
    for x in (x_prompt, x_sample):
        bsz, seq, _ = x.shape
        xf = x.reshape(bsz * seq, D_MODEL)
        for l in range(depth):
            gb, gcu, q, k, v, osig, grow = inproj(xf, norm1_g[l].reshape(1, -1), w_in_p[l], bias_p[l])
            hf, hb = mlstm(q, k, v, grow, seq)
            g2 = norm2_g[l].reshape(1, -1)
            x1e, aff_t = outproj(gb, gcu, hf, hb, osig, head_norm_g[l].reshape(1, -1), xf, cw_p[l], w_out_b[l],
                                 g2, wr_p[l], seq)
            xf = moe(x1e, aff_t, g2, w_gate, w_up, w_down, l, fg if l == depth - 1 else None)
        outs.append(xf.reshape(bsz, seq, D_MODEL))
    return tuple(outs)
```

```python
import functools

import jax
import jax.numpy as jnp
from jax import lax
from jax.experimental import pallas as pl
from jax.experimental.pallas import tpu as pltpu

F32 = jnp.float32
BF16 = jnp.bfloat16
I32 = jnp.int32

LANES = 128
N_EXPERTS = 16
CAPACITY_DIV = 8
DISP_BITS = 16
VALID_BIT = 24
TOKEN_ID_LANE = N_EXPERTS


def _threshold_kernel(aff_ref, thr_ref, need_ref, *, cap):
    bits = pltpu.bitcast(aff_ref[...], I32)
    cap_f = jnp.float32(cap)

    def count_ge(cand):
        return jnp.sum(jnp.where(bits >= cand, 1.0, 0.0), axis=1, keepdims=True)

    def body(i, thr):
        cand = thr | jnp.left_shift(jnp.int32(1), 30 - i)
        return jnp.where(count_ge(cand) >= cap_f, cand, thr)

    thr = lax.fori_loop(0, 31, body, jnp.zeros((N_EXPERTS, 1), I32))
    n_gt = jnp.sum(jnp.where(bits > thr, 1.0, 0.0), axis=1, keepdims=True)
    need = (cap_f - n_gt).astype(I32)
    thr_ref[...] = jnp.broadcast_to(thr, thr_ref.shape)
    need_ref[...] = jnp.broadcast_to(need, need_ref.shape)


def _lane_inclusive_scan(x, lane):
    for b in range(7):
        s = 1 << b
        x = x + jnp.where(lane >= s, pltpu.roll(x, s, axis=1), 0.0)
    return x


def _row_exclusive_scan(t, row, n_rows):
    inc = t
    s = 1
    while s < n_rows:
        inc = inc + jnp.where(row >= s, pltpu.roll(inc, s, axis=0), 0.0)
        s *= 2
    return inc - t


def _token_exclusive_scan(x, lane, row, n_rows):
    inc = _lane_inclusive_scan(x, lane)
    tot = jnp.broadcast_to(inc[:, LANES - 1:LANES], x.shape)
    return inc - x + _row_exclusive_scan(tot, row, n_rows)


def _compact_kernel(thr_ref, need_ref, aff_ref, idx_ref, rank_ref, *, n_rows, cap_rows):
    e = pl.program_id(0)
    shape = (n_rows, LANES)
    lane = lax.broadcasted_iota(I32, shape, 1)
    row = lax.broadcasted_iota(I32, shape, 0)
    bits = pltpu.bitcast(aff_ref[0], I32)
    thr = thr_ref[e]
    need = need_ref[e].astype(F32)
    eq = bits == thr
    pre_eq = _token_exclusive_scan(jnp.where(eq, 1.0, 0.0), lane, row, n_rows)
    sel = (bits > thr) | (eq & (pre_eq < need))
    sel_f = jnp.where(sel, 1.0, 0.0)
    rank = _token_exclusive_scan(sel_f, lane, row, n_rows).astype(I32)
    rank_ref[0] = rank

    pos = row * LANES + lane
    disp = pos - rank
    v = jnp.where(sel, disp | (1 << VALID_BIT), 0)
    n_bits = (n_rows * LANES - 1).bit_length()
    for b in range(n_bits):
        if b < 7:
            s = 1 << b
            r1 = pltpu.roll(v, LANES - s, axis=1)
            r2 = pltpu.roll(r1, n_rows - 1, axis=0)
            moved = jnp.where(lane < LANES - s, r1, r2)
        else:
            sr = 1 << (b - 7)
            moved = pltpu.roll(v, n_rows - sr, axis=0)
        take = ((moved >> VALID_BIT) & 1 == 1) & ((moved >> b) & 1 == 1)
        stay = ((v >> VALID_BIT) & 1 == 1) & ((v >> b) & 1 == 0)
        v = jnp.where(take, moved, jnp.where(stay, v, 0))
    idx_ref[0] = (pos + (v & ((1 << DISP_BITS) - 1)))[:cap_rows]


def route(aff_t):
    n_exp, n = aff_t.shape
    assert n <= (1 << DISP_BITS)
    cap = n // CAPACITY_DIV
    n_rows = n // LANES
    cap_rows = cap // LANES
    thr, need = pl.pallas_call(
        functools.partial(_threshold_kernel, cap=cap),
        out_shape=(jax.ShapeDtypeStruct((n_exp, LANES), I32), jax.ShapeDtypeStruct((n_exp, LANES), I32)),
        name="route_threshold",
    )(aff_t)
    grid_spec = pltpu.PrefetchScalarGridSpec(
        num_scalar_prefetch=2,
        grid=(n_exp,),
        in_specs=[pl.BlockSpec((1, n_rows, LANES), lambda e, *_: (e, 0, 0))],
        out_specs=[
            pl.BlockSpec((1, cap_rows, LANES), lambda e, *_: (e, 0, 0)),
            pl.BlockSpec((1, n_rows, LANES), lambda e, *_: (e, 0, 0)),
        ],
    )
    idx, rank = pl.pallas_call(
        functools.partial(_compact_kernel, n_rows=n_rows, cap_rows=cap_rows),
        grid_spec=grid_spec,
        out_shape=(
            jax.ShapeDtypeStruct((n_exp, cap_rows, LANES), I32),
            jax.ShapeDtypeStruct((n_exp, n_rows, LANES), I32),
        ),
        compiler_params=pltpu.CompilerParams(dimension_semantics=("parallel",)),
        name="route_compact",
    )(thr[:, 0], need[:, 0], aff_t.reshape(n_exp, n_rows, LANES))
    return idx.reshape(n_exp, cap), rank.reshape(n_exp, n)


D_MODEL = 1024
CONV_W = 512
N_HEADS = 4
V_DIM = 128
QK_DIM = 64
QK_W = N_HEADS * QK_DIM
V_W = N_HEADS * V_DIM
CHUNK = 128
EXPERT_FF = 1024
EPS = 1e-6
C_GB, C_GC, C_U, C_Q, C_K, C_V, C_O, C_G = 0, 512, 1024, 1536, 1792, 2048, 2560, 3072
D_IN = 3088
D_IN_PAD = 3200
TOK_EXT = D_MODEL + LANES
HALF = D_MODEL // 2
TOK_PACK = HALF + LANES

ROW_TILE = 512
INPROJ_TILE = 1024
MLSTM_BLOCK = 256
SLOT_TILE = 512
COMBINE_TILE = 256
VMEM_LIMIT = 56 * 1024 * 1024


def _cparams(*sem):
    return pltpu.CompilerParams(dimension_semantics=sem, vmem_limit_bytes=VMEM_LIMIT)


def _rms(x, g):
    return x * lax.rsqrt(jnp.mean(x * x, axis=-1, keepdims=True) + EPS) * g


GATE_GROUP = 8
GATE_ROWS = 6 * GATE_GROUP
G_A, G_PM, G_B = 0, 1, 2


def _chunk_scan(x, op, identity, reverse):
    width = x.shape[1]
    pos = lax.broadcasted_iota(I32, x.shape, 1) & (CHUNK - 1)
    for b in range(7):
        s = 1 << b
        if reverse:
            shifted = jnp.where(pos < CHUNK - s, pltpu.roll(x, width - s, axis=1), identity)
        else:
            shifted = jnp.where(pos >= s, pltpu.roll(x, s, axis=1), identity)
        x = op(x, shifted)
    return x


def _inproj_kernel(x_ref, g_ref, w_ref, bias_ref, gb_ref, gcu_ref, q_ref, k_ref, v_ref, os_ref, grow_ref):
    hn = _rms(x_ref[...], g_ref[...]).astype(BF16)

    def seg(a, b):
        return jnp.dot(hn, w_ref[:, a:b], preferred_element_type=F32)

    gates = seg(C_G, D_IN_PAD) + bias_ref[...]
    lane = lax.broadcasted_iota(I32, gates.shape, 1)
    log_sig = jnp.minimum(gates, 0.0) - jnp.log1p(jnp.exp(-jnp.abs(gates)))
    g16 = jnp.where((lane >> 2) & 1 == 1, log_sig, gates).T[:4 * N_HEADS]
    fwd = lax.broadcasted_iota(I32, g16.shape, 0) < 2 * N_HEADS
    cs = jnp.where(fwd, _chunk_scan(g16, jnp.add, 0.0, False), _chunk_scan(g16, jnp.add, 0.0, True))
    cs = pltpu.roll(cs, 3 * N_HEADS, axis=0)
    a = g16 - cs
    neg_inf = jnp.float32(-jnp.inf)
    pm = jnp.where(fwd, _chunk_scan(a, jnp.maximum, neg_inf, False), _chunk_scan(a, jnp.maximum, neg_inf, True))
    for d in range(2):
        grp = slice(d * GATE_GROUP, (d + 1) * GATE_GROUP)
        for g, val in ((G_A, a), (G_PM, pm), (G_B, cs)):
            r0 = (3 * d + g) * GATE_GROUP
            grow_ref[r0:r0 + GATE_GROUP, :] = val[grp]

    gb_ref[...] = seg(C_GB, C_GC).astype(BF16)
    gcu_ref[...] = (seg(C_GC, C_U) * seg(C_U, C_Q)).astype(BF16)
    q_ref[...] = (seg(C_Q, C_K) * (QK_DIM ** -0.5)).astype(BF16)
    k_ref[...] = seg(C_K, C_V).astype(BF16)
    v_ref[...] = seg(C_V, C_O).astype(BF16)
    os_ref[...] = jax.nn.sigmoid(seg(C_O, C_G)).astype(BF16)


def inproj(x, g1, w_in_p, bias_p):
    n = x.shape[0]
    tm = INPROJ_TILE
    row = lambda w: pl.BlockSpec((tm, w), lambda i: (i, 0))
    full = lambda a: pl.BlockSpec(a.shape, lambda i: (0,) * a.ndim)
    return pl.pallas_call(
        _inproj_kernel,
        grid=(n // tm,),
        in_specs=[row(D_MODEL), full(g1), full(w_in_p), full(bias_p)],
        out_specs=[row(CONV_W), row(CONV_W), row(QK_W), row(QK_W), row(V_W), row(V_W),
                   pl.BlockSpec((GATE_ROWS, tm), lambda i: (0, i))],
        out_shape=[
            jax.ShapeDtypeStruct((n, CONV_W), BF16), jax.ShapeDtypeStruct((n, CONV_W), BF16),
            jax.ShapeDtypeStruct((n, QK_W), BF16), jax.ShapeDtypeStruct((n, QK_W), BF16),
            jax.ShapeDtypeStruct((n, V_W), BF16), jax.ShapeDtypeStruct((n, V_W), BF16),
            jax.ShapeDtypeStruct((GATE_ROWS, n), F32),
        ],
        compiler_params=_cparams("parallel"),
        name="mixer_inproj",
    )(x, g1, w_in_p, bias_p)


def _mlstm_kernel(qf_ref, kf_ref, vf_ref, gf_ref, qb_ref, kb_ref, vb_ref, gb_ref, hf_ref, hb_ref,
                  cf_ref, mf_ref, cb_ref, mb_ref, *, seq_len):
    j = pl.program_id(0)

    @pl.when(((j * MLSTM_BLOCK) % seq_len) == 0)
    def _():
        for ref in (cf_ref, mf_ref, cb_ref, mb_ref):
            ref[...] = jnp.zeros_like(ref)

    _mlstm_block(qf_ref, kf_ref, vf_ref, gf_ref, hf_ref, cf_ref, mf_ref, reverse=False)
    _mlstm_block(qb_ref, kb_ref, vb_ref, gb_ref, hb_ref, cb_ref, mb_ref, reverse=True)


def _mlstm_block(q_ref, k_ref, v_ref, grow_ref, out_ref, c_ref, m_ref, *, reverse):
    blk = MLSTM_BLOCK
    t_i = lax.broadcasted_iota(I32, (CHUNK, CHUNK), 0)
    s_i = lax.broadcasted_iota(I32, (CHUNK, CHUNK), 1)
    tri = (s_i >= t_i) if reverse else (s_i <= t_i)
    lane = lax.broadcasted_iota(I32, (CHUNK, LANES), 1)
    half_masks = [jnp.where((lane >> 6) == hh, 1.0, 0.0).astype(BF16) for hh in range(2)]
    ones_col = jnp.where(lane == 0, 1.0, 0.0).astype(BF16)
    d = 3 if reverse else 0

    def gate_rows(g, cols):
        r0 = (d + g) * GATE_GROUP
        return grow_ref[r0:r0 + N_HEADS, cols]

    n_chunks = blk // CHUNK
    order = list(range(n_chunks - 1, -1, -1) if reverse else range(n_chunks))
    end = 0 if reverse else CHUNK - 1
    m_old = m_ref[0:N_HEADS, :]
    m_in = {}
    for c in order:
        last = slice(c * CHUNK + end, c * CHUNK + end + 1)
        m_in[c] = m_old
        m_old = gate_rows(G_B, last) + jnp.maximum(m_old, gate_rows(G_PM, last))
    m_ref[0:N_HEADS, :] = m_old

    c_state = [c_ref[h] for h in range(N_HEADS)]
    for c in order:
        r0 = c * CHUNK
        rows = slice(r0, r0 + CHUNK)
        m_o = m_in[c]
        a = gate_rows(G_A, rows)
        mm = jnp.maximum(m_o, gate_rows(G_PM, rows))
        mm_last = jnp.maximum(m_o, gate_rows(G_PM, slice(r0 + end, r0 + end + 1)))
        sc = jnp.exp(m_o - mm)
        emt = jnp.exp(-(gate_rows(G_B, rows) + mm))
        w = jnp.exp(a - mm_last)
        decay = jnp.exp(m_o - mm_last)
        stack = jnp.concatenate([mm, sc, emt, jnp.zeros((LANES - 3 * N_HEADS, CHUNK), F32)], axis=0)
        cols = stack.T
        for h in range(N_HEADS):
            pair = slice((h // 2) * LANES, (h // 2 + 1) * LANES)
            hv = slice(h * V_DIM, (h + 1) * V_DIM)
            q2 = q_ref[rows, pair]
            km = k_ref[rows, pair] * half_masks[h % 2]
            vh = v_ref[rows, hv]
            c_old = c_state[h]
            mm_col, sc_col = cols[:, h:h + 1], cols[:, N_HEADS + h:N_HEADS + h + 1]
            emt_col = cols[:, 2 * N_HEADS + h:2 * N_HEADS + h + 1]
            dmat = jnp.where(tri, jnp.exp(a[h:h + 1, :] - mm_col), 0.0)
            s_mat = lax.dot_general(q2, km, (((1,), (1,)), ((), ())), preferred_element_type=F32) * dmat
            qc = jnp.dot(q2, c_old.astype(BF16), preferred_element_type=F32)
            num = jnp.dot(s_mat.astype(BF16), vh, preferred_element_type=F32) + sc_col * qc[:, :V_DIM]
            den = jnp.sum(s_mat, axis=1, keepdims=True) + sc_col * qc[:, V_DIM:V_DIM + 1]
            h_out = num * (1.0 / jnp.maximum(jnp.abs(den), emt_col))
            kmt = (km.astype(F32).T * w[h:h + 1, :]).astype(BF16)
            kv = jnp.dot(kmt, jnp.concatenate([vh, ones_col], axis=1), preferred_element_type=F32)
            c_state[h] = decay[h:h + 1, 0:1] * c_old + kv
            out_ref[rows, hv] = h_out.astype(BF16)
    for h in range(N_HEADS):
        c_ref[h] = c_state[h]


def mlstm(q, k, v, grow, seq_len):
    n = q.shape[0]
    blk = MLSTM_BLOCK
    nb = n // blk
    assert n % seq_len == 0 and seq_len % blk == 0
    fwd = lambda w: pl.BlockSpec((blk, w), lambda j: (j, 0))
    bwd = lambda w: pl.BlockSpec((blk, w), lambda j: (nb - 1 - j, 0))
    state = [pltpu.VMEM((N_HEADS, LANES, 2 * LANES), F32), pltpu.VMEM((8, LANES), F32)]
    return pl.pallas_call(
        functools.partial(_mlstm_kernel, seq_len=seq_len),
        grid=(nb,),
        in_specs=[fwd(QK_W), fwd(QK_W), fwd(V_W), pl.BlockSpec((GATE_ROWS, blk), lambda j: (0, j)),
                  bwd(QK_W), bwd(QK_W), bwd(V_W), pl.BlockSpec((GATE_ROWS, blk), lambda j: (0, nb - 1 - j))],
        out_specs=[fwd(V_W), bwd(V_W)],
        out_shape=[jax.ShapeDtypeStruct((n, V_W), BF16), jax.ShapeDtypeStruct((n, V_W), BF16)],
        scratch_shapes=state + state,
        compiler_params=_cparams("arbitrary"),
        name="mlstm",
    )(q, k, v, grow, q, k, v, grow)


HALO = 16


def _outproj_kernel(gb_ref, gcu_ref, gprev_ref, gnext_ref, hf_ref, hb_ref, os_ref, hng_ref, x_ref, cw_ref, wo_ref,
                    g2_ref, wr_ref, x1_ref, xp_ref, afft_ref, *, seq_len):
    i = pl.program_id(0)
    tm = ROW_TILE
    first = ((i * tm) % seq_len) == 0
    last = (((i + 1) * tm) % seq_len) == 0
    g = gcu_ref[...].astype(F32)
    prev_row = jnp.where(first, 0.0, gprev_ref[HALO - 1:HALO, :].astype(F32))
    next_row = jnp.where(last, 0.0, gnext_ref[0:1, :].astype(F32))
    rid = lax.broadcasted_iota(I32, g.shape, 0)
    dn = jnp.where(rid == 0, prev_row, pltpu.roll(g, 1, axis=0))
    up = jnp.where(rid == tm - 1, next_row, pltpu.roll(g, tm - 1, axis=0))
    conv = dn * cw_ref[0:1, :] + g * cw_ref[1:2, :] + up * cw_ref[2:3, :]
    co = (gb_ref[...].astype(F32) * conv).astype(BF16)
    y = jnp.dot(co, wo_ref[:CONV_W, :], preferred_element_type=F32)
    heads = []
    for h in range(N_HEADS):
        hv = slice(h * V_DIM, (h + 1) * V_DIM)
        ht = _rms(hf_ref[:, hv].astype(F32) + hb_ref[:, hv].astype(F32), hng_ref[:, hv])
        heads.append((os_ref[:, hv].astype(F32) * ht).astype(BF16))
    mo = jnp.concatenate(heads, axis=1)
    y = y + jnp.dot(mo, wo_ref[CONV_W:, :], preferred_element_type=F32)
    x1 = x_ref[...] + y
    x1_ref[...] = x1
    tokens = _rms(x1, g2_ref[...])
    t_hi = tokens.astype(BF16)
    t_lo = (tokens - t_hi.astype(F32)).astype(BF16)
    p_hi = jnp.dot(t_hi, wr_ref[...], preferred_element_type=F32)
    logits = p_hi[:, :LANES] + p_hi[:, LANES:] + jnp.dot(t_lo, wr_ref[:, :LANES], preferred_element_type=F32)
    lane = lax.broadcasted_iota(I32, logits.shape, 1)
    logits = jnp.where(lane < N_EXPERTS, logits, -jnp.inf)
    ex = jnp.exp(logits - jnp.max(logits, axis=-1, keepdims=True))
    aff = ex / jnp.sum(ex, axis=-1, keepdims=True)
    lo = pltpu.bitcast(x1[:, :HALF].astype(BF16).astype(F32), I32)
    hi = pltpu.bitcast(x1[:, HALF:].astype(BF16).astype(F32), I32)
    xp_ref[:, :HALF] = (hi & jnp.int32(-65536)) | lax.shift_right_logical(lo, 16)
    row_id = (i * tm + lax.broadcasted_iota(I32, aff.shape, 0)).astype(F32)
    xp_ref[:, HALF:] = pltpu.bitcast(jnp.where(lane == TOKEN_ID_LANE, row_id, aff), I32)
    afft_ref[...] = aff.T[:N_EXPERTS]


def outproj(gb, gcu, hf, hb, osig, hng, x, cw_p, w_out_b, g2, wr_p, seq_len):
    n = x.shape[0]
    tm = ROW_TILE
    halos = tm // HALO
    n_halo = n // HALO
    row = lambda w: pl.BlockSpec((tm, w), lambda i: (i, 0))
    full = lambda a: pl.BlockSpec(a.shape, lambda i: (0,) * a.ndim)
    prev = pl.BlockSpec((HALO, CONV_W), lambda i: (jnp.maximum(i * halos - 1, 0), 0))
    nxt = pl.BlockSpec((HALO, CONV_W), lambda i: (jnp.minimum((i + 1) * halos, n_halo - 1), 0))
    return pl.pallas_call(
        functools.partial(_outproj_kernel, seq_len=seq_len),
        grid=(n // tm,),
        in_specs=[row(CONV_W), row(CONV_W), prev, nxt, row(V_W), row(V_W), row(V_W), full(hng), row(D_MODEL),
                  full(cw_p), full(w_out_b), full(g2), full(wr_p)],
        out_specs=[row(D_MODEL), row(TOK_PACK), pl.BlockSpec((N_EXPERTS, tm), lambda i: (0, i))],
        out_shape=[jax.ShapeDtypeStruct((n, D_MODEL), F32), jax.ShapeDtypeStruct((n, TOK_PACK), I32),
                   jax.ShapeDtypeStruct((N_EXPERTS, n), F32)],
        compiler_params=_cparams("parallel"),
        name="mixer_outproj_router",
    )(gb, gcu, gcu, gcu, hf, hb, osig, hng, x, cw_p, w_out_b, g2, wr_p)


def _ffn_kernel(idx_ref, tok_hbm, g2_ref, wg_ref, wu_ref, wd_ref, ye_ref, xg_ref, wgb_ref, wub_ref, wdb_ref, sem):
    e = pl.program_id(0)
    s = pl.program_id(1)
    n_steps = pl.num_programs(1)
    ts = SLOT_TILE
    t = e * n_steps + s
    last = N_EXPERTS * n_steps - 1

    def row_copy(tile, buf, i):
        tok = idx_ref[tile * ts + i]
        return pltpu.make_async_copy(tok_hbm.at[pl.ds(tok, 1)], xg_ref.at[buf, pl.ds(i, 1)], sem.at[buf])

    def wait_rows(buf):
        pltpu.make_async_copy(tok_hbm.at[pl.ds(0, ts)], xg_ref.at[buf], sem.at[buf]).wait()

    @pl.when(t == 0)
    def _():
        def issue(i, carry):
            row_copy(0, 0, i).start()
            return carry

        lax.fori_loop(0, ts, issue, 0, unroll=8)

    @pl.when(s == 0)
    def _():
        wgb_ref[...] = wg_ref[0, 0].astype(BF16)
        wub_ref[...] = wu_ref[0, 0].astype(BF16)
        wdb_ref[...] = wd_ref[0, 0].astype(BF16)

    def half(buf, next_tile, rows):
        wait_rows(buf)
        words = xg_ref[buf]
        x_lo = pltpu.bitcast(words[:, :HALF] << 16, F32)
        x_hi = pltpu.bitcast(words[:, :HALF] & jnp.int32(-65536), F32)
        ext = pltpu.bitcast(words[:, HALF:], F32)
        ms = (jnp.sum(x_lo * x_lo, axis=-1, keepdims=True) + jnp.sum(x_hi * x_hi, axis=-1, keepdims=True)) / D_MODEL
        inv = lax.rsqrt(ms + EPS)
        xb_lo = (x_lo * inv * g2_ref[:, :HALF]).astype(BF16)
        xb_hi = (x_hi * inv * g2_ref[:, HALF:]).astype(BF16)
        lane = lax.broadcasted_iota(I32, ext.shape, 1)
        gate = jnp.sum(jnp.where(lane == e, ext, 0.0), axis=1, keepdims=True)
        for i in range(ts):
            row_copy(next_tile, 1 - buf, i).start(priority=i % 2)
        hg = (jnp.dot(xb_lo, wgb_ref[:HALF, :], preferred_element_type=F32)
              + jnp.dot(xb_hi, wgb_ref[HALF:, :], preferred_element_type=F32))
        hu = (jnp.dot(xb_lo, wub_ref[:HALF, :], preferred_element_type=F32)
              + jnp.dot(xb_hi, wub_ref[HALF:, :], preferred_element_type=F32))
        hid = (hg * jax.nn.sigmoid(hg) * hu).astype(BF16)
        ye_ref[rows, :D_MODEL] = jnp.dot(hid, wdb_ref[...], preferred_element_type=F32) * gate
        ye_ref[rows, D_MODEL:] = ext

    half(0, 2 * t + 1, slice(0, ts))
    half(1, jnp.minimum(2 * t + 2, 2 * last), slice(ts, 2 * ts))

    @pl.when(t == last)
    def _():
        wait_rows(0)


def expert_ffn(idx_flat, xp, g2, wg, wu, wd, cap, layer):
    assert cap % (2 * SLOT_TILE) == 0
    n_steps = cap // (2 * SLOT_TILE)
    wspec = lambda: pl.BlockSpec((1, 1, D_MODEL, EXPERT_FF), lambda e, s, *_: (layer, e, 0, 0))
    grid_spec = pltpu.PrefetchScalarGridSpec(
        num_scalar_prefetch=1,
        grid=(N_EXPERTS, n_steps),
        in_specs=[pl.BlockSpec(memory_space=pl.ANY), pl.BlockSpec((1, D_MODEL), lambda e, s, *_: (0, 0)), wspec(), wspec(),
                  pl.BlockSpec((1, 1, EXPERT_FF, D_MODEL), lambda e, s, *_: (layer, e, 0, 0))],
        out_specs=pl.BlockSpec((2 * SLOT_TILE, TOK_EXT), lambda e, s, *_: (e * n_steps + s, 0)),
        scratch_shapes=[pltpu.VMEM((2, SLOT_TILE, TOK_PACK), I32),
                        pltpu.VMEM((D_MODEL, EXPERT_FF), BF16), pltpu.VMEM((D_MODEL, EXPERT_FF), BF16),
                        pltpu.VMEM((EXPERT_FF, D_MODEL), BF16), pltpu.SemaphoreType.DMA((2,))],
    )
    return pl.pallas_call(
        _ffn_kernel,
        grid_spec=grid_spec,
        out_shape=jax.ShapeDtypeStruct((N_EXPERTS * cap, TOK_EXT), F32),
        compiler_params=_cparams("arbitrary", "arbitrary"),
        name="expert_ffn",
    )(idx_flat, xp, g2, wg, wu, wd)


WIN = 8
STACK_TILE = 256


def _stack_rows(tc):
    rows = N_EXPERTS * (tc + 2 * (WIN - 1))
    return -(-rows // STACK_TILE) * STACK_TILE


def _combine_kernel(off_ref, x1_ref, ye_hbm, *rest, cap, n_blocks, final):
    if final:
        fg_ref, out_ref, ys_ref, sem = rest
    else:
        out_ref, ys_ref, sem = rest
    j = pl.program_id(0)
    tc = COMBINE_TILE
    buf = j % 2

    def windows(tile, e):
        lo = off_ref[e * (n_blocks + 1) + tile]
        hi = off_ref[e * (n_blocks + 1) + tile + 1]
        start = (lo >> 3) << 3
        return start, jnp.where(hi > lo, (hi - start + (WIN - 1)) >> 3, 0)

    def stacked_rows(tile):
        total = jnp.int32(0)
        for e in range(N_EXPERTS):
            total = total + windows(tile, e)[1] * WIN
        return total

    def fetch(tile, buf_):
        base = jnp.int32(0)
        for e in range(N_EXPERTS):
            start, n_win = windows(tile, e)

            def issue(w, carry, e=e, start=start, base=base):
                src = pl.multiple_of(e * cap + start + w * WIN, WIN)
                dst = pl.multiple_of(base + w * WIN, WIN)
                pltpu.make_async_copy(ye_hbm.at[pl.ds(src, WIN)], ys_ref.at[buf_, pl.ds(dst, WIN)], sem.at[buf_]).start()
                return carry

            lax.fori_loop(0, n_win, issue, 0)
            base = base + n_win * WIN

    @pl.when(j == 0)
    def _():
        ys_ref[...] = jnp.zeros_like(ys_ref)
        fetch(0, 0)

    base = stacked_rows(j)
    n_total = base >> 3
    for bit in range((_stack_rows(tc) // WIN).bit_length()):
        @pl.when((n_total & (1 << bit)) != 0)
        def _(bit=bit):
            rows = WIN << bit
            pltpu.make_async_copy(ye_hbm.at[pl.ds(0, rows)], ye_hbm.at[pl.ds(0, rows)], sem.at[buf]).wait()

    @pl.when(j + 1 < n_blocks)
    def _():
        fetch(j + 1, 1 - buf)

    tok0 = (j * tc).astype(F32)
    lane_t = lax.broadcasted_iota(I32, (STACK_TILE, tc), 1).astype(F32)
    row_i = lax.broadcasted_iota(I32, (STACK_TILE, 1), 0)

    def accumulate(kt, acc):
        r0 = pl.multiple_of(kt * STACK_TILE, STACK_TILE)
        rows = ys_ref[buf, pl.ds(r0, STACK_TILE), :]
        tok_local = rows[:, D_MODEL + TOKEN_ID_LANE:D_MODEL + TOKEN_ID_LANE + 1] - tok0
        tok_local = jnp.where(r0 + row_i < base, tok_local, -1.0)
        onehot = jnp.where(tok_local == lane_t, 1.0, 0.0).astype(BF16)
        out_ref[...] += lax.dot_general(onehot, rows[:, :D_MODEL].astype(BF16), (((0,), (0,)), ((), ())),
                                        preferred_element_type=F32)
        return acc

    n_kt = (base + (STACK_TILE - 1)) // STACK_TILE
    out_ref[...] = x1_ref[...]
    lax.fori_loop(0, n_kt, accumulate, 0)
    if final:
        out_ref[...] = _rms(out_ref[...], fg_ref[...])


def combine(off_flat, x1, ye, cap, final_g=None):
    n = x1.shape[0]
    tc = COMBINE_TILE
    nb = n // tc
    final = final_g is not None
    in_specs = [pl.BlockSpec((tc, D_MODEL), lambda j, *_: (j, 0)),
                pl.BlockSpec(memory_space=pl.ANY)]
    args = [x1, ye]
    if final:
        in_specs.append(pl.BlockSpec((1, D_MODEL), lambda j, *_: (0, 0)))
        args.append(final_g)
    grid_spec = pltpu.PrefetchScalarGridSpec(
        num_scalar_prefetch=1,
        grid=(nb,),
        in_specs=in_specs,
        out_specs=pl.BlockSpec((tc, D_MODEL), lambda j, *_: (j, 0)),
        scratch_shapes=[pltpu.VMEM((2, _stack_rows(tc), TOK_EXT), F32), pltpu.SemaphoreType.DMA((2,))],
    )
    return pl.pallas_call(
        functools.partial(_combine_kernel, cap=cap, n_blocks=nb, final=final),
        grid_spec=grid_spec,
        out_shape=jax.ShapeDtypeStruct((n, D_MODEL), F32),
        compiler_params=_cparams("arbitrary"),
        name="moe_combine",
    )(off_flat, *args)


def moe(x1, xp, aff_t, g2, wg, wu, wd, layer, final_g=None):
    n = x1.shape[0]
    cap = n // CAPACITY_DIV
    idx, rank = route(aff_t)
    off = jnp.concatenate([rank[:, ::COMBINE_TILE], jnp.full((N_EXPERTS, 1), cap, I32)], axis=1)
    ye = expert_ffn(idx.reshape(-1), xp, g2, wg, wu, wd, cap, layer)
    return combine(off.reshape(-1), x1, ye, cap, final_g)


def kernel(x_prompt, x_sample, norm1_g, w_in, conv_w, gate_bias, head_norm_g, w_out, norm2_g, w_router, w_gate, w_up, w_down, final_g):
    depth = w_in.shape[0]
    w_in_p = jnp.pad(w_in, ((0, 0), (0, 0), (0, D_IN_PAD - D_IN))).astype(BF16)
    bias_p = jnp.pad(gate_bias.reshape(depth, 1, 4 * N_HEADS), ((0, 0), (0, 0), (0, LANES - 4 * N_HEADS)))
    cw_p = jnp.pad(conv_w, ((0, 0), (0, 8 - conv_w.shape[1]), (0, 0)))
    wr_f = jnp.pad(w_router, ((0, 0), (0, 0), (0, LANES - N_EXPERTS)))
    wr_hi = wr_f.astype(BF16)
    wr_p = jnp.concatenate([wr_hi, (wr_f - wr_hi.astype(F32)).astype(BF16)], axis=-1)
    w_out_b = w_out.astype(BF16)
    fg = final_g.reshape(1, D_MODEL)

    outs = []
    for x in (x_prompt, x_sample):
        bsz, seq, _ = x.shape
        xf = x.reshape(bsz * seq, D_MODEL)
        for l in range(depth):
            gb, gcu, q, k, v, osig, grow = inproj(xf, norm1_g[l].reshape(1, -1), w_in_p[l], bias_p[l])
            hf, hb = mlstm(q, k, v, grow, seq)
            g2 = norm2_g[l].reshape(1, -1)
            x1, xp, aff_t = outproj(gb, gcu, hf, hb, osig, head_norm_g[l].reshape(1, -1), xf, cw_p[l], w_out_b[l],
                                    g2, wr_p[l], seq)
            xf = moe(x1, xp, aff_t, g2, w_gate, w_up, w_down, l, fg if l == depth - 1 else None)
        outs.append(xf.reshape(bsz, seq, D_MODEL))
    return tuple(outs)
```

```python
import functools

import jax
import jax.numpy as jnp
from jax import lax
from jax.experimental import pallas as pl
from jax.experimental.pallas import tpu as pltpu

F32 = jnp.float32
BF16 = jnp.bfloat16
I32 = jnp.int32

LANES = 128
N_EXPERTS = 16
CAPACITY_DIV = 8
DISP_BITS = 16
VALID_BIT = 24
TOKEN_ID_LANE = N_EXPERTS


def _threshold_kernel(aff_ref, thr_ref, need_ref, *, cap):
    bits = pltpu.bitcast(aff_ref[...], I32)
    cap_f = jnp.float32(cap)

    def count_ge(cand):
        return jnp.sum(jnp.where(bits >= cand, 1.0, 0.0), axis=1, keepdims=True)

    def body(i, thr):
        cand = thr | jnp.left_shift(jnp.int32(1), 30 - i)
        return jnp.where(count_ge(cand) >= cap_f, cand, thr)

    thr = lax.fori_loop(0, 31, body, jnp.zeros((N_EXPERTS, 1), I32))
    n_gt = jnp.sum(jnp.where(bits > thr, 1.0, 0.0), axis=1, keepdims=True)
    need = (cap_f - n_gt).astype(I32)
    thr_ref[...] = jnp.broadcast_to(thr, thr_ref.shape)
    need_ref[...] = jnp.broadcast_to(need, need_ref.shape)


def _lane_inclusive_scan(x, lane):
    for b in range(7):
        s = 1 << b
        x = x + jnp.where(lane >= s, pltpu.roll(x, s, axis=1), 0.0)
    return x


def _row_exclusive_scan(t, row, n_rows):
    inc = t
    s = 1
    while s < n_rows:
        inc = inc + jnp.where(row >= s, pltpu.roll(inc, s, axis=0), 0.0)
        s *= 2
    return inc - t


def _token_exclusive_scan(x, lane, row, n_rows):
    inc = _lane_inclusive_scan(x, lane)
    tot = jnp.broadcast_to(inc[:, LANES - 1:LANES], x.shape)
    return inc - x + _row_exclusive_scan(tot, row, n_rows)


def _compact_kernel(thr_ref, need_ref, aff_ref, idx_ref, rank_ref, *, n_rows, cap_rows):
    e = pl.program_id(0)
    shape = (n_rows, LANES)
    lane = lax.broadcasted_iota(I32, shape, 1)
    row = lax.broadcasted_iota(I32, shape, 0)
    bits = pltpu.bitcast(aff_ref[0], I32)
    thr = thr_ref[e]
    need = need_ref[e].astype(F32)
    eq = bits == thr
    pre_eq = _token_exclusive_scan(jnp.where(eq, 1.0, 0.0), lane, row, n_rows)
    sel = (bits > thr) | (eq & (pre_eq < need))
    sel_f = jnp.where(sel, 1.0, 0.0)
    rank = _token_exclusive_scan(sel_f, lane, row, n_rows).astype(I32)
    rank_ref[0] = rank

    pos = row * LANES + lane
    disp = pos - rank
    v = jnp.where(sel, disp | (1 << VALID_BIT), 0)
    n_bits = (n_rows * LANES - 1).bit_length()
    for b in range(n_bits):
        if b < 7:
            s = 1 << b
            r1 = pltpu.roll(v, LANES - s, axis=1)
            r2 = pltpu.roll(r1, n_rows - 1, axis=0)
            moved = jnp.where(lane < LANES - s, r1, r2)
        else:
            sr = 1 << (b - 7)
            moved = pltpu.roll(v, n_rows - sr, axis=0)
        take = ((moved >> VALID_BIT) & 1 == 1) & ((moved >> b) & 1 == 1)
        stay = ((v >> VALID_BIT) & 1 == 1) & ((v >> b) & 1 == 0)
        v = jnp.where(take, moved, jnp.where(stay, v, 0))
    idx_ref[0] = (pos + (v & ((1 << DISP_BITS) - 1)))[:cap_rows]


def route(aff_t):
    n_exp, n = aff_t.shape
    assert n <= (1 << DISP_BITS)
    cap = n // CAPACITY_DIV
    n_rows = n // LANES
    cap_rows = cap // LANES
    thr, need = pl.pallas_call(
        functools.partial(_threshold_kernel, cap=cap),
        out_shape=(jax.ShapeDtypeStruct((n_exp, LANES), I32), jax.ShapeDtypeStruct((n_exp, LANES), I32)),
        name="route_threshold",
    )(aff_t)
    grid_spec = pltpu.PrefetchScalarGridSpec(
        num_scalar_prefetch=2,
        grid=(n_exp,),
        in_specs=[pl.BlockSpec((1, n_rows, LANES), lambda e, *_: (e, 0, 0))],
        out_specs=[
            pl.BlockSpec((1, cap_rows, LANES), lambda e, *_: (e, 0, 0)),
            pl.BlockSpec((1, n_rows, LANES), lambda e, *_: (e, 0, 0)),
        ],
    )
    idx, rank = pl.pallas_call(
        functools.partial(_compact_kernel, n_rows=n_rows, cap_rows=cap_rows),
        grid_spec=grid_spec,
        out_shape=(
            jax.ShapeDtypeStruct((n_exp, cap_rows, LANES), I32),
            jax.ShapeDtypeStruct((n_exp, n_rows, LANES), I32),
        ),
        compiler_params=pltpu.CompilerParams(dimension_semantics=("parallel",)),
        name="route_compact",
    )(thr[:, 0], need[:, 0], aff_t.reshape(n_exp, n_rows, LANES))
    return idx.reshape(n_exp, cap), rank.reshape(n_exp, n)


D_MODEL = 1024
CONV_W = 512
N_HEADS = 4
V_DIM = 128
QK_DIM = 64
QK_W = N_HEADS * QK_DIM
V_W = N_HEADS * V_DIM
CHUNK = 128
EXPERT_FF = 1024
EPS = 1e-6
C_GB, C_GC, C_U, C_Q, C_K, C_V, C_O, C_G = 0, 512, 1024, 1536, 1792, 2048, 2560, 3072
D_IN = 3088
D_IN_PAD = 3200
TOK_EXT = D_MODEL + LANES

ROW_TILE = 512
INPROJ_TILE = 1024
MLSTM_BLOCK = 256
SLOT_TILE = 512
COMBINE_TILE = 256
VMEM_LIMIT = 56 * 1024 * 1024


def _cparams(*sem):
    return pltpu.CompilerParams(dimension_semantics=sem, vmem_limit_bytes=VMEM_LIMIT)


def _rms(x, g):
    return x * lax.rsqrt(jnp.mean(x * x, axis=-1, keepdims=True) + EPS) * g


GATE_GROUP = 8
GATE_ROWS = 6 * GATE_GROUP
G_A, G_PM, G_B = 0, 1, 2


def _chunk_scan(x, op, identity, reverse):
    width = x.shape[1]
    pos = lax.broadcasted_iota(I32, x.shape, 1) & (CHUNK - 1)
    for b in range(7):
        s = 1 << b
        if reverse:
            shifted = jnp.where(pos < CHUNK - s, pltpu.roll(x, width - s, axis=1), identity)
        else:
            shifted = jnp.where(pos >= s, pltpu.roll(x, s, axis=1), identity)
        x = op(x, shifted)
    return x


def _inproj_kernel(x_ref, g_ref, w_ref, bias_ref, gb_ref, gcu_ref, q_ref, k_ref, v_ref, os_ref, grow_ref):
    hn = _rms(x_ref[...], g_ref[...]).astype(BF16)

    def seg(a, b):
        return jnp.dot(hn, w_ref[:, a:b], preferred_element_type=F32)

    gates = seg(C_G, D_IN_PAD) + bias_ref[...]
    lane = lax.broadcasted_iota(I32, gates.shape, 1)
    log_sig = jnp.minimum(gates, 0.0) - jnp.log1p(jnp.exp(-jnp.abs(gates)))
    g16 = jnp.where((lane >> 2) & 1 == 1, log_sig, gates).T[:4 * N_HEADS]
    fwd = lax.broadcasted_iota(I32, g16.shape, 0) < 2 * N_HEADS
    cs = jnp.where(fwd, _chunk_scan(g16, jnp.add, 0.0, False), _chunk_scan(g16, jnp.add, 0.0, True))
    cs = pltpu.roll(cs, 3 * N_HEADS, axis=0)
    a = g16 - cs
    neg_inf = jnp.float32(-jnp.inf)
    pm = jnp.where(fwd, _chunk_scan(a, jnp.maximum, neg_inf, False), _chunk_scan(a, jnp.maximum, neg_inf, True))
    for d in range(2):
        grp = slice(d * GATE_GROUP, (d + 1) * GATE_GROUP)
        for g, val in ((G_A, a), (G_PM, pm), (G_B, cs)):
            r0 = (3 * d + g) * GATE_GROUP
            grow_ref[r0:r0 + GATE_GROUP, :] = val[grp]

    gb_ref[...] = seg(C_GB, C_GC).astype(BF16)
    gcu_ref[...] = (seg(C_GC, C_U) * seg(C_U, C_Q)).astype(BF16)
    q_ref[...] = (seg(C_Q, C_K) * (QK_DIM ** -0.5)).astype(BF16)
    k_ref[...] = seg(C_K, C_V).astype(BF16)
    v_ref[...] = seg(C_V, C_O).astype(BF16)
    os_ref[...] = jax.nn.sigmoid(seg(C_O, C_G)).astype(BF16)


def inproj(x, g1, w_in_p, bias_p):
    n = x.shape[0]
    tm = INPROJ_TILE
    row = lambda w: pl.BlockSpec((tm, w), lambda i: (i, 0))
    full = lambda a: pl.BlockSpec(a.shape, lambda i: (0,) * a.ndim)
    return pl.pallas_call(
        _inproj_kernel,
        grid=(n // tm,),
        in_specs=[row(D_MODEL), full(g1), full(w_in_p), full(bias_p)],
        out_specs=[row(CONV_W), row(CONV_W), row(QK_W), row(QK_W), row(V_W), row(V_W),
                   pl.BlockSpec((GATE_ROWS, tm), lambda i: (0, i))],
        out_shape=[
            jax.ShapeDtypeStruct((n, CONV_W), BF16), jax.ShapeDtypeStruct((n, CONV_W), BF16),
            jax.ShapeDtypeStruct((n, QK_W), BF16), jax.ShapeDtypeStruct((n, QK_W), BF16),
            jax.ShapeDtypeStruct((n, V_W), BF16), jax.ShapeDtypeStruct((n, V_W), BF16),
            jax.ShapeDtypeStruct((GATE_ROWS, n), F32),
        ],
        compiler_params=_cparams("parallel"),
        name="mixer_inproj",
    )(x, g1, w_in_p, bias_p)


def _mlstm_kernel(qf_ref, kf_ref, vf_ref, gf_ref, qb_ref, kb_ref, vb_ref, gb_ref, hf_ref, hb_ref,
                  cf_ref, mf_ref, cb_ref, mb_ref, *, seq_len):
    j = pl.program_id(0)

    @pl.when(((j * MLSTM_BLOCK) % seq_len) == 0)
    def _():
        for ref in (cf_ref, mf_ref, cb_ref, mb_ref):
            ref[...] = jnp.zeros_like(ref)

    terms_f = _mlstm_step_terms(gf_ref, mf_ref, reverse=False)
    terms_b = _mlstm_step_terms(gb_ref, mb_ref, reverse=True)
    _mlstm_block(qf_ref, kf_ref, vf_ref, hf_ref, cf_ref, terms_f, reverse=False)
    _mlstm_block(qb_ref, kb_ref, vb_ref, hb_ref, cb_ref, terms_b, reverse=True)


def _chunk_order(reverse):
    n_chunks = MLSTM_BLOCK // CHUNK
    return list(range(n_chunks - 1, -1, -1) if reverse else range(n_chunks))


def _mlstm_step_terms(grow_ref, m_ref, *, reverse):
    d = 3 if reverse else 0

    def gate_rows(g, cols):
        r0 = (d + g) * GATE_GROUP
        return grow_ref[r0:r0 + N_HEADS, cols]

    order = _chunk_order(reverse)
    end = 0 if reverse else CHUNK - 1
    m_old = m_ref[0:N_HEADS, :]
    m_in = {}
    for c in order:
        last = slice(c * CHUNK + end, c * CHUNK + end + 1)
        m_in[c] = m_old
        m_old = gate_rows(G_B, last) + jnp.maximum(m_old, gate_rows(G_PM, last))
    m_ref[0:N_HEADS, :] = m_old

    terms = {}
    for c in order:
        r0 = c * CHUNK
        rows = slice(r0, r0 + CHUNK)
        m_o = m_in[c]
        a = gate_rows(G_A, rows)
        mm = jnp.maximum(m_o, gate_rows(G_PM, rows))
        mm_last = jnp.maximum(m_o, gate_rows(G_PM, slice(r0 + end, r0 + end + 1)))
        sc = jnp.exp(m_o - mm)
        emt = jnp.exp(-(gate_rows(G_B, rows) + mm))
        w = jnp.exp(a - mm_last)
        decay = jnp.exp(m_o - mm_last)
        stack = jnp.concatenate([mm, sc, emt, jnp.zeros((LANES - 3 * N_HEADS, CHUNK), F32)], axis=0)
        terms[c] = (a, w, decay, stack.T)
    return terms


def _mlstm_block(q_ref, k_ref, v_ref, out_ref, c_ref, terms, *, reverse):
    t_i = lax.broadcasted_iota(I32, (CHUNK, CHUNK), 0)
    s_i = lax.broadcasted_iota(I32, (CHUNK, CHUNK), 1)
    tri = (s_i >= t_i) if reverse else (s_i <= t_i)
    lane = lax.broadcasted_iota(I32, (CHUNK, LANES), 1)
    half_masks = [jnp.where((lane >> 6) == hh, 1.0, 0.0).astype(BF16) for hh in range(2)]
    ones_col = jnp.where(lane == 0, 1.0, 0.0).astype(BF16)
    c_state = [c_ref[h] for h in range(N_HEADS)]
    for c in _chunk_order(reverse):
        rows = slice(c * CHUNK, (c + 1) * CHUNK)
        a, w, decay, cols = terms[c]
        for h in range(N_HEADS):
            pair = slice((h // 2) * LANES, (h // 2 + 1) * LANES)
            hv = slice(h * V_DIM, (h + 1) * V_DIM)
            q2 = q_ref[rows, pair]
            km = k_ref[rows, pair] * half_masks[h % 2]
            vh = v_ref[rows, hv]
            c_old = c_state[h]
            mm_col, sc_col = cols[:, h:h + 1], cols[:, N_HEADS + h:N_HEADS + h + 1]
            emt_col = cols[:, 2 * N_HEADS + h:2 * N_HEADS + h + 1]
            dmat = jnp.where(tri, jnp.exp(a[h:h + 1, :] - mm_col), 0.0)
            s_mat = lax.dot_general(q2, km, (((1,), (1,)), ((), ())), preferred_element_type=F32) * dmat
            qc = jnp.dot(q2, c_old.astype(BF16), preferred_element_type=F32)
            num = jnp.dot(s_mat.astype(BF16), vh, preferred_element_type=F32) + sc_col * qc[:, :V_DIM]
            den = jnp.sum(s_mat, axis=1, keepdims=True) + sc_col * qc[:, V_DIM:V_DIM + 1]
            h_out = num * (1.0 / jnp.maximum(jnp.abs(den), emt_col))
            kmt = (km.astype(F32).T * w[h:h + 1, :]).astype(BF16)
            kv = jnp.dot(kmt, jnp.concatenate([vh, ones_col], axis=1), preferred_element_type=F32)
            c_state[h] = decay[h:h + 1, 0:1] * c_old + kv
            out_ref[rows, hv] = h_out.astype(BF16)
    for h in range(N_HEADS):
        c_ref[h] = c_state[h]


def mlstm(q, k, v, grow, seq_len):
    n = q.shape[0]
    blk = MLSTM_BLOCK
    nb = n // blk
    assert n % seq_len == 0 and seq_len % blk == 0
    fwd = lambda w: pl.BlockSpec((blk, w), lambda j: (j, 0))
    bwd = lambda w: pl.BlockSpec((blk, w), lambda j: (nb - 1 - j, 0))
    state = [pltpu.VMEM((N_HEADS, LANES, 2 * LANES), F32), pltpu.VMEM((8, LANES), F32)]
    return pl.pallas_call(
        functools.partial(_mlstm_kernel, seq_len=seq_len),
        grid=(nb,),
        in_specs=[fwd(QK_W), fwd(QK_W), fwd(V_W), pl.BlockSpec((GATE_ROWS, blk), lambda j: (0, j)),
                  bwd(QK_W), bwd(QK_W), bwd(V_W), pl.BlockSpec((GATE_ROWS, blk), lambda j: (0, nb - 1 - j))],
        out_specs=[fwd(V_W), bwd(V_W)],
        out_shape=[jax.ShapeDtypeStruct((n, V_W), BF16), jax.ShapeDtypeStruct((n, V_W), BF16)],
        scratch_shapes=state + state,
        compiler_params=_cparams("arbitrary"),
        name="mlstm",
    )(q, k, v, grow, q, k, v, grow)


HALO = 16


def _outproj_kernel(gb_ref, gcu_ref, gprev_ref, gnext_ref, hf_ref, hb_ref, os_ref, hng_ref, x_ref, cw_ref, wo_ref,
                    g2_ref, wr_ref, x1e_ref, afft_ref, *, seq_len):
    i = pl.program_id(0)
    tm = ROW_TILE
    first = ((i * tm) % seq_len) == 0
    last = (((i + 1) * tm) % seq_len) == 0
    g = gcu_ref[...].astype(F32)
    prev_row = jnp.where(first, 0.0, gprev_ref[HALO - 1:HALO, :].astype(F32))
    next_row = jnp.where(last, 0.0, gnext_ref[0:1, :].astype(F32))
    rid = lax.broadcasted_iota(I32, g.shape, 0)
    dn = jnp.where(rid == 0, prev_row, pltpu.roll(g, 1, axis=0))
    up = jnp.where(rid == tm - 1, next_row, pltpu.roll(g, tm - 1, axis=0))
    conv = dn * cw_ref[0:1, :] + g * cw_ref[1:2, :] + up * cw_ref[2:3, :]
    co = (gb_ref[...].astype(F32) * conv).astype(BF16)
    y = jnp.dot(co, wo_ref[:CONV_W, :], preferred_element_type=F32)
    heads = []
    for h in range(N_HEADS):
        hv = slice(h * V_DIM, (h + 1) * V_DIM)
        ht = _rms(hf_ref[:, hv].astype(F32) + hb_ref[:, hv].astype(F32), hng_ref[:, hv])
        heads.append((os_ref[:, hv].astype(F32) * ht).astype(BF16))
    mo = jnp.concatenate(heads, axis=1)
    y = y + jnp.dot(mo, wo_ref[CONV_W:, :], preferred_element_type=F32)
    x1 = x_ref[...] + y
    tokens = _rms(x1, g2_ref[...])
    t_hi = tokens.astype(BF16)
    t_lo = (tokens - t_hi.astype(F32)).astype(BF16)
    p_hi = jnp.dot(t_hi, wr_ref[...], preferred_element_type=F32)
    logits = p_hi[:, :LANES] + p_hi[:, LANES:] + jnp.dot(t_lo, wr_ref[:, :LANES], preferred_element_type=F32)
    lane = lax.broadcasted_iota(I32, logits.shape, 1)
    logits = jnp.where(lane < N_EXPERTS, logits, -jnp.inf)
    ex = jnp.exp(logits - jnp.max(logits, axis=-1, keepdims=True))
    aff = ex / jnp.sum(ex, axis=-1, keepdims=True)
    x1e_ref[:, :D_MODEL] = x1
    row_id = (i * tm + lax.broadcasted_iota(I32, aff.shape, 0)).astype(F32)
    x1e_ref[:, D_MODEL:] = jnp.where(lane == TOKEN_ID_LANE, row_id, aff)
    afft_ref[...] = aff.T[:N_EXPERTS]


def outproj(gb, gcu, hf, hb, osig, hng, x, cw_p, w_out_b, g2, wr_p, seq_len):
    n = x.shape[0]
    tm = ROW_TILE
    halos = tm // HALO
    n_halo = n // HALO
    row = lambda w: pl.BlockSpec((tm, w), lambda i: (i, 0))
    full = lambda a: pl.BlockSpec(a.shape, lambda i: (0,) * a.ndim)
    prev = pl.BlockSpec((HALO, CONV_W), lambda i: (jnp.maximum(i * halos - 1, 0), 0))
    nxt = pl.BlockSpec((HALO, CONV_W), lambda i: (jnp.minimum((i + 1) * halos, n_halo - 1), 0))
    return pl.pallas_call(
        functools.partial(_outproj_kernel, seq_len=seq_len),
        grid=(n // tm,),
        in_specs=[row(CONV_W), row(CONV_W), prev, nxt, row(V_W), row(V_W), row(V_W), full(hng), row(D_MODEL),
                  full(cw_p), full(w_out_b), full(g2), full(wr_p)],
        out_specs=[row(TOK_EXT), pl.BlockSpec((N_EXPERTS, tm), lambda i: (0, i))],
        out_shape=[jax.ShapeDtypeStruct((n, TOK_EXT), F32), jax.ShapeDtypeStruct((N_EXPERTS, n), F32)],
        compiler_params=_cparams("parallel"),
        name="mixer_outproj_router",
    )(gb, gcu, gcu, gcu, hf, hb, osig, hng, x, cw_p, w_out_b, g2, wr_p)


def _ffn_kernel(idx_ref, tok_hbm, g2_ref, wg_ref, wu_ref, wd_ref, ye_ref, xg_ref, wgb_ref, wub_ref, wdb_ref, sem):
    e = pl.program_id(0)
    s = pl.program_id(1)
    n_steps = pl.num_programs(1)
    ts = SLOT_TILE
    t = e * n_steps + s
    last = N_EXPERTS * n_steps - 1

    def row_copy(tile, buf, i):
        tok = idx_ref[tile * ts + i]
        return pltpu.make_async_copy(tok_hbm.at[pl.ds(tok, 1)], xg_ref.at[buf, pl.ds(i, 1)], sem.at[buf])

    def wait_rows(buf):
        pltpu.make_async_copy(tok_hbm.at[pl.ds(0, ts)], xg_ref.at[buf], sem.at[buf]).wait()

    @pl.when(t == 0)
    def _():
        def issue(i, carry):
            row_copy(0, 0, i).start()
            return carry

        lax.fori_loop(0, ts, issue, 0, unroll=8)

    @pl.when(s == 0)
    def _():
        wgb_ref[...] = wg_ref[0, 0].astype(BF16)
        wub_ref[...] = wu_ref[0, 0].astype(BF16)
        wdb_ref[...] = wd_ref[0, 0].astype(BF16)

    def half(buf, next_tile, rows):
        wait_rows(buf)
        x = xg_ref[buf]
        xb = _rms(x[:, :D_MODEL], g2_ref[...]).astype(BF16)
        ext = x[:, D_MODEL:]
        lane = lax.broadcasted_iota(I32, ext.shape, 1)
        gate = jnp.sum(jnp.where(lane == e, ext, 0.0), axis=1, keepdims=True)
        for i in range(ts):
            row_copy(next_tile, 1 - buf, i).start(priority=i % 2)
        hg = jnp.dot(xb, wgb_ref[...], preferred_element_type=F32)
        hu = jnp.dot(xb, wub_ref[...], preferred_element_type=F32)
        hid = (hg * jax.nn.sigmoid(hg) * hu).astype(BF16)
        ye_ref[rows, :D_MODEL] = jnp.dot(hid, wdb_ref[...], preferred_element_type=F32) * gate
        ye_ref[rows, D_MODEL:] = ext

    half(0, 2 * t + 1, slice(0, ts))
    half(1, jnp.minimum(2 * t + 2, 2 * last), slice(ts, 2 * ts))

    @pl.when(t == last)
    def _():
        wait_rows(0)


def expert_ffn(idx_flat, x1e, g2, wg, wu, wd, cap, layer):
    assert cap % (2 * SLOT_TILE) == 0
    n_steps = cap // (2 * SLOT_TILE)
    wspec = lambda: pl.BlockSpec((1, 1, D_MODEL, EXPERT_FF), lambda e, s, *_: (layer, e, 0, 0))
    grid_spec = pltpu.PrefetchScalarGridSpec(
        num_scalar_prefetch=1,
        grid=(N_EXPERTS, n_steps),
        in_specs=[pl.BlockSpec(memory_space=pl.ANY), pl.BlockSpec((1, D_MODEL), lambda e, s, *_: (0, 0)), wspec(), wspec(),
                  pl.BlockSpec((1, 1, EXPERT_FF, D_MODEL), lambda e, s, *_: (layer, e, 0, 0))],
        out_specs=pl.BlockSpec((2 * SLOT_TILE, TOK_EXT), lambda e, s, *_: (e * n_steps + s, 0)),
        scratch_shapes=[pltpu.VMEM((2, SLOT_TILE, TOK_EXT), F32),
                        pltpu.VMEM((D_MODEL, EXPERT_FF), BF16), pltpu.VMEM((D_MODEL, EXPERT_FF), BF16),
                        pltpu.VMEM((EXPERT_FF, D_MODEL), BF16), pltpu.SemaphoreType.DMA((2,))],
    )
    return pl.pallas_call(
        _ffn_kernel,
        grid_spec=grid_spec,
        out_shape=jax.ShapeDtypeStruct((N_EXPERTS * cap, TOK_EXT), F32),
        compiler_params=_cparams("arbitrary", "arbitrary"),
        name="expert_ffn",
    )(idx_flat, x1e, g2, wg, wu, wd)


WIN = 8
STACK_TILE = 256


def _stack_rows(tc):
    rows = N_EXPERTS * (tc + 2 * (WIN - 1))
    return -(-rows // STACK_TILE) * STACK_TILE


def _combine_kernel(off_ref, x1_ref, ye_hbm, *rest, cap, n_blocks, final):
    if final:
        fg_ref, out_ref, ys_ref, sem = rest
    else:
        out_ref, ys_ref, sem = rest
    j = pl.program_id(0)
    tc = COMBINE_TILE
    buf = j % 2

    def windows(tile, e):
        lo = off_ref[e * (n_blocks + 1) + tile]
        hi = off_ref[e * (n_blocks + 1) + tile + 1]
        start = (lo >> 3) << 3
        return start, jnp.where(hi > lo, (hi - start + (WIN - 1)) >> 3, 0)

    def stacked_rows(tile):
        total = jnp.int32(0)
        for e in range(N_EXPERTS):
            total = total + windows(tile, e)[1] * WIN
        return total

    def fetch(tile, buf_):
        base = jnp.int32(0)
        for e in range(N_EXPERTS):
            start, n_win = windows(tile, e)

            def issue(w, carry, e=e, start=start, base=base):
                src = pl.multiple_of(e * cap + start + w * WIN, WIN)
                dst = pl.multiple_of(base + w * WIN, WIN)
                pltpu.make_async_copy(ye_hbm.at[pl.ds(src, WIN)], ys_ref.at[buf_, pl.ds(dst, WIN)], sem.at[buf_]).start()
                return carry

            lax.fori_loop(0, n_win, issue, 0)
            base = base + n_win * WIN

    @pl.when(j == 0)
    def _():
        ys_ref[...] = jnp.zeros_like(ys_ref)
        fetch(0, 0)

    base = stacked_rows(j)
    n_total = base >> 3
    for bit in range((_stack_rows(tc) // WIN).bit_length()):
        @pl.when((n_total & (1 << bit)) != 0)
        def _(bit=bit):
            rows = WIN << bit
            pltpu.make_async_copy(ye_hbm.at[pl.ds(0, rows)], ye_hbm.at[pl.ds(0, rows)], sem.at[buf]).wait()

    @pl.when(j + 1 < n_blocks)
    def _():
        fetch(j + 1, 1 - buf)

    tok0 = (j * tc).astype(F32)
    lane_t = lax.broadcasted_iota(I32, (STACK_TILE, tc), 1).astype(F32)
    row_i = lax.broadcasted_iota(I32, (STACK_TILE, 1), 0)

    def accumulate(kt, acc):
        r0 = pl.multiple_of(kt * STACK_TILE, STACK_TILE)
        rows = ys_ref[buf, pl.ds(r0, STACK_TILE), :]
        tok_local = rows[:, D_MODEL + TOKEN_ID_LANE:D_MODEL + TOKEN_ID_LANE + 1] - tok0
        tok_local = jnp.where(r0 + row_i < base, tok_local, -1.0)
        onehot = jnp.where(tok_local == lane_t, 1.0, 0.0).astype(BF16)
        out_ref[...] += lax.dot_general(onehot, rows[:, :D_MODEL].astype(BF16), (((0,), (0,)), ((), ())),
                                        preferred_element_type=F32)
        return acc

    n_kt = (base + (STACK_TILE - 1)) // STACK_TILE
    out_ref[...] = x1_ref[...]
    lax.fori_loop(0, n_kt, accumulate, 0)
    if final:
        out_ref[...] = _rms(out_ref[...], fg_ref[...])


def combine(off_flat, x1e, ye, cap, final_g=None):
    n = x1e.shape[0]
    tc = COMBINE_TILE
    nb = n // tc
    final = final_g is not None
    in_specs = [pl.BlockSpec((tc, D_MODEL), lambda j, *_: (j, 0)),
                pl.BlockSpec(memory_space=pl.ANY)]
    args = [x1e, ye]
    if final:
        in_specs.append(pl.BlockSpec((1, D_MODEL), lambda j, *_: (0, 0)))
        args.append(final_g)
    grid_spec = pltpu.PrefetchScalarGridSpec(
        num_scalar_prefetch=1,
        grid=(nb,),
        in_specs=in_specs,
        out_specs=pl.BlockSpec((tc, D_MODEL), lambda j, *_: (j, 0)),
        scratch_shapes=[pltpu.VMEM((2, _stack_rows(tc), TOK_EXT), F32), pltpu.SemaphoreType.DMA((2,))],
    )
    return pl.pallas_call(
        functools.partial(_combine_kernel, cap=cap, n_blocks=nb, final=final),
        grid_spec=grid_spec,
        out_shape=jax.ShapeDtypeStruct((n, D_MODEL), F32),
        compiler_params=_cparams("arbitrary"),
        name="moe_combine",
    )(off_flat, *args)


def moe(x1e, aff_t, g2, wg, wu, wd, layer, final_g=None):
    n = x1e.shape[0]
    cap = n // CAPACITY_DIV
    idx, rank = route(aff_t)
    off = jnp.concatenate([rank[:, ::COMBINE_TILE], jnp.full((N_EXPERTS, 1), cap, I32)], axis=1)
    ye = expert_ffn(idx.reshape(-1), x1e, g2, wg, wu, wd, cap, layer)
    return combine(off.reshape(-1), x1e, ye, cap, final_g)


def kernel(x_prompt, x_sample, norm1_g, w_in, conv_w, gate_bias, head_norm_g, w_out, norm2_g, w_router, w_gate, w_up, w_down, final_g):
    depth = w_in.shape[0]
    w_in_p = jnp.pad(w_in, ((0, 0), (0, 0), (0, D_IN_PAD - D_IN))).astype(BF16)
    bias_p = jnp.pad(gate_bias.reshape(depth, 1, 4 * N_HEADS), ((0, 0), (0, 0), (0, LANES - 4 * N_HEADS)))
    cw_p = jnp.pad(conv_w, ((0, 0), (0, 8 - conv_w.shape[1]), (0, 0)))
    wr_f = jnp.pad(w_router, ((0, 0), (0, 0), (0, LANES - N_EXPERTS)))
    wr_hi = wr_f.astype(BF16)
    wr_p = jnp.concatenate([wr_hi, (wr_f - wr_hi.astype(F32)).astype(BF16)], axis=-1)
    w_out_b = w_out.astype(BF16)
    fg = final_g.reshape(1, D_MODEL)

    outs = []
    for x in (x_prompt, x_sample):
        bsz, seq, _ = x.shape
        xf = x.reshape(bsz * seq, D_MODEL)
        for l in range(depth):
            gb, gcu, q, k, v, osig, grow = inproj(xf, norm1_g[l].reshape(1, -1), w_in_p[l], bias_p[l])
            hf, hb = mlstm(q, k, v, grow, seq)
            g2 = norm2_g[l].reshape(1, -1)
            x1e, aff_t = outproj(gb, gcu, hf, hb, osig, head_norm_g[l].reshape(1, -1), xf, cw_p[l], w_out_b[l],
                                 g2, wr_p[l], seq)
            xf = moe(x1e, aff_t, g2, w_gate, w_up, w_down, l, fg if l == depth - 1 else None)
        outs.append(xf.reshape(bsz, seq, D_MODEL))
    return tuple(outs)
```

```python
import functools

import jax
import jax.numpy as jnp
from jax import lax
from jax.experimental import pallas as pl
from jax.experimental.pallas import tpu as pltpu

F32 = jnp.float32
BF16 = jnp.bfloat16
I32 = jnp.int32

LANES = 128
N_EXPERTS = 16
CAPACITY_DIV = 8
DISP_BITS = 16
VALID_BIT = 24
TOKEN_ID_LANE = N_EXPERTS


def _threshold_kernel(aff_ref, thr_ref, need_ref, *, cap):
    bits = pltpu.bitcast(aff_ref[...], I32)
    cap_f = jnp.float32(cap)

    def count_ge(cand):
        return jnp.sum(jnp.where(bits >= cand, 1.0, 0.0), axis=1, keepdims=True)

    def body(i, thr):
        cand = thr | jnp.left_shift(jnp.int32(1), 30 - i)
        return jnp.where(count_ge(cand) >= cap_f, cand, thr)

    thr = lax.fori_loop(0, 31, body, jnp.zeros((N_EXPERTS, 1), I32))
    n_gt = jnp.sum(jnp.where(bits > thr, 1.0, 0.0), axis=1, keepdims=True)
    need = (cap_f - n_gt).astype(I32)
    thr_ref[...] = jnp.broadcast_to(thr, thr_ref.shape)
    need_ref[...] = jnp.broadcast_to(need, need_ref.shape)


def _lane_inclusive_scan(x, lane):
    for b in range(7):
        s = 1 << b
        x = x + jnp.where(lane >= s, pltpu.roll(x, s, axis=1), 0.0)
    return x


def _row_exclusive_scan(t, row, n_rows):
    inc = t
    s = 1
    while s < n_rows:
        inc = inc + jnp.where(row >= s, pltpu.roll(inc, s, axis=0), 0.0)
        s *= 2
    return inc - t


def _token_exclusive_scan(x, lane, row, n_rows):
    inc = _lane_inclusive_scan(x, lane)
    tot = jnp.broadcast_to(inc[:, LANES - 1:LANES], x.shape)
    return inc - x + _row_exclusive_scan(tot, row, n_rows)


def _compact_kernel(thr_ref, need_ref, aff_ref, idx_ref, rank_ref, *, n_rows, cap_rows):
    e = pl.program_id(0)
    shape = (n_rows, LANES)
    lane = lax.broadcasted_iota(I32, shape, 1)
    row = lax.broadcasted_iota(I32, shape, 0)
    bits = pltpu.bitcast(aff_ref[0], I32)
    thr = thr_ref[e]
    need = need_ref[e].astype(F32)
    eq = bits == thr
    pre_eq = _token_exclusive_scan(jnp.where(eq, 1.0, 0.0), lane, row, n_rows)
    sel = (bits > thr) | (eq & (pre_eq < need))
    sel_f = jnp.where(sel, 1.0, 0.0)
    rank = _token_exclusive_scan(sel_f, lane, row, n_rows).astype(I32)
    rank_ref[0] = rank

    pos = row * LANES + lane
    disp = pos - rank
    v = jnp.where(sel, disp | (1 << VALID_BIT), 0)
    n_bits = (n_rows * LANES - 1).bit_length()
    for b in range(n_bits):
        if b < 7:
            s = 1 << b
            r1 = pltpu.roll(v, LANES - s, axis=1)
            r2 = pltpu.roll(r1, n_rows - 1, axis=0)
            moved = jnp.where(lane < LANES - s, r1, r2)
        else:
            sr = 1 << (b - 7)
            moved = pltpu.roll(v, n_rows - sr, axis=0)
        take = ((moved >> VALID_BIT) & 1 == 1) & ((moved >> b) & 1 == 1)
        stay = ((v >> VALID_BIT) & 1 == 1) & ((v >> b) & 1 == 0)
        v = jnp.where(take, moved, jnp.where(stay, v, 0))
    idx_ref[0] = (pos + (v & ((1 << DISP_BITS) - 1)))[:cap_rows]


def route(aff_t):
    n_exp, n = aff_t.shape
    assert n <= (1 << DISP_BITS)
    cap = n // CAPACITY_DIV
    n_rows = n // LANES
    cap_rows = cap // LANES
    thr, need = pl.pallas_call(
        functools.partial(_threshold_kernel, cap=cap),
        out_shape=(jax.ShapeDtypeStruct((n_exp, LANES), I32), jax.ShapeDtypeStruct((n_exp, LANES), I32)),
        name="route_threshold",
    )(aff_t)
    grid_spec = pltpu.PrefetchScalarGridSpec(
        num_scalar_prefetch=2,
        grid=(n_exp,),
        in_specs=[pl.BlockSpec((1, n_rows, LANES), lambda e, *_: (e, 0, 0))],
        out_specs=[
            pl.BlockSpec((1, cap_rows, LANES), lambda e, *_: (e, 0, 0)),
            pl.BlockSpec((1, n_rows, LANES), lambda e, *_: (e, 0, 0)),
        ],
    )
    idx, rank = pl.pallas_call(
        functools.partial(_compact_kernel, n_rows=n_rows, cap_rows=cap_rows),
        grid_spec=grid_spec,
        out_shape=(
            jax.ShapeDtypeStruct((n_exp, cap_rows, LANES), I32),
            jax.ShapeDtypeStruct((n_exp, n_rows, LANES), I32),
        ),
        compiler_params=pltpu.CompilerParams(dimension_semantics=("parallel",)),
        name="route_compact",
    )(thr[:, 0], need[:, 0], aff_t.reshape(n_exp, n_rows, LANES))
    return idx.reshape(n_exp, cap), rank.reshape(n_exp, n)


D_MODEL = 1024
CONV_W = 512
N_HEADS = 4
V_DIM = 128
QK_DIM = 64
QK_W = N_HEADS * QK_DIM
V_W = N_HEADS * V_DIM
CHUNK = 128
EXPERT_FF = 1024
EPS = 1e-6
C_GB, C_GC, C_U, C_Q, C_K, C_V, C_O, C_G = 0, 512, 1024, 1536, 1792, 2048, 2560, 3072
D_IN = 3088
D_IN_PAD = 3200
TOK_EXT = D_MODEL + LANES

ROW_TILE = 512
INPROJ_TILE = 1024
MLSTM_BLOCK = 256
SLOT_TILE = 512
COMBINE_TILE = 256
VMEM_LIMIT = 56 * 1024 * 1024


def _cparams(*sem):
    return pltpu.CompilerParams(dimension_semantics=sem, vmem_limit_bytes=VMEM_LIMIT)


def _rms(x, g):
    return x * lax.rsqrt(jnp.mean(x * x, axis=-1, keepdims=True) + EPS) * g


GATE_GROUP = 8
GATE_ROWS = 6 * GATE_GROUP
G_A, G_PM, G_B = 0, 1, 2


def _chunk_scan(x, op, identity, reverse):
    width = x.shape[1]
    pos = lax.broadcasted_iota(I32, x.shape, 1) & (CHUNK - 1)
    for b in range(7):
        s = 1 << b
        if reverse:
            shifted = jnp.where(pos < CHUNK - s, pltpu.roll(x, width - s, axis=1), identity)
        else:
            shifted = jnp.where(pos >= s, pltpu.roll(x, s, axis=1), identity)
        x = op(x, shifted)
    return x


def _inproj_kernel(x_ref, g_ref, w_ref, bias_ref, gb_ref, gcu_ref, q_ref, k_ref, v_ref, os_ref, grow_ref):
    hn = _rms(x_ref[...], g_ref[...]).astype(BF16)

    def seg(a, b):
        return jnp.dot(hn, w_ref[:, a:b], preferred_element_type=F32)

    gates = seg(C_G, D_IN_PAD) + bias_ref[...]
    lane = lax.broadcasted_iota(I32, gates.shape, 1)
    log_sig = jnp.minimum(gates, 0.0) - jnp.log1p(jnp.exp(-jnp.abs(gates)))
    g16 = jnp.where((lane // N_HEADS) % 2 == 1, log_sig, gates).T[:4 * N_HEADS]
    fwd = lax.broadcasted_iota(I32, g16.shape, 0) < 2 * N_HEADS
    cs = jnp.where(fwd, _chunk_scan(g16, jnp.add, 0.0, False), _chunk_scan(g16, jnp.add, 0.0, True))
    cs = pltpu.roll(cs, 3 * N_HEADS, axis=0)
    a = g16 - cs
    neg_inf = jnp.float32(-jnp.inf)
    pm = jnp.where(fwd, _chunk_scan(a, jnp.maximum, neg_inf, False), _chunk_scan(a, jnp.maximum, neg_inf, True))
    for d in range(2):
        grp = slice(d * GATE_GROUP, (d + 1) * GATE_GROUP)
        for g, val in ((G_A, a), (G_PM, pm), (G_B, cs)):
            r0 = (3 * d + g) * GATE_GROUP
            grow_ref[r0:r0 + GATE_GROUP, :] = val[grp]

    gb_ref[...] = seg(C_GB, C_GC).astype(BF16)
    gcu_ref[...] = (seg(C_GC, C_U) * seg(C_U, C_Q)).astype(BF16)
    q_ref[...] = (seg(C_Q, C_K) * (QK_DIM ** -0.5)).astype(BF16)
    k_ref[...] = seg(C_K, C_V).astype(BF16)
    v_ref[...] = seg(C_V, C_O).astype(BF16)
    os_ref[...] = jax.nn.sigmoid(seg(C_O, C_G)).astype(BF16)


def inproj(x, g1, w_in_p, bias_p):
    n = x.shape[0]
    tm = INPROJ_TILE
    row = lambda w: pl.BlockSpec((tm, w), lambda i: (i, 0))
    full = lambda a: pl.BlockSpec(a.shape, lambda i: (0,) * a.ndim)
    return pl.pallas_call(
        _inproj_kernel,
        grid=(n // tm,),
        in_specs=[row(D_MODEL), full(g1), full(w_in_p), full(bias_p)],
        out_specs=[row(CONV_W), row(CONV_W), row(QK_W), row(QK_W), row(V_W), row(V_W),
                   pl.BlockSpec((GATE_ROWS, tm), lambda i: (0, i))],
        out_shape=[
            jax.ShapeDtypeStruct((n, CONV_W), BF16), jax.ShapeDtypeStruct((n, CONV_W), BF16),
            jax.ShapeDtypeStruct((n, QK_W), BF16), jax.ShapeDtypeStruct((n, QK_W), BF16),
            jax.ShapeDtypeStruct((n, V_W), BF16), jax.ShapeDtypeStruct((n, V_W), BF16),
            jax.ShapeDtypeStruct((GATE_ROWS, n), F32),
        ],
        compiler_params=_cparams("parallel"),
        name="mixer_inproj",
    )(x, g1, w_in_p, bias_p)


def _mlstm_kernel(qf_ref, kf_ref, vf_ref, gf_ref, qb_ref, kb_ref, vb_ref, gb_ref, hf_ref, hb_ref,
                  cf_ref, mf_ref, cb_ref, mb_ref, *, seq_len):
    j = pl.program_id(0)

    @pl.when(((j * MLSTM_BLOCK) % seq_len) == 0)
    def _():
        for ref in (cf_ref, mf_ref, cb_ref, mb_ref):
            ref[...] = jnp.zeros_like(ref)

    terms_f = _mlstm_step_terms(gf_ref, mf_ref, reverse=False)
    terms_b = _mlstm_step_terms(gb_ref, mb_ref, reverse=True)
    _mlstm_block(qf_ref, kf_ref, vf_ref, hf_ref, cf_ref, terms_f, reverse=False)
    _mlstm_block(qb_ref, kb_ref, vb_ref, hb_ref, cb_ref, terms_b, reverse=True)


def _chunk_order(reverse):
    n_chunks = MLSTM_BLOCK // CHUNK
    return list(range(n_chunks - 1, -1, -1) if reverse else range(n_chunks))


def _mlstm_step_terms(grow_ref, m_ref, *, reverse):
    d = 3 if reverse else 0

    def gate_rows(g, cols):
        r0 = (d + g) * GATE_GROUP
        return grow_ref[r0:r0 + N_HEADS, cols]

    order = _chunk_order(reverse)
    end = 0 if reverse else CHUNK - 1
    m_old = m_ref[0:N_HEADS, :]
    m_in = {}
    for c in order:
        last = slice(c * CHUNK + end, c * CHUNK + end + 1)
        m_in[c] = m_old
        m_old = gate_rows(G_B, last) + jnp.maximum(m_old, gate_rows(G_PM, last))
    m_ref[0:N_HEADS, :] = m_old

    terms = {}
    for c in order:
        r0 = c * CHUNK
        rows = slice(r0, r0 + CHUNK)
        m_o = m_in[c]
        a = gate_rows(G_A, rows)
        mm = jnp.maximum(m_o, gate_rows(G_PM, rows))
        mm_last = jnp.maximum(m_o, gate_rows(G_PM, slice(r0 + end, r0 + end + 1)))
        sc = jnp.exp(m_o - mm)
        emt = jnp.exp(-(gate_rows(G_B, rows) + mm))
        w = jnp.exp(a - mm_last)
        decay = jnp.exp(m_o - mm_last)
        stack = jnp.concatenate([mm, sc, emt, jnp.zeros((LANES - 3 * N_HEADS, CHUNK), F32)], axis=0)
        terms[c] = (a, w, decay, stack.T)
    return terms


def _mlstm_block(q_ref, k_ref, v_ref, out_ref, c_ref, terms, *, reverse):
    t_i = lax.broadcasted_iota(I32, (CHUNK, CHUNK), 0)
    s_i = lax.broadcasted_iota(I32, (CHUNK, CHUNK), 1)
    tri = (s_i >= t_i) if reverse else (s_i <= t_i)
    lane = lax.broadcasted_iota(I32, (CHUNK, LANES), 1)
    half_masks = [jnp.where(lane // QK_DIM == hh, 1.0, 0.0).astype(BF16) for hh in range(2)]
    ones_col = jnp.where(lane == 0, 1.0, 0.0).astype(BF16)
    kmts = {}
    for c in _chunk_order(reverse):
        rows = slice(c * CHUNK, (c + 1) * CHUNK)
        w = terms[c][1]
        for h in range(N_HEADS):
            pair = slice((h // 2) * LANES, (h // 2 + 1) * LANES)
            km = k_ref[rows, pair] * half_masks[h % 2]
            kmts[c, h] = (km.astype(F32).T * w[h:h + 1, :]).astype(BF16)

    c_state = [c_ref[h] for h in range(N_HEADS)]
    for c in _chunk_order(reverse):
        rows = slice(c * CHUNK, (c + 1) * CHUNK)
        a, _, decay, cols = terms[c]
        for h in range(N_HEADS):
            pair = slice((h // 2) * LANES, (h // 2 + 1) * LANES)
            hv = slice(h * V_DIM, (h + 1) * V_DIM)
            q2 = q_ref[rows, pair]
            km = k_ref[rows, pair] * half_masks[h % 2]
            vh = v_ref[rows, hv]
            c_old = c_state[h]
            mm_col, sc_col = cols[:, h:h + 1], cols[:, N_HEADS + h:N_HEADS + h + 1]
            emt_col = cols[:, 2 * N_HEADS + h:2 * N_HEADS + h + 1]
            dmat = jnp.where(tri, jnp.exp(a[h:h + 1, :] - mm_col), 0.0)
            s_mat = lax.dot_general(q2, km, (((1,), (1,)), ((), ())), preferred_element_type=F32) * dmat
            qc = jnp.dot(q2, c_old.astype(BF16), preferred_element_type=F32)
            num = jnp.dot(s_mat.astype(BF16), vh, preferred_element_type=F32) + sc_col * qc[:, :V_DIM]
            den = jnp.sum(s_mat, axis=1, keepdims=True) + sc_col * qc[:, V_DIM:V_DIM + 1]
            h_out = num * (1.0 / jnp.maximum(jnp.abs(den), emt_col))
            kv = jnp.dot(kmts[c, h], jnp.concatenate([vh, ones_col], axis=1), preferred_element_type=F32)
            c_state[h] = decay[h:h + 1, 0:1] * c_old + kv
            out_ref[rows, hv] = h_out.astype(BF16)
    for h in range(N_HEADS):
        c_ref[h] = c_state[h]


def mlstm(q, k, v, grow, seq_len):
    n = q.shape[0]
    blk = MLSTM_BLOCK
    nb = n // blk
    assert n % seq_len == 0 and seq_len % blk == 0
    fwd = lambda w: pl.BlockSpec((blk, w), lambda j: (j, 0))
    bwd = lambda w: pl.BlockSpec((blk, w), lambda j: (nb - 1 - j, 0))
    state = [pltpu.VMEM((N_HEADS, LANES, 2 * LANES), F32), pltpu.VMEM((8, LANES), F32)]
    return pl.pallas_call(
        functools.partial(_mlstm_kernel, seq_len=seq_len),
        grid=(nb,),
        in_specs=[fwd(QK_W), fwd(QK_W), fwd(V_W), pl.BlockSpec((GATE_ROWS, blk), lambda j: (0, j)),
                  bwd(QK_W), bwd(QK_W), bwd(V_W), pl.BlockSpec((GATE_ROWS, blk), lambda j: (0, nb - 1 - j))],
        out_specs=[fwd(V_W), bwd(V_W)],
        out_shape=[jax.ShapeDtypeStruct((n, V_W), BF16), jax.ShapeDtypeStruct((n, V_W), BF16)],
        scratch_shapes=state + state,
        compiler_params=_cparams("arbitrary"),
        name="mlstm",
    )(q, k, v, grow, q, k, v, grow)


HALO = 16


def _outproj_kernel(gb_ref, gcu_ref, gprev_ref, gnext_ref, hf_ref, hb_ref, os_ref, hng_ref, x_ref, cw_ref, wo_ref,
                    g2_ref, wr_ref, x1e_ref, afft_ref, *, seq_len):
    i = pl.program_id(0)
    tm = ROW_TILE
    first = ((i * tm) % seq_len) == 0
    last = (((i + 1) * tm) % seq_len) == 0
    g = gcu_ref[...].astype(F32)
    prev_row = jnp.where(first, 0.0, gprev_ref[HALO - 1:HALO, :].astype(F32))
    next_row = jnp.where(last, 0.0, gnext_ref[0:1, :].astype(F32))
    rid = lax.broadcasted_iota(I32, g.shape, 0)
    dn = jnp.where(rid == 0, prev_row, pltpu.roll(g, 1, axis=0))
    up = jnp.where(rid == tm - 1, next_row, pltpu.roll(g, tm - 1, axis=0))
    conv = dn * cw_ref[0:1, :] + g * cw_ref[1:2, :] + up * cw_ref[2:3, :]
    co = (gb_ref[...].astype(F32) * conv).astype(BF16)
    y = jnp.dot(co, wo_ref[:CONV_W, :], preferred_element_type=F32)
    heads = []
    for h in range(N_HEADS):
        hv = slice(h * V_DIM, (h + 1) * V_DIM)
        ht = _rms(hf_ref[:, hv].astype(F32) + hb_ref[:, hv].astype(F32), hng_ref[:, hv])
        heads.append((os_ref[:, hv].astype(F32) * ht).astype(BF16))
    mo = jnp.concatenate(heads, axis=1)
    y = y + jnp.dot(mo, wo_ref[CONV_W:, :], preferred_element_type=F32)
    x1 = x_ref[...] + y
    tokens = _rms(x1, g2_ref[...])
    t_hi = tokens.astype(BF16)
    t_lo = (tokens - t_hi.astype(F32)).astype(BF16)
    p_hi = jnp.dot(t_hi, wr_ref[...], preferred_element_type=F32)
    logits = p_hi[:, :LANES] + p_hi[:, LANES:] + jnp.dot(t_lo, wr_ref[:, :LANES], preferred_element_type=F32)
    lane = lax.broadcasted_iota(I32, logits.shape, 1)
    logits = jnp.where(lane < N_EXPERTS, logits, -jnp.inf)
    ex = jnp.exp(logits - jnp.max(logits, axis=-1, keepdims=True))
    aff = ex / jnp.sum(ex, axis=-1, keepdims=True)
    x1e_ref[:, :D_MODEL] = x1
    row_id = (i * tm + lax.broadcasted_iota(I32, aff.shape, 0)).astype(F32)
    x1e_ref[:, D_MODEL:] = jnp.where(lane == TOKEN_ID_LANE, row_id, aff)
    afft_ref[...] = aff.T[:N_EXPERTS]


def outproj(gb, gcu, hf, hb, osig, hng, x, cw_p, w_out_b, g2, wr_p, seq_len):
    n = x.shape[0]
    tm = ROW_TILE
    halos = tm // HALO
    n_halo = n // HALO
    row = lambda w: pl.BlockSpec((tm, w), lambda i: (i, 0))
    full = lambda a: pl.BlockSpec(a.shape, lambda i: (0,) * a.ndim)
    prev = pl.BlockSpec((HALO, CONV_W), lambda i: (jnp.maximum(i * halos - 1, 0), 0))
    nxt = pl.BlockSpec((HALO, CONV_W), lambda i: (jnp.minimum((i + 1) * halos, n_halo - 1), 0))
    return pl.pallas_call(
        functools.partial(_outproj_kernel, seq_len=seq_len),
        grid=(n // tm,),
        in_specs=[row(CONV_W), row(CONV_W), prev, nxt, row(V_W), row(V_W), row(V_W), full(hng), row(D_MODEL),
                  full(cw_p), full(w_out_b), full(g2), full(wr_p)],
        out_specs=[row(TOK_EXT), pl.BlockSpec((N_EXPERTS, tm), lambda i: (0, i))],
        out_shape=[jax.ShapeDtypeStruct((n, TOK_EXT), F32), jax.ShapeDtypeStruct((N_EXPERTS, n), F32)],
        compiler_params=_cparams("parallel"),
        name="mixer_outproj_router",
    )(gb, gcu, gcu, gcu, hf, hb, osig, hng, x, cw_p, w_out_b, g2, wr_p)


def _ffn_kernel(idx_ref, tok_hbm, g2_ref, wg_ref, wu_ref, wd_ref, ye_ref, xg_ref, wgb_ref, wub_ref, wdb_ref, sem):
    e = pl.program_id(0)
    s = pl.program_id(1)
    n_steps = pl.num_programs(1)
    ts = SLOT_TILE
    t = e * n_steps + s
    last = N_EXPERTS * n_steps - 1

    def row_copy(tile, buf, i):
        tok = idx_ref[tile * ts + i]
        return pltpu.make_async_copy(tok_hbm.at[pl.ds(tok, 1)], xg_ref.at[buf, pl.ds(i, 1)], sem.at[buf])

    def wait_rows(buf):
        pltpu.make_async_copy(tok_hbm.at[pl.ds(0, ts)], xg_ref.at[buf], sem.at[buf]).wait()

    @pl.when(t == 0)
    def _():
        def issue(i, carry):
            row_copy(0, 0, i).start()
            return carry

        lax.fori_loop(0, ts, issue, 0, unroll=8)

    @pl.when(s == 0)
    def _():
        wgb_ref[...] = wg_ref[0, 0].astype(BF16)
        wub_ref[...] = wu_ref[0, 0].astype(BF16)
        wdb_ref[...] = wd_ref[0, 0].astype(BF16)

    def half(buf, next_tile, rows):
        wait_rows(buf)
        x = xg_ref[buf]
        xb = _rms(x[:, :D_MODEL], g2_ref[...]).astype(BF16)
        ext = x[:, D_MODEL:]
        lane = lax.broadcasted_iota(I32, ext.shape, 1)
        gate = jnp.sum(jnp.where(lane == e, ext, 0.0), axis=1, keepdims=True)
        for i in range(ts):
            row_copy(next_tile, 1 - buf, i).start(priority=i % 2)
        hg = jnp.dot(xb, wgb_ref[...], preferred_element_type=F32)
        hu = jnp.dot(xb, wub_ref[...], preferred_element_type=F32)
        hid = (hg * jax.nn.sigmoid(hg) * hu).astype(BF16)
        ye_ref[rows, :D_MODEL] = jnp.dot(hid, wdb_ref[...], preferred_element_type=F32) * gate
        ye_ref[rows, D_MODEL:] = ext

    half(0, 2 * t + 1, slice(0, ts))
    half(1, jnp.minimum(2 * t + 2, 2 * last), slice(ts, 2 * ts))

    @pl.when(t == last)
    def _():
        wait_rows(0)


def expert_ffn(idx_flat, x1e, g2, wg, wu, wd, cap, layer):
    assert cap % (2 * SLOT_TILE) == 0
    n_steps = cap // (2 * SLOT_TILE)
    wspec = lambda: pl.BlockSpec((1, 1, D_MODEL, EXPERT_FF), lambda e, s, *_: (layer, e, 0, 0))
    grid_spec = pltpu.PrefetchScalarGridSpec(
        num_scalar_prefetch=1,
        grid=(N_EXPERTS, n_steps),
        in_specs=[pl.BlockSpec(memory_space=pl.ANY), pl.BlockSpec((1, D_MODEL), lambda e, s, *_: (0, 0)), wspec(), wspec(),
                  pl.BlockSpec((1, 1, EXPERT_FF, D_MODEL), lambda e, s, *_: (layer, e, 0, 0))],
        out_specs=pl.BlockSpec((2 * SLOT_TILE, TOK_EXT), lambda e, s, *_: (e * n_steps + s, 0)),
        scratch_shapes=[pltpu.VMEM((2, SLOT_TILE, TOK_EXT), F32),
                        pltpu.VMEM((D_MODEL, EXPERT_FF), BF16), pltpu.VMEM((D_MODEL, EXPERT_FF), BF16),
                        pltpu.VMEM((EXPERT_FF, D_MODEL), BF16), pltpu.SemaphoreType.DMA((2,))],
    )
    return pl.pallas_call(
        _ffn_kernel,
        grid_spec=grid_spec,
        out_shape=jax.ShapeDtypeStruct((N_EXPERTS * cap, TOK_EXT), F32),
        compiler_params=_cparams("arbitrary", "arbitrary"),
        name="expert_ffn",
    )(idx_flat, x1e, g2, wg, wu, wd)


WIN = 8
WIN_SHIFT = WIN.bit_length() - 1
STACK_TILE = 256


def _stack_rows(tc):
    rows = N_EXPERTS * (tc + 2 * (WIN - 1))
    return -(-rows // STACK_TILE) * STACK_TILE


def _combine_kernel(off_ref, x1_ref, ye_hbm, *rest, cap, n_blocks, final):
    if final:
        fg_ref, out_ref, ys_ref, sem = rest
    else:
        out_ref, ys_ref, sem = rest
    j = pl.program_id(0)
    tc = COMBINE_TILE
    buf = j % 2

    def windows(tile, e):
        lo = off_ref[e * (n_blocks + 1) + tile]
        hi = off_ref[e * (n_blocks + 1) + tile + 1]
        start = (lo >> WIN_SHIFT) << WIN_SHIFT
        return start, jnp.where(hi > lo, (hi - start + (WIN - 1)) >> WIN_SHIFT, 0)

    def stacked_rows(tile):
        total = jnp.int32(0)
        for e in range(N_EXPERTS):
            total = total + windows(tile, e)[1] * WIN
        return total

    def fetch(tile, buf_):
        base = jnp.int32(0)
        for e in range(N_EXPERTS):
            start, n_win = windows(tile, e)

            def issue(w, carry, e=e, start=start, base=base):
                src = pl.multiple_of(e * cap + start + w * WIN, WIN)
                dst = pl.multiple_of(base + w * WIN, WIN)
                pltpu.make_async_copy(ye_hbm.at[pl.ds(src, WIN)], ys_ref.at[buf_, pl.ds(dst, WIN)], sem.at[buf_]).start()
                return carry

            lax.fori_loop(0, n_win, issue, 0)
            base = base + n_win * WIN

    @pl.when(j == 0)
    def _():
        ys_ref[...] = jnp.zeros_like(ys_ref)
        fetch(0, 0)

    base = stacked_rows(j)
    n_total = base >> WIN_SHIFT
    for bit in range((_stack_rows(tc) // WIN).bit_length()):
        @pl.when((n_total & (1 << bit)) != 0)
        def _(bit=bit):
            rows = WIN << bit
            pltpu.make_async_copy(ye_hbm.at[pl.ds(0, rows)], ye_hbm.at[pl.ds(0, rows)], sem.at[buf]).wait()

    @pl.when(j + 1 < n_blocks)
    def _():
        fetch(j + 1, 1 - buf)

    tok0 = (j * tc).astype(F32)
    lane_t = lax.broadcasted_iota(I32, (STACK_TILE, tc), 1).astype(F32)
    row_i = lax.broadcasted_iota(I32, (STACK_TILE, 1), 0)

    def accumulate(kt, acc):
        r0 = pl.multiple_of(kt * STACK_TILE, STACK_TILE)
        rows = ys_ref[buf, pl.ds(r0, STACK_TILE), :]
        tok_local = rows[:, D_MODEL + TOKEN_ID_LANE:D_MODEL + TOKEN_ID_LANE + 1] - tok0
        tok_local = jnp.where(r0 + row_i < base, tok_local, -1.0)
        onehot = jnp.where(tok_local == lane_t, 1.0, 0.0).astype(BF16)
        out_ref[...] += lax.dot_general(onehot, rows[:, :D_MODEL].astype(BF16), (((0,), (0,)), ((), ())),
                                        preferred_element_type=F32)
        return acc

    n_kt = (base + (STACK_TILE - 1)) // STACK_TILE
    out_ref[...] = x1_ref[...]
    lax.fori_loop(0, n_kt, accumulate, 0)
    if final:
        out_ref[...] = _rms(out_ref[...], fg_ref[...])


def combine(off_flat, x1e, ye, cap, final_g=None):
    n = x1e.shape[0]
    tc = COMBINE_TILE
    nb = n // tc
    final = final_g is not None
    in_specs = [pl.BlockSpec((tc, D_MODEL), lambda j, *_: (j, 0)),
                pl.BlockSpec(memory_space=pl.ANY)]
    args = [x1e, ye]
    if final:
        in_specs.append(pl.BlockSpec((1, D_MODEL), lambda j, *_: (0, 0)))
        args.append(final_g)
    grid_spec = pltpu.PrefetchScalarGridSpec(
        num_scalar_prefetch=1,
        grid=(nb,),
        in_specs=in_specs,
        out_specs=pl.BlockSpec((tc, D_MODEL), lambda j, *_: (j, 0)),
        scratch_shapes=[pltpu.VMEM((2, _stack_rows(tc), TOK_EXT), F32), pltpu.SemaphoreType.DMA((2,))],
    )
    return pl.pallas_call(
        functools.partial(_combine_kernel, cap=cap, n_blocks=nb, final=final),
        grid_spec=grid_spec,
        out_shape=jax.ShapeDtypeStruct((n, D_MODEL), F32),
        compiler_params=_cparams("arbitrary"),
        name="moe_combine",
    )(off_flat, *args)


def moe(x1e, aff_t, g2, wg, wu, wd, layer, final_g=None):
    n = x1e.shape[0]
    cap = n // CAPACITY_DIV
    idx, rank = route(aff_t)
    off = jnp.concatenate([rank[:, ::COMBINE_TILE], jnp.full((N_EXPERTS, 1), cap, I32)], axis=1)
    ye = expert_ffn(idx.reshape(-1), x1e, g2, wg, wu, wd, cap, layer)
    return combine(off.reshape(-1), x1e, ye, cap, final_g)


def kernel(x_prompt, x_sample, norm1_g, w_in, conv_w, gate_bias, head_norm_g, w_out, norm2_g, w_router, w_gate, w_up, w_down, final_g):
    depth = w_in.shape[0]
    w_in_p = jnp.pad(w_in, ((0, 0), (0, 0), (0, D_IN_PAD - D_IN))).astype(BF16)
    bias_p = jnp.pad(gate_bias.reshape(depth, 1, 4 * N_HEADS), ((0, 0), (0, 0), (0, LANES - 4 * N_HEADS)))
    cw_p = jnp.pad(conv_w, ((0, 0), (0, 8 - conv_w.shape[1]), (0, 0)))
    wr_f = jnp.pad(w_router, ((0, 0), (0, 0), (0, LANES - N_EXPERTS)))
    wr_hi = wr_f.astype(BF16)
    wr_p = jnp.concatenate([wr_hi, (wr_f - wr_hi.astype(F32)).astype(BF16)], axis=-1)
    w_out_b = w_out.astype(BF16)
    fg = final_g.reshape(1, D_MODEL)

    outs = []
    for x in (x_prompt, x_sample):
        bsz, seq, _ = x.shape
        xf = x.reshape(bsz * seq, D_MODEL)
        for l in range(depth):
            gb, gcu, q, k, v, osig, grow = inproj(xf, norm1_g[l].reshape(1, -1), w_in_p[l], bias_p[l])
            hf, hb = mlstm(q, k, v, grow, seq)
            g2 = norm2_g[l].reshape(1, -1)
            x1e, aff_t = outproj(gb, gcu, hf, hb, osig, head_norm_g[l].reshape(1, -1), xf, cw_p[l], w_out_b[l],
                                 g2, wr_p[l], seq)
            xf = moe(x1e, aff_t, g2, w_gate, w_up, w_down, l, fg if l == depth - 1 else None)
        outs.append(xf.reshape(bsz, seq, D_MODEL))
    return tuple(outs)
```

```python
import functools

import jax
import jax.numpy as jnp
from jax import lax
from jax.experimental import pallas as pl
from jax.experimental.pallas import tpu as pltpu

F32 = jnp.float32
BF16 = jnp.bfloat16
I32 = jnp.int32

LANES = 128
N_EXPERTS = 16
CAPACITY_DIV = 8
DISP_BITS = 16
VALID_BIT = 24
TOKEN_ID_LANE = N_EXPERTS


def _threshold_kernel(aff_ref, thr_ref, need_ref, *, cap):
    bits = pltpu.bitcast(aff_ref[...], I32)
    cap_f = jnp.float32(cap)

    def count_ge(cand):
        return jnp.sum(jnp.where(bits >= cand, 1.0, 0.0), axis=1, keepdims=True)

    def body(i, thr):
        cand = thr | jnp.left_shift(jnp.int32(1), 30 - i)
        return jnp.where(count_ge(cand) >= cap_f, cand, thr)

    thr = lax.fori_loop(0, 31, body, jnp.zeros((N_EXPERTS, 1), I32))
    n_gt = jnp.sum(jnp.where(bits > thr, 1.0, 0.0), axis=1, keepdims=True)
    need = (cap_f - n_gt).astype(I32)
    thr_ref[...] = jnp.broadcast_to(thr, thr_ref.shape)
    need_ref[...] = jnp.broadcast_to(need, need_ref.shape)


def _lane_inclusive_scan(x, lane):
    for b in range(7):
        s = 1 << b
        x = x + jnp.where(lane >= s, pltpu.roll(x, s, axis=1), 0.0)
    return x


def _row_exclusive_scan(t, row, n_rows):
    inc = t
    s = 1
    while s < n_rows:
        inc = inc + jnp.where(row >= s, pltpu.roll(inc, s, axis=0), 0.0)
        s *= 2
    return inc - t


def _token_exclusive_scan(x, lane, row, n_rows):
    inc = _lane_inclusive_scan(x, lane)
    tot = jnp.broadcast_to(inc[:, LANES - 1:LANES], x.shape)
    return inc - x + _row_exclusive_scan(tot, row, n_rows)


def _compact_kernel(thr_ref, need_ref, aff_ref, idx_ref, rank_ref, *, n_rows, cap_rows):
    e = pl.program_id(0)
    shape = (n_rows, LANES)
    lane = lax.broadcasted_iota(I32, shape, 1)
    row = lax.broadcasted_iota(I32, shape, 0)
    bits = pltpu.bitcast(aff_ref[0], I32)
    thr = thr_ref[e]
    need = need_ref[e].astype(F32)
    eq = bits == thr
    pre_eq = _token_exclusive_scan(jnp.where(eq, 1.0, 0.0), lane, row, n_rows)
    sel = (bits > thr) | (eq & (pre_eq < need))
    sel_f = jnp.where(sel, 1.0, 0.0)
    rank = _token_exclusive_scan(sel_f, lane, row, n_rows).astype(I32)
    rank_ref[0] = rank

    pos = row * LANES + lane
    disp = pos - rank
    v = jnp.where(sel, disp | (1 << VALID_BIT), 0)
    n_bits = (n_rows * LANES - 1).bit_length()
    for b in range(n_bits):
        if b < 7:
            s = 1 << b
            r1 = pltpu.roll(v, LANES - s, axis=1)
            r2 = pltpu.roll(r1, n_rows - 1, axis=0)
            moved = jnp.where(lane < LANES - s, r1, r2)
        else:
            sr = 1 << (b - 7)
            moved = pltpu.roll(v, n_rows - sr, axis=0)
        take = ((moved >> VALID_BIT) & 1 == 1) & ((moved >> b) & 1 == 1)
        stay = ((v >> VALID_BIT) & 1 == 1) & ((v >> b) & 1 == 0)
        v = jnp.where(take, moved, jnp.where(stay, v, 0))
    idx_ref[0] = (pos + (v & ((1 << DISP_BITS) - 1)))[:cap_rows]


def route(aff_t):
    n_exp, n = aff_t.shape
    assert n <= (1 << DISP_BITS)
    cap = n // CAPACITY_DIV
    n_rows = n // LANES
    cap_rows = cap // LANES
    thr, need = pl.pallas_call(
        functools.partial(_threshold_kernel, cap=cap),
        out_shape=(jax.ShapeDtypeStruct((n_exp, LANES), I32), jax.ShapeDtypeStruct((n_exp, LANES), I32)),
        name="route_threshold",
    )(aff_t)
    grid_spec = pltpu.PrefetchScalarGridSpec(
        num_scalar_prefetch=2,
        grid=(n_exp,),
        in_specs=[pl.BlockSpec((1, n_rows, LANES), lambda e, *_: (e, 0, 0))],
        out_specs=[
            pl.BlockSpec((1, cap_rows, LANES), lambda e, *_: (e, 0, 0)),
            pl.BlockSpec((1, n_rows, LANES), lambda e, *_: (e, 0, 0)),
        ],
    )
    idx, rank = pl.pallas_call(
        functools.partial(_compact_kernel, n_rows=n_rows, cap_rows=cap_rows),
        grid_spec=grid_spec,
        out_shape=(
            jax.ShapeDtypeStruct((n_exp, cap_rows, LANES), I32),
            jax.ShapeDtypeStruct((n_exp, n_rows, LANES), I32),
        ),
        compiler_params=pltpu.CompilerParams(dimension_semantics=("parallel",)),
        name="route_compact",
    )(thr[:, 0], need[:, 0], aff_t.reshape(n_exp, n_rows, LANES))
    return idx.reshape(n_exp, cap), rank.reshape(n_exp, n)


D_MODEL = 1024
CONV_W = 512
N_HEADS = 4
V_DIM = 128
QK_DIM = 64
QK_W = N_HEADS * QK_DIM
V_W = N_HEADS * V_DIM
CHUNK = 128
EXPERT_FF = 1024
EPS = 1e-6
C_GB, C_GC, C_U, C_Q, C_K, C_V, C_O, C_G = 0, 512, 1024, 1536, 1792, 2048, 2560, 3072
D_IN = 3088
D_IN_PAD = 3200
TOK_EXT = D_MODEL + LANES

ROW_TILE = 512
INPROJ_TILE = 1024
MLSTM_BLOCK = 256
SLOT_TILE = 512
COMBINE_TILE = 256
VMEM_LIMIT = 56 * 1024 * 1024


def _cparams(*sem):
    return pltpu.CompilerParams(dimension_semantics=sem, vmem_limit_bytes=VMEM_LIMIT)


def _rms(x, g):
    return x * lax.rsqrt(jnp.mean(x * x, axis=-1, keepdims=True) + EPS) * g


GATE_GROUP = 8
GATE_ROWS = 6 * GATE_GROUP
G_A, G_PM, G_B = 0, 1, 2


def _chunk_scan(x, op, identity, reverse):
    width = x.shape[1]
    pos = lax.broadcasted_iota(I32, x.shape, 1) & (CHUNK - 1)
    for b in range(7):
        s = 1 << b
        if reverse:
            shifted = jnp.where(pos < CHUNK - s, pltpu.roll(x, width - s, axis=1), identity)
        else:
            shifted = jnp.where(pos >= s, pltpu.roll(x, s, axis=1), identity)
        x = op(x, shifted)
    return x


def _inproj_kernel(x_ref, g_ref, w_ref, bias_ref, gb_ref, gcu_ref, q_ref, k_ref, v_ref, os_ref, grow_ref):
    hn = _rms(x_ref[...], g_ref[...]).astype(BF16)

    def seg(a, b):
        return jnp.dot(hn, w_ref[:, a:b], preferred_element_type=F32)

    gates = seg(C_G, D_IN_PAD) + bias_ref[...]
    lane = lax.broadcasted_iota(I32, gates.shape, 1)
    log_sig = jnp.minimum(gates, 0.0) - jnp.log1p(jnp.exp(-jnp.abs(gates)))
    g16 = jnp.where((lane // N_HEADS) % 2 == 1, log_sig, gates).T[:4 * N_HEADS]
    fwd = lax.broadcasted_iota(I32, g16.shape, 0) < 2 * N_HEADS
    cs = jnp.where(fwd, _chunk_scan(g16, jnp.add, 0.0, False), _chunk_scan(g16, jnp.add, 0.0, True))
    cs = pltpu.roll(cs, 3 * N_HEADS, axis=0)
    a = g16 - cs
    neg_inf = jnp.float32(-jnp.inf)
    pm = jnp.where(fwd, _chunk_scan(a, jnp.maximum, neg_inf, False), _chunk_scan(a, jnp.maximum, neg_inf, True))
    for d in range(2):
        grp = slice(d * GATE_GROUP, (d + 1) * GATE_GROUP)
        for g, val in ((G_A, a), (G_PM, pm), (G_B, cs)):
            r0 = (3 * d + g) * GATE_GROUP
            grow_ref[r0:r0 + GATE_GROUP, :] = val[grp]

    gb_ref[...] = seg(C_GB, C_GC).astype(BF16)
    gcu_ref[...] = (seg(C_GC, C_U) * seg(C_U, C_Q)).astype(BF16)
    q_ref[...] = (seg(C_Q, C_K) * (QK_DIM ** -0.5)).astype(BF16)
    k_ref[...] = seg(C_K, C_V).astype(BF16)
    v_ref[...] = seg(C_V, C_O).astype(BF16)
    os_ref[...] = jax.nn.sigmoid(seg(C_O, C_G)).astype(BF16)


def inproj(x, g1, w_in_p, bias_p):
    n = x.shape[0]
    tm = INPROJ_TILE
    row = lambda w: pl.BlockSpec((tm, w), lambda i: (i, 0))
    full = lambda a: pl.BlockSpec(a.shape, lambda i: (0,) * a.ndim)
    return pl.pallas_call(
        _inproj_kernel,
        grid=(n // tm,),
        in_specs=[row(D_MODEL), full(g1), full(w_in_p), full(bias_p)],
        out_specs=[row(CONV_W), row(CONV_W), row(QK_W), row(QK_W), row(V_W), row(V_W),
                   pl.BlockSpec((GATE_ROWS, tm), lambda i: (0, i))],
        out_shape=[
            jax.ShapeDtypeStruct((n, CONV_W), BF16), jax.ShapeDtypeStruct((n, CONV_W), BF16),
            jax.ShapeDtypeStruct((n, QK_W), BF16), jax.ShapeDtypeStruct((n, QK_W), BF16),
            jax.ShapeDtypeStruct((n, V_W), BF16), jax.ShapeDtypeStruct((n, V_W), BF16),
            jax.ShapeDtypeStruct((GATE_ROWS, n), F32),
        ],
        compiler_params=_cparams("parallel"),
        name="mixer_inproj",
    )(x, g1, w_in_p, bias_p)


def _mlstm_kernel(qf_ref, kf_ref, vf_ref, gf_ref, qb_ref, kb_ref, vb_ref, gb_ref, hf_ref, hb_ref,
                  cf_ref, mf_ref, cb_ref, mb_ref, *, seq_len):
    j = pl.program_id(0)

    @pl.when(((j * MLSTM_BLOCK) % seq_len) == 0)
    def _():
        for ref in (cf_ref, mf_ref, cb_ref, mb_ref):
            ref[...] = jnp.zeros_like(ref)

    terms_f = _mlstm_step_terms(gf_ref, mf_ref, reverse=False)
    terms_b = _mlstm_step_terms(gb_ref, mb_ref, reverse=True)
    _mlstm_block(qf_ref, kf_ref, vf_ref, hf_ref, cf_ref, terms_f, reverse=False)
    _mlstm_block(qb_ref, kb_ref, vb_ref, hb_ref, cb_ref, terms_b, reverse=True)


def _chunk_order(reverse):
    n_chunks = MLSTM_BLOCK // CHUNK
    return list(range(n_chunks - 1, -1, -1) if reverse else range(n_chunks))


def _mlstm_step_terms(grow_ref, m_ref, *, reverse):
    d = 3 if reverse else 0

    def gate_rows(g, cols):
        r0 = (d + g) * GATE_GROUP
        return grow_ref[r0:r0 + N_HEADS, cols]

    order = _chunk_order(reverse)
    end = 0 if reverse else CHUNK - 1
    m_old = m_ref[0:N_HEADS, :]
    m_in = {}
    for c in order:
        last = slice(c * CHUNK + end, c * CHUNK + end + 1)
        m_in[c] = m_old
        m_old = gate_rows(G_B, last) + jnp.maximum(m_old, gate_rows(G_PM, last))
    m_ref[0:N_HEADS, :] = m_old

    terms = {}
    for c in order:
        r0 = c * CHUNK
        rows = slice(r0, r0 + CHUNK)
        m_o = m_in[c]
        a = gate_rows(G_A, rows)
        mm = jnp.maximum(m_o, gate_rows(G_PM, rows))
        mm_last = jnp.maximum(m_o, gate_rows(G_PM, slice(r0 + end, r0 + end + 1)))
        sc = jnp.exp(m_o - mm)
        emt = jnp.exp(-(gate_rows(G_B, rows) + mm))
        w = jnp.exp(a - mm_last)
        decay = jnp.exp(m_o - mm_last)
        stack = jnp.concatenate([mm, sc, emt, jnp.zeros((LANES - 3 * N_HEADS, CHUNK), F32)], axis=0)
        terms[c] = (a, w, decay, stack.T)
    return terms


def _mlstm_block(q_ref, k_ref, v_ref, out_ref, c_ref, terms, *, reverse):
    t_i = lax.broadcasted_iota(I32, (CHUNK, CHUNK), 0)
    s_i = lax.broadcasted_iota(I32, (CHUNK, CHUNK), 1)
    tri = (s_i >= t_i) if reverse else (s_i <= t_i)
    lane = lax.broadcasted_iota(I32, (CHUNK, LANES), 1)
    half_masks = [jnp.where(lane // QK_DIM == hh, 1.0, 0.0).astype(BF16) for hh in range(2)]
    ones_col = jnp.where(lane == 0, 1.0, 0.0).astype(BF16)
    s_mats, kmts = {}, {}
    for c in _chunk_order(reverse):
        rows = slice(c * CHUNK, (c + 1) * CHUNK)
        a, w, _, cols = terms[c]
        for h in range(N_HEADS):
            pair = slice((h // 2) * LANES, (h // 2 + 1) * LANES)
            km = k_ref[rows, pair] * half_masks[h % 2]
            kmts[c, h] = (km.astype(F32).T * w[h:h + 1, :]).astype(BF16)
            dmat = jnp.where(tri, jnp.exp(a[h:h + 1, :] - cols[:, h:h + 1]), 0.0)
            s_mats[c, h] = lax.dot_general(q_ref[rows, pair], km, (((1,), (1,)), ((), ())),
                                           preferred_element_type=F32) * dmat

    c_state = [c_ref[h] for h in range(N_HEADS)]
    for c in _chunk_order(reverse):
        rows = slice(c * CHUNK, (c + 1) * CHUNK)
        _, _, decay, cols = terms[c]
        for h in range(N_HEADS):
            pair = slice((h // 2) * LANES, (h // 2 + 1) * LANES)
            hv = slice(h * V_DIM, (h + 1) * V_DIM)
            vh = v_ref[rows, hv]
            c_old = c_state[h]
            sc_col = cols[:, N_HEADS + h:N_HEADS + h + 1]
            emt_col = cols[:, 2 * N_HEADS + h:2 * N_HEADS + h + 1]
            s_mat = s_mats[c, h]
            qc = jnp.dot(q_ref[rows, pair], c_old.astype(BF16), preferred_element_type=F32)
            num = jnp.dot(s_mat.astype(BF16), vh, preferred_element_type=F32) + sc_col * qc[:, :V_DIM]
            den = jnp.sum(s_mat, axis=1, keepdims=True) + sc_col * qc[:, V_DIM:V_DIM + 1]
            h_out = num * (1.0 / jnp.maximum(jnp.abs(den), emt_col))
            kv = jnp.dot(kmts[c, h], jnp.concatenate([vh, ones_col], axis=1), preferred_element_type=F32)
            c_state[h] = decay[h:h + 1, 0:1] * c_old + kv
            out_ref[rows, hv] = h_out.astype(BF16)
    for h in range(N_HEADS):
        c_ref[h] = c_state[h]


def mlstm(q, k, v, grow, seq_len):
    n = q.shape[0]
    blk = MLSTM_BLOCK
    nb = n // blk
    assert n % seq_len == 0 and seq_len % blk == 0
    fwd = lambda w: pl.BlockSpec((blk, w), lambda j: (j, 0))
    bwd = lambda w: pl.BlockSpec((blk, w), lambda j: (nb - 1 - j, 0))
    state = [pltpu.VMEM((N_HEADS, LANES, 2 * LANES), F32), pltpu.VMEM((8, LANES), F32)]
    return pl.pallas_call(
        functools.partial(_mlstm_kernel, seq_len=seq_len),
        grid=(nb,),
        in_specs=[fwd(QK_W), fwd(QK_W), fwd(V_W), pl.BlockSpec((GATE_ROWS, blk), lambda j: (0, j)),
                  bwd(QK_W), bwd(QK_W), bwd(V_W), pl.BlockSpec((GATE_ROWS, blk), lambda j: (0, nb - 1 - j))],
        out_specs=[fwd(V_W), bwd(V_W)],
        out_shape=[jax.ShapeDtypeStruct((n, V_W), BF16), jax.ShapeDtypeStruct((n, V_W), BF16)],
        scratch_shapes=state + state,
        compiler_params=_cparams("arbitrary"),
        name="mlstm",
    )(q, k, v, grow, q, k, v, grow)


HALO = 16


def _outproj_kernel(gb_ref, gcu_ref, gprev_ref, gnext_ref, hf_ref, hb_ref, os_ref, hng_ref, x_ref, cw_ref, wo_ref,
                    g2_ref, wr_ref, x1e_ref, afft_ref, *, seq_len):
    i = pl.program_id(0)
    tm = ROW_TILE
    first = ((i * tm) % seq_len) == 0
    last = (((i + 1) * tm) % seq_len) == 0
    g = gcu_ref[...].astype(F32)
    prev_row = jnp.where(first, 0.0, gprev_ref[HALO - 1:HALO, :].astype(F32))
    next_row = jnp.where(last, 0.0, gnext_ref[0:1, :].astype(F32))
    rid = lax.broadcasted_iota(I32, g.shape, 0)
    dn = jnp.where(rid == 0, prev_row, pltpu.roll(g, 1, axis=0))
    up = jnp.where(rid == tm - 1, next_row, pltpu.roll(g, tm - 1, axis=0))
    conv = dn * cw_ref[0:1, :] + g * cw_ref[1:2, :] + up * cw_ref[2:3, :]
    co = (gb_ref[...].astype(F32) * conv).astype(BF16)
    y = jnp.dot(co, wo_ref[:CONV_W, :], preferred_element_type=F32)
    heads = []
    for h in range(N_HEADS):
        hv = slice(h * V_DIM, (h + 1) * V_DIM)
        ht = _rms(hf_ref[:, hv].astype(F32) + hb_ref[:, hv].astype(F32), hng_ref[:, hv])
        heads.append((os_ref[:, hv].astype(F32) * ht).astype(BF16))
    mo = jnp.concatenate(heads, axis=1)
    y = y + jnp.dot(mo, wo_ref[CONV_W:, :], preferred_element_type=F32)
    x1 = x_ref[...] + y
    tokens = _rms(x1, g2_ref[...])
    t_hi = tokens.astype(BF16)
    t_lo = (tokens - t_hi.astype(F32)).astype(BF16)
    p_hi = jnp.dot(t_hi, wr_ref[...], preferred_element_type=F32)
    logits = p_hi[:, :LANES] + p_hi[:, LANES:] + jnp.dot(t_lo, wr_ref[:, :LANES], preferred_element_type=F32)
    lane = lax.broadcasted_iota(I32, logits.shape, 1)
    logits = jnp.where(lane < N_EXPERTS, logits, -jnp.inf)
    ex = jnp.exp(logits - jnp.max(logits, axis=-1, keepdims=True))
    aff = ex / jnp.sum(ex, axis=-1, keepdims=True)
    x1e_ref[:, :D_MODEL] = x1
    row_id = (i * tm + lax.broadcasted_iota(I32, aff.shape, 0)).astype(F32)
    x1e_ref[:, D_MODEL:] = jnp.where(lane == TOKEN_ID_LANE, row_id, aff)
    afft_ref[...] = aff.T[:N_EXPERTS]


def outproj(gb, gcu, hf, hb, osig, hng, x, cw_p, w_out_b, g2, wr_p, seq_len):
    n = x.shape[0]
    tm = ROW_TILE
    halos = tm // HALO
    n_halo = n // HALO
    row = lambda w: pl.BlockSpec((tm, w), lambda i: (i, 0))
    full = lambda a: pl.BlockSpec(a.shape, lambda i: (0,) * a.ndim)
    prev = pl.BlockSpec((HALO, CONV_W), lambda i: (jnp.maximum(i * halos - 1, 0), 0))
    nxt = pl.BlockSpec((HALO, CONV_W), lambda i: (jnp.minimum((i + 1) * halos, n_halo - 1), 0))
    return pl.pallas_call(
        functools.partial(_outproj_kernel, seq_len=seq_len),
        grid=(n // tm,),
        in_specs=[row(CONV_W), row(CONV_W), prev, nxt, row(V_W), row(V_W), row(V_W), full(hng), row(D_MODEL),
                  full(cw_p), full(w_out_b), full(g2), full(wr_p)],
        out_specs=[row(TOK_EXT), pl.BlockSpec((N_EXPERTS, tm), lambda i: (0, i))],
        out_shape=[jax.ShapeDtypeStruct((n, TOK_EXT), F32), jax.ShapeDtypeStruct((N_EXPERTS, n), F32)],
        compiler_params=_cparams("parallel"),
        name="mixer_outproj_router",
    )(gb, gcu, gcu, gcu, hf, hb, osig, hng, x, cw_p, w_out_b, g2, wr_p)


def _ffn_kernel(idx_ref, tok_hbm, g2_ref, wg_ref, wu_ref, wd_ref, ye_ref, xg_ref, wgb_ref, wub_ref, wdb_ref, sem):
    e = pl.program_id(0)
    s = pl.program_id(1)
    n_steps = pl.num_programs(1)
    ts = SLOT_TILE
    t = e * n_steps + s
    last = N_EXPERTS * n_steps - 1

    def row_copy(tile, buf, i):
        tok = idx_ref[tile * ts + i]
        return pltpu.make_async_copy(tok_hbm.at[pl.ds(tok, 1)], xg_ref.at[buf, pl.ds(i, 1)], sem.at[buf])

    def wait_rows(buf):
        pltpu.make_async_copy(tok_hbm.at[pl.ds(0, ts)], xg_ref.at[buf], sem.at[buf]).wait()

    @pl.when(t == 0)
    def _():
        def issue(i, carry):
            row_copy(0, 0, i).start()
            return carry

        lax.fori_loop(0, ts, issue, 0, unroll=8)

    @pl.when(s == 0)
    def _():
        wgb_ref[...] = wg_ref[0, 0].astype(BF16)
        wub_ref[...] = wu_ref[0, 0].astype(BF16)
        wdb_ref[...] = wd_ref[0, 0].astype(BF16)

    def half(buf, next_tile, rows):
        wait_rows(buf)
        x = xg_ref[buf]
        xb = _rms(x[:, :D_MODEL], g2_ref[...]).astype(BF16)
        ext = x[:, D_MODEL:]
        lane = lax.broadcasted_iota(I32, ext.shape, 1)
        gate = jnp.sum(jnp.where(lane == e, ext, 0.0), axis=1, keepdims=True)
        for i in range(ts):
            row_copy(next_tile, 1 - buf, i).start(priority=i % 2)
        hg = jnp.dot(xb, wgb_ref[...], preferred_element_type=F32)
        hu = jnp.dot(xb, wub_ref[...], preferred_element_type=F32)
        hid = (hg * jax.nn.sigmoid(hg) * hu).astype(BF16)
        ye_ref[rows, :D_MODEL] = jnp.dot(hid, wdb_ref[...], preferred_element_type=F32) * gate
        ye_ref[rows, D_MODEL:] = ext

    half(0, 2 * t + 1, slice(0, ts))
    half(1, jnp.minimum(2 * t + 2, 2 * last), slice(ts, 2 * ts))

    @pl.when(t == last)
    def _():
        wait_rows(0)


def expert_ffn(idx_flat, x1e, g2, wg, wu, wd, cap, layer):
    assert cap % (2 * SLOT_TILE) == 0
    n_steps = cap // (2 * SLOT_TILE)
    wspec = lambda: pl.BlockSpec((1, 1, D_MODEL, EXPERT_FF), lambda e, s, *_: (layer, e, 0, 0))
    grid_spec = pltpu.PrefetchScalarGridSpec(
        num_scalar_prefetch=1,
        grid=(N_EXPERTS, n_steps),
        in_specs=[pl.BlockSpec(memory_space=pl.ANY), pl.BlockSpec((1, D_MODEL), lambda e, s, *_: (0, 0)), wspec(), wspec(),
                  pl.BlockSpec((1, 1, EXPERT_FF, D_MODEL), lambda e, s, *_: (layer, e, 0, 0))],
        out_specs=pl.BlockSpec((2 * SLOT_TILE, TOK_EXT), lambda e, s, *_: (e * n_steps + s, 0)),
        scratch_shapes=[pltpu.VMEM((2, SLOT_TILE, TOK_EXT), F32),
                        pltpu.VMEM((D_MODEL, EXPERT_FF), BF16), pltpu.VMEM((D_MODEL, EXPERT_FF), BF16),
                        pltpu.VMEM((EXPERT_FF, D_MODEL), BF16), pltpu.SemaphoreType.DMA((2,))],
    )
    return pl.pallas_call(
        _ffn_kernel,
        grid_spec=grid_spec,
        out_shape=jax.ShapeDtypeStruct((N_EXPERTS * cap, TOK_EXT), F32),
        compiler_params=_cparams("arbitrary", "arbitrary"),
        name="expert_ffn",
    )(idx_flat, x1e, g2, wg, wu, wd)


WIN = 8
WIN_SHIFT = WIN.bit_length() - 1
STACK_TILE = 256


def _stack_rows(tc):
    rows = N_EXPERTS * (tc + 2 * (WIN - 1))
    return -(-rows // STACK_TILE) * STACK_TILE


def _combine_kernel(off_ref, x1_ref, ye_hbm, *rest, cap, n_blocks, final):
    if final:
        fg_ref, out_ref, ys_ref, sem = rest
    else:
        out_ref, ys_ref, sem = rest
    j = pl.program_id(0)
    tc = COMBINE_TILE
    buf = j % 2

    def windows(tile, e):
        lo = off_ref[e * (n_blocks + 1) + tile]
        hi = off_ref[e * (n_blocks + 1) + tile + 1]
        start = (lo >> WIN_SHIFT) << WIN_SHIFT
        return start, jnp.where(hi > lo, (hi - start + (WIN - 1)) >> WIN_SHIFT, 0)

    def stacked_rows(tile):
        total = jnp.int32(0)
        for e in range(N_EXPERTS):
            total = total + windows(tile, e)[1] * WIN
        return total

    def fetch(tile, buf_):
        base = jnp.int32(0)
        for e in range(N_EXPERTS):
            start, n_win = windows(tile, e)

            def issue(w, carry, e=e, start=start, base=base):
                src = pl.multiple_of(e * cap + start + w * WIN, WIN)
                dst = pl.multiple_of(base + w * WIN, WIN)
                pltpu.make_async_copy(ye_hbm.at[pl.ds(src, WIN)], ys_ref.at[buf_, pl.ds(dst, WIN)], sem.at[buf_]).start()
                return carry

            lax.fori_loop(0, n_win, issue, 0)
            base = base + n_win * WIN

    @pl.when(j == 0)
    def _():
        ys_ref[...] = jnp.zeros_like(ys_ref)
        fetch(0, 0)

    base = stacked_rows(j)
    n_total = base >> WIN_SHIFT
    for bit in range((_stack_rows(tc) // WIN).bit_length()):
        @pl.when((n_total & (1 << bit)) != 0)
        def _(bit=bit):
            rows = WIN << bit
            pltpu.make_async_copy(ye_hbm.at[pl.ds(0, rows)], ye_hbm.at[pl.ds(0, rows)], sem.at[buf]).wait()

    @pl.when(j + 1 < n_blocks)
    def _():
        fetch(j + 1, 1 - buf)

    tok0 = (j * tc).astype(F32)
    lane_t = lax.broadcasted_iota(I32, (STACK_TILE, tc), 1).astype(F32)
    row_i = lax.broadcasted_iota(I32, (STACK_TILE, 1), 0)

    def accumulate(kt, acc):
        r0 = pl.multiple_of(kt * STACK_TILE, STACK_TILE)
        rows = ys_ref[buf, pl.ds(r0, STACK_TILE), :]
        tok_local = rows[:, D_MODEL + TOKEN_ID_LANE:D_MODEL + TOKEN_ID_LANE + 1] - tok0
        tok_local = jnp.where(r0 + row_i < base, tok_local, -1.0)
        onehot = jnp.where(tok_local == lane_t, 1.0, 0.0).astype(BF16)
        out_ref[...] += lax.dot_general(onehot, rows[:, :D_MODEL].astype(BF16), (((0,), (0,)), ((), ())),
                                        preferred_element_type=F32)
        return acc

    n_kt = (base + (STACK_TILE - 1)) // STACK_TILE
    out_ref[...] = x1_ref[...]
    lax.fori_loop(0, n_kt, accumulate, 0)
    if final:
        out_ref[...] = _rms(out_ref[...], fg_ref[...])


def combine(off_flat, x1e, ye, cap, final_g=None):
    n = x1e.shape[0]
    tc = COMBINE_TILE
    nb = n // tc
    final = final_g is not None
    in_specs = [pl.BlockSpec((tc, D_MODEL), lambda j, *_: (j, 0)),
                pl.BlockSpec(memory_space=pl.ANY)]
    args = [x1e, ye]
    if final:
        in_specs.append(pl.BlockSpec((1, D_MODEL), lambda j, *_: (0, 0)))
        args.append(final_g)
    grid_spec = pltpu.PrefetchScalarGridSpec(
        num_scalar_prefetch=1,
        grid=(nb,),
        in_specs=in_specs,
        out_specs=pl.BlockSpec((tc, D_MODEL), lambda j, *_: (j, 0)),
        scratch_shapes=[pltpu.VMEM((2, _stack_rows(tc), TOK_EXT), F32), pltpu.SemaphoreType.DMA((2,))],
    )
    return pl.pallas_call(
        functools.partial(_combine_kernel, cap=cap, n_blocks=nb, final=final),
        grid_spec=grid_spec,
        out_shape=jax.ShapeDtypeStruct((n, D_MODEL), F32),
        compiler_params=_cparams("arbitrary"),
        name="moe_combine",
    )(off_flat, *args)


def moe(x1e, aff_t, g2, wg, wu, wd, layer, final_g=None):
    n = x1e.shape[0]
    cap = n // CAPACITY_DIV
    idx, rank = route(aff_t)
    off = jnp.concatenate([rank[:, ::COMBINE_TILE], jnp.full((N_EXPERTS, 1), cap, I32)], axis=1)
    ye = expert_ffn(idx.reshape(-1), x1e, g2, wg, wu, wd, cap, layer)
    return combine(off.reshape(-1), x1e, ye, cap, final_g)


def kernel(x_prompt, x_sample, norm1_g, w_in, conv_w, gate_bias, head_norm_g, w_out, norm2_g, w_router, w_gate, w_up, w_down, final_g):
    depth = w_in.shape[0]
    w_in_p = jnp.pad(w_in, ((0, 0), (0, 0), (0, D_IN_PAD - D_IN))).astype(BF16)
    bias_p = jnp.pad(gate_bias.reshape(depth, 1, 4 * N_HEADS), ((0, 0), (0, 0), (0, LANES - 4 * N_HEADS)))
    cw_p = jnp.pad(conv_w, ((0, 0), (0, 8 - conv_w.shape[1]), (0, 0)))
    wr_f = jnp.pad(w_router, ((0, 0), (0, 0), (0, LANES - N_EXPERTS)))
    wr_hi = wr_f.astype(BF16)
    wr_p = jnp.concatenate([wr_hi, (wr_f - wr_hi.astype(F32)).astype(BF16)], axis=-1)
    w_out_b = w_out.astype(BF16)
    fg = final_g.reshape(1, D_MODEL)

    outs = []
    for x in (x_prompt, x_sample):
        bsz, seq, _ = x.shape
        xf = x.reshape(bsz * seq, D_MODEL)
        for l in range(depth):
            gb, gcu, q, k, v, osig, grow = inproj(xf, norm1_g[l].reshape(1, -1), w_in_p[l], bias_p[l])
            hf, hb = mlstm(q, k, v, grow, seq)
            g2 = norm2_g[l].reshape(1, -1)
            x1e, aff_t = outproj(gb, gcu, hf, hb, osig, head_norm_g[l].reshape(1, -1), xf, cw_p[l], w_out_b[l],
                                 g2, wr_p[l], seq)
            xf = moe(x1e, aff_t, g2, w_gate, w_up, w_down, l, fg if l == depth - 1 else None)
        outs.append(xf.reshape(bsz, seq, D_MODEL))
    return tuple(outs)
```

```python
import functools

import jax
import jax.numpy as jnp
from jax import lax
from jax.experimental import pallas as pl
from jax.experimental.pallas import tpu as pltpu

F32 = jnp.float32
BF16 = jnp.bfloat16
I32 = jnp.int32

LANES = 128
N_EXPERTS = 16
CAPACITY_DIV = 8
DISP_BITS = 16
VALID_BIT = 24
TOKEN_ID_LANE = N_EXPERTS


def _threshold_kernel(aff_ref, thr_ref, need_ref, *, cap):
    bits = pltpu.bitcast(aff_ref[...], I32)
    cap_f = jnp.float32(cap)

    def count_ge(cand):
        return jnp.sum(jnp.where(bits >= cand, 1.0, 0.0), axis=1, keepdims=True)

    def body(i, thr):
        cand = thr | jnp.left_shift(jnp.int32(1), 30 - i)
        return jnp.where(count_ge(cand) >= cap_f, cand, thr)

    thr = lax.fori_loop(0, 31, body, jnp.zeros((N_EXPERTS, 1), I32))
    n_gt = jnp.sum(jnp.where(bits > thr, 1.0, 0.0), axis=1, keepdims=True)
    need = (cap_f - n_gt).astype(I32)
    thr_ref[...] = jnp.broadcast_to(thr, thr_ref.shape)
    need_ref[...] = jnp.broadcast_to(need, need_ref.shape)


def _lane_inclusive_scan(x, lane):
    for b in range(7):
        s = 1 << b
        x = x + jnp.where(lane >= s, pltpu.roll(x, s, axis=1), 0.0)
    return x


def _row_exclusive_scan(t, row, n_rows):
    inc = t
    s = 1
    while s < n_rows:
        inc = inc + jnp.where(row >= s, pltpu.roll(inc, s, axis=0), 0.0)
        s *= 2
    return inc - t


def _token_exclusive_scan(x, lane, row, n_rows):
    inc = _lane_inclusive_scan(x, lane)
    tot = jnp.broadcast_to(inc[:, LANES - 1:LANES], x.shape)
    return inc - x + _row_exclusive_scan(tot, row, n_rows)


def _compact_kernel(thr_ref, need_ref, aff_ref, idx_ref, rank_ref, *, n_rows, cap_rows):
    e = pl.program_id(0)
    shape = (n_rows, LANES)
    lane = lax.broadcasted_iota(I32, shape, 1)
    row = lax.broadcasted_iota(I32, shape, 0)
    bits = pltpu.bitcast(aff_ref[0], I32)
    thr = thr_ref[e]
    need = need_ref[e].astype(F32)
    eq = bits == thr
    pre_eq = _token_exclusive_scan(jnp.where(eq, 1.0, 0.0), lane, row, n_rows)
    sel = (bits > thr) | (eq & (pre_eq < need))
    sel_f = jnp.where(sel, 1.0, 0.0)
    rank = _token_exclusive_scan(sel_f, lane, row, n_rows).astype(I32)
    rank_ref[0] = rank

    pos = row * LANES + lane
    disp = pos - rank
    v = jnp.where(sel, disp | (1 << VALID_BIT), 0)
    n_bits = (n_rows * LANES - 1).bit_length()
    for b in range(n_bits):
        if b < 7:
            s = 1 << b
            r1 = pltpu.roll(v, LANES - s, axis=1)
            r2 = pltpu.roll(r1, n_rows - 1, axis=0)
            moved = jnp.where(lane < LANES - s, r1, r2)
        else:
            sr = 1 << (b - 7)
            moved = pltpu.roll(v, n_rows - sr, axis=0)
        take = ((moved >> VALID_BIT) & 1 == 1) & ((moved >> b) & 1 == 1)
        stay = ((v >> VALID_BIT) & 1 == 1) & ((v >> b) & 1 == 0)
        v = jnp.where(take, moved, jnp.where(stay, v, 0))
    idx_ref[0] = (pos + (v & ((1 << DISP_BITS) - 1)))[:cap_rows]


def route(aff_t):
    n_exp, n = aff_t.shape
    assert n <= (1 << DISP_BITS)
    cap = n // CAPACITY_DIV
    n_rows = n // LANES
    cap_rows = cap // LANES
    thr, need = pl.pallas_call(
        functools.partial(_threshold_kernel, cap=cap),
        out_shape=(jax.ShapeDtypeStruct((n_exp, LANES), I32), jax.ShapeDtypeStruct((n_exp, LANES), I32)),
        name="route_threshold",
    )(aff_t)
    grid_spec = pltpu.PrefetchScalarGridSpec(
        num_scalar_prefetch=2,
        grid=(n_exp,),
        in_specs=[pl.BlockSpec((1, n_rows, LANES), lambda e, *_: (e, 0, 0))],
        out_specs=[
            pl.BlockSpec((1, cap_rows, LANES), lambda e, *_: (e, 0, 0)),
            pl.BlockSpec((1, n_rows, LANES), lambda e, *_: (e, 0, 0)),
        ],
    )
    idx, rank = pl.pallas_call(
        functools.partial(_compact_kernel, n_rows=n_rows, cap_rows=cap_rows),
        grid_spec=grid_spec,
        out_shape=(
            jax.ShapeDtypeStruct((n_exp, cap_rows, LANES), I32),
            jax.ShapeDtypeStruct((n_exp, n_rows, LANES), I32),
        ),
        compiler_params=pltpu.CompilerParams(dimension_semantics=("parallel",)),
        name="route_compact",
    )(thr[:, 0], need[:, 0], aff_t.reshape(n_exp, n_rows, LANES))
    return idx.reshape(n_exp, cap), rank.reshape(n_exp, n)


D_MODEL = 1024
CONV_W = 512
N_HEADS = 4
V_DIM = 128
QK_DIM = 64
QK_W = N_HEADS * QK_DIM
V_W = N_HEADS * V_DIM
CHUNK = 128
EXPERT_FF = 1024
EPS = 1e-6
C_GB, C_GC, C_U, C_Q, C_K, C_V, C_O, C_G = 0, 512, 1024, 1536, 1792, 2048, 2560, 3072
D_IN = 3088
D_IN_PAD = 3200
TOK_EXT = D_MODEL + LANES

ROW_TILE = 512
INPROJ_TILE = 1024
MLSTM_BLOCK = 256
SLOT_TILE = 512
COMBINE_TILE = 256
VMEM_LIMIT = 56 * 1024 * 1024


def _cparams(*sem):
    return pltpu.CompilerParams(dimension_semantics=sem, vmem_limit_bytes=VMEM_LIMIT)


def _rms(x, g):
    return x * lax.rsqrt(jnp.mean(x * x, axis=-1, keepdims=True) + EPS) * g


GATE_GROUP = 8
GATE_ROWS = 6 * GATE_GROUP
G_A, G_PM, G_B = 0, 1, 2


def _chunk_scan(x, op, identity, reverse):
    width = x.shape[1]
    pos = lax.broadcasted_iota(I32, x.shape, 1) & (CHUNK - 1)
    for b in range(7):
        s = 1 << b
        if reverse:
            shifted = jnp.where(pos < CHUNK - s, pltpu.roll(x, width - s, axis=1), identity)
        else:
            shifted = jnp.where(pos >= s, pltpu.roll(x, s, axis=1), identity)
        x = op(x, shifted)
    return x


def _inproj_kernel(x_ref, g_ref, w_ref, bias_ref, gb_ref, gcu_ref, q_ref, k_ref, v_ref, os_ref, grow_ref):
    hn = _rms(x_ref[...], g_ref[...]).astype(BF16)

    def seg(a, b):
        return jnp.dot(hn, w_ref[:, a:b], preferred_element_type=F32)

    gates = seg(C_G, D_IN_PAD) + bias_ref[...]
    lane = lax.broadcasted_iota(I32, gates.shape, 1)
    log_sig = jnp.minimum(gates, 0.0) - jnp.log1p(jnp.exp(-jnp.abs(gates)))
    g16 = jnp.where((lane // N_HEADS) % 2 == 1, log_sig, gates).T[:4 * N_HEADS]
    fwd = lax.broadcasted_iota(I32, g16.shape, 0) < 2 * N_HEADS
    cs = jnp.where(fwd, _chunk_scan(g16, jnp.add, 0.0, False), _chunk_scan(g16, jnp.add, 0.0, True))
    cs = pltpu.roll(cs, 3 * N_HEADS, axis=0)
    a = g16 - cs
    neg_inf = jnp.float32(-jnp.inf)
    pm = jnp.where(fwd, _chunk_scan(a, jnp.maximum, neg_inf, False), _chunk_scan(a, jnp.maximum, neg_inf, True))
    for d in range(2):
        grp = slice(d * GATE_GROUP, (d + 1) * GATE_GROUP)
        for g, val in ((G_A, a), (G_PM, pm), (G_B, cs)):
            r0 = (3 * d + g) * GATE_GROUP
            grow_ref[r0:r0 + GATE_GROUP, :] = val[grp]

    gb_ref[...] = seg(C_GB, C_GC).astype(BF16)
    gcu_ref[...] = (seg(C_GC, C_U) * seg(C_U, C_Q)).astype(BF16)
    q_ref[...] = (seg(C_Q, C_K) * (QK_DIM ** -0.5)).astype(BF16)
    k_ref[...] = seg(C_K, C_V).astype(BF16)
    v_ref[...] = seg(C_V, C_O).astype(BF16)
    os_ref[...] = jax.nn.sigmoid(seg(C_O, C_G)).astype(BF16)


def inproj(x, g1, w_in_p, bias_p):
    n = x.shape[0]
    tm = INPROJ_TILE
    row = lambda w: pl.BlockSpec((tm, w), lambda i: (i, 0))
    full = lambda a: pl.BlockSpec(a.shape, lambda i: (0,) * a.ndim)
    return pl.pallas_call(
        _inproj_kernel,
        grid=(n // tm,),
        in_specs=[row(D_MODEL), full(g1), full(w_in_p), full(bias_p)],
        out_specs=[row(CONV_W), row(CONV_W), row(QK_W), row(QK_W), row(V_W), row(V_W),
                   pl.BlockSpec((GATE_ROWS, tm), lambda i: (0, i))],
        out_shape=[
            jax.ShapeDtypeStruct((n, CONV_W), BF16), jax.ShapeDtypeStruct((n, CONV_W), BF16),
            jax.ShapeDtypeStruct((n, QK_W), BF16), jax.ShapeDtypeStruct((n, QK_W), BF16),
            jax.ShapeDtypeStruct((n, V_W), BF16), jax.ShapeDtypeStruct((n, V_W), BF16),
            jax.ShapeDtypeStruct((GATE_ROWS, n), F32),
        ],
        compiler_params=_cparams("parallel"),
        name="mixer_inproj",
    )(x, g1, w_in_p, bias_p)


def _mlstm_kernel(qf_ref, kf_ref, vf_ref, gf_ref, qb_ref, kb_ref, vb_ref, gb_ref, hf_ref, hb_ref,
                  cf_ref, mf_ref, cb_ref, mb_ref, *, seq_len):
    j = pl.program_id(0)

    @pl.when(((j * MLSTM_BLOCK) % seq_len) == 0)
    def _():
        for ref in (cf_ref, mf_ref, cb_ref, mb_ref):
            ref[...] = jnp.zeros_like(ref)

    terms_f = _mlstm_step_terms(gf_ref, mf_ref, reverse=False)
    terms_b = _mlstm_step_terms(gb_ref, mb_ref, reverse=True)
    static_f = _mlstm_static(qf_ref, kf_ref, terms_f, reverse=False)
    static_b = _mlstm_static(qb_ref, kb_ref, terms_b, reverse=True)
    _mlstm_recurrence(qf_ref, vf_ref, hf_ref, cf_ref, terms_f, static_f, reverse=False)
    _mlstm_recurrence(qb_ref, vb_ref, hb_ref, cb_ref, terms_b, static_b, reverse=True)


def _chunk_order(reverse):
    n_chunks = MLSTM_BLOCK // CHUNK
    return list(range(n_chunks - 1, -1, -1) if reverse else range(n_chunks))


def _mlstm_step_terms(grow_ref, m_ref, *, reverse):
    d = 3 if reverse else 0

    def gate_rows(g, cols):
        r0 = (d + g) * GATE_GROUP
        return grow_ref[r0:r0 + N_HEADS, cols]

    order = _chunk_order(reverse)
    end = 0 if reverse else CHUNK - 1
    m_old = m_ref[0:N_HEADS, :]
    m_in = {}
    for c in order:
        last = slice(c * CHUNK + end, c * CHUNK + end + 1)
        m_in[c] = m_old
        m_old = gate_rows(G_B, last) + jnp.maximum(m_old, gate_rows(G_PM, last))
    m_ref[0:N_HEADS, :] = m_old

    terms = {}
    for c in order:
        r0 = c * CHUNK
        rows = slice(r0, r0 + CHUNK)
        m_o = m_in[c]
        a = gate_rows(G_A, rows)
        mm = jnp.maximum(m_o, gate_rows(G_PM, rows))
        mm_last = jnp.maximum(m_o, gate_rows(G_PM, slice(r0 + end, r0 + end + 1)))
        sc = jnp.exp(m_o - mm)
        emt = jnp.exp(-(gate_rows(G_B, rows) + mm))
        w = jnp.exp(a - mm_last)
        decay = jnp.exp(m_o - mm_last)
        stack = jnp.concatenate([mm, sc, emt, jnp.zeros((LANES - 3 * N_HEADS, CHUNK), F32)], axis=0)
        terms[c] = (a, w, decay, stack.T)
    return terms


def _mlstm_static(q_ref, k_ref, terms, *, reverse):
    t_i = lax.broadcasted_iota(I32, (CHUNK, CHUNK), 0)
    s_i = lax.broadcasted_iota(I32, (CHUNK, CHUNK), 1)
    tri = (s_i >= t_i) if reverse else (s_i <= t_i)
    lane = lax.broadcasted_iota(I32, (CHUNK, LANES), 1)
    half_masks = [jnp.where(lane // QK_DIM == hh, 1.0, 0.0).astype(BF16) for hh in range(2)]
    s_mats, kmts = {}, {}
    for c in _chunk_order(reverse):
        rows = slice(c * CHUNK, (c + 1) * CHUNK)
        a, w, _, cols = terms[c]
        for h in range(N_HEADS):
            pair = slice((h // 2) * LANES, (h // 2 + 1) * LANES)
            km = k_ref[rows, pair] * half_masks[h % 2]
            kmts[c, h] = (km.astype(F32).T * w[h:h + 1, :]).astype(BF16)
            dmat = jnp.where(tri, jnp.exp(a[h:h + 1, :] - cols[:, h:h + 1]), 0.0)
            s_mats[c, h] = lax.dot_general(q_ref[rows, pair], km, (((1,), (1,)), ((), ())),
                                           preferred_element_type=F32) * dmat
    return s_mats, kmts


def _mlstm_recurrence(q_ref, v_ref, out_ref, c_ref, terms, static, *, reverse):
    s_mats, kmts = static
    lane = lax.broadcasted_iota(I32, (CHUNK, LANES), 1)
    ones_col = jnp.where(lane == 0, 1.0, 0.0).astype(BF16)
    c_state = [c_ref[h] for h in range(N_HEADS)]
    for c in _chunk_order(reverse):
        rows = slice(c * CHUNK, (c + 1) * CHUNK)
        _, _, decay, cols = terms[c]
        for h in range(N_HEADS):
            pair = slice((h // 2) * LANES, (h // 2 + 1) * LANES)
            hv = slice(h * V_DIM, (h + 1) * V_DIM)
            vh = v_ref[rows, hv]
            c_old = c_state[h]
            sc_col = cols[:, N_HEADS + h:N_HEADS + h + 1]
            emt_col = cols[:, 2 * N_HEADS + h:2 * N_HEADS + h + 1]
            s_mat = s_mats[c, h]
            qc = jnp.dot(q_ref[rows, pair], c_old.astype(BF16), preferred_element_type=F32)
            num = jnp.dot(s_mat.astype(BF16), vh, preferred_element_type=F32) + sc_col * qc[:, :V_DIM]
            den = jnp.sum(s_mat, axis=1, keepdims=True) + sc_col * qc[:, V_DIM:V_DIM + 1]
            h_out = num * (1.0 / jnp.maximum(jnp.abs(den), emt_col))
            kv = jnp.dot(kmts[c, h], jnp.concatenate([vh, ones_col], axis=1), preferred_element_type=F32)
            c_state[h] = decay[h:h + 1, 0:1] * c_old + kv
            out_ref[rows, hv] = h_out.astype(BF16)
    for h in range(N_HEADS):
        c_ref[h] = c_state[h]


def mlstm(q, k, v, grow, seq_len):
    n = q.shape[0]
    blk = MLSTM_BLOCK
    nb = n // blk
    assert n % seq_len == 0 and seq_len % blk == 0
    fwd = lambda w: pl.BlockSpec((blk, w), lambda j: (j, 0))
    bwd = lambda w: pl.BlockSpec((blk, w), lambda j: (nb - 1 - j, 0))
    state = [pltpu.VMEM((N_HEADS, LANES, 2 * LANES), F32), pltpu.VMEM((8, LANES), F32)]
    return pl.pallas_call(
        functools.partial(_mlstm_kernel, seq_len=seq_len),
        grid=(nb,),
        in_specs=[fwd(QK_W), fwd(QK_W), fwd(V_W), pl.BlockSpec((GATE_ROWS, blk), lambda j: (0, j)),
                  bwd(QK_W), bwd(QK_W), bwd(V_W), pl.BlockSpec((GATE_ROWS, blk), lambda j: (0, nb - 1 - j))],
        out_specs=[fwd(V_W), bwd(V_W)],
        out_shape=[jax.ShapeDtypeStruct((n, V_W), BF16), jax.ShapeDtypeStruct((n, V_W), BF16)],
        scratch_shapes=state + state,
        compiler_params=_cparams("arbitrary"),
        name="mlstm",
    )(q, k, v, grow, q, k, v, grow)


HALO = 16


def _outproj_kernel(gb_ref, gcu_ref, gprev_ref, gnext_ref, hf_ref, hb_ref, os_ref, hng_ref, x_ref, cw_ref, wo_ref,
                    g2_ref, wr_ref, x1e_ref, afft_ref, *, seq_len):
    i = pl.program_id(0)
    tm = ROW_TILE
    first = ((i * tm) % seq_len) == 0
    last = (((i + 1) * tm) % seq_len) == 0
    g = gcu_ref[...].astype(F32)
    prev_row = jnp.where(first, 0.0, gprev_ref[HALO - 1:HALO, :].astype(F32))
    next_row = jnp.where(last, 0.0, gnext_ref[0:1, :].astype(F32))
    rid = lax.broadcasted_iota(I32, g.shape, 0)
    dn = jnp.where(rid == 0, prev_row, pltpu.roll(g, 1, axis=0))
    up = jnp.where(rid == tm - 1, next_row, pltpu.roll(g, tm - 1, axis=0))
    conv = dn * cw_ref[0:1, :] + g * cw_ref[1:2, :] + up * cw_ref[2:3, :]
    co = (gb_ref[...].astype(F32) * conv).astype(BF16)
    y = jnp.dot(co, wo_ref[:CONV_W, :], preferred_element_type=F32)
    heads = []
    for h in range(N_HEADS):
        hv = slice(h * V_DIM, (h + 1) * V_DIM)
        ht = _rms(hf_ref[:, hv].astype(F32) + hb_ref[:, hv].astype(F32), hng_ref[:, hv])
        heads.append((os_ref[:, hv].astype(F32) * ht).astype(BF16))
    mo = jnp.concatenate(heads, axis=1)
    y = y + jnp.dot(mo, wo_ref[CONV_W:, :], preferred_element_type=F32)
    x1 = x_ref[...] + y
    tokens = _rms(x1, g2_ref[...])
    t_hi = tokens.astype(BF16)
    t_lo = (tokens - t_hi.astype(F32)).astype(BF16)
    p_hi = jnp.dot(t_hi, wr_ref[...], preferred_element_type=F32)
    logits = p_hi[:, :LANES] + p_hi[:, LANES:] + jnp.dot(t_lo, wr_ref[:, :LANES], preferred_element_type=F32)
    lane = lax.broadcasted_iota(I32, logits.shape, 1)
    logits = jnp.where(lane < N_EXPERTS, logits, -jnp.inf)
    ex = jnp.exp(logits - jnp.max(logits, axis=-1, keepdims=True))
    aff = ex / jnp.sum(ex, axis=-1, keepdims=True)
    x1e_ref[:, :D_MODEL] = x1
    row_id = (i * tm + lax.broadcasted_iota(I32, aff.shape, 0)).astype(F32)
    x1e_ref[:, D_MODEL:] = jnp.where(lane == TOKEN_ID_LANE, row_id, aff)
    afft_ref[...] = aff.T[:N_EXPERTS]


def outproj(gb, gcu, hf, hb, osig, hng, x, cw_p, w_out_b, g2, wr_p, seq_len):
    n = x.shape[0]
    tm = ROW_TILE
    halos = tm // HALO
    n_halo = n // HALO
    row = lambda w: pl.BlockSpec((tm, w), lambda i: (i, 0))
    full = lambda a: pl.BlockSpec(a.shape, lambda i: (0,) * a.ndim)
    prev = pl.BlockSpec((HALO, CONV_W), lambda i: (jnp.maximum(i * halos - 1, 0), 0))
    nxt = pl.BlockSpec((HALO, CONV_W), lambda i: (jnp.minimum((i + 1) * halos, n_halo - 1), 0))
    return pl.pallas_call(
        functools.partial(_outproj_kernel, seq_len=seq_len),
        grid=(n // tm,),
        in_specs=[row(CONV_W), row(CONV_W), prev, nxt, row(V_W), row(V_W), row(V_W), full(hng), row(D_MODEL),
                  full(cw_p), full(w_out_b), full(g2), full(wr_p)],
        out_specs=[row(TOK_EXT), pl.BlockSpec((N_EXPERTS, tm), lambda i: (0, i))],
        out_shape=[jax.ShapeDtypeStruct((n, TOK_EXT), F32), jax.ShapeDtypeStruct((N_EXPERTS, n), F32)],
        compiler_params=_cparams("parallel"),
        name="mixer_outproj_router",
    )(gb, gcu, gcu, gcu, hf, hb, osig, hng, x, cw_p, w_out_b, g2, wr_p)


def _ffn_kernel(idx_ref, tok_hbm, g2_ref, wg_ref, wu_ref, wd_ref, ye_ref, xg_ref, wgb_ref, wub_ref, wdb_ref, sem):
    e = pl.program_id(0)
    s = pl.program_id(1)
    n_steps = pl.num_programs(1)
    ts = SLOT_TILE
    t = e * n_steps + s
    last = N_EXPERTS * n_steps - 1

    def row_copy(tile, buf, i):
        tok = idx_ref[tile * ts + i]
        return pltpu.make_async_copy(tok_hbm.at[pl.ds(tok, 1)], xg_ref.at[buf, pl.ds(i, 1)], sem.at[buf])

    def wait_rows(buf):
        pltpu.make_async_copy(tok_hbm.at[pl.ds(0, ts)], xg_ref.at[buf], sem.at[buf]).wait()

    @pl.when(t == 0)
    def _():
        def issue(i, carry):
            row_copy(0, 0, i).start()
            return carry

        lax.fori_loop(0, ts, issue, 0, unroll=8)

    @pl.when(s == 0)
    def _():
        wgb_ref[...] = wg_ref[0, 0].astype(BF16)
        wub_ref[...] = wu_ref[0, 0].astype(BF16)
        wdb_ref[...] = wd_ref[0, 0].astype(BF16)

    def half(buf, next_tile, rows):
        wait_rows(buf)
        x = xg_ref[buf]
        xb = _rms(x[:, :D_MODEL], g2_ref[...]).astype(BF16)
        ext = x[:, D_MODEL:]
        lane = lax.broadcasted_iota(I32, ext.shape, 1)
        gate = jnp.sum(jnp.where(lane == e, ext, 0.0), axis=1, keepdims=True)
        for i in range(ts):
            row_copy(next_tile, 1 - buf, i).start(priority=i % 2)
        hg = jnp.dot(xb, wgb_ref[...], preferred_element_type=F32)
        hu = jnp.dot(xb, wub_ref[...], preferred_element_type=F32)
        hid = (hg * jax.nn.sigmoid(hg) * hu).astype(BF16)
        ye_ref[rows, :D_MODEL] = jnp.dot(hid, wdb_ref[...], preferred_element_type=F32) * gate
        ye_ref[rows, D_MODEL:] = ext

    half(0, 2 * t + 1, slice(0, ts))
    half(1, jnp.minimum(2 * t + 2, 2 * last), slice(ts, 2 * ts))

    @pl.when(t == last)
    def _():
        wait_rows(0)


def expert_ffn(idx_flat, x1e, g2, wg, wu, wd, cap, layer):
    assert cap % (2 * SLOT_TILE) == 0
    n_steps = cap // (2 * SLOT_TILE)
    wspec = lambda: pl.BlockSpec((1, 1, D_MODEL, EXPERT_FF), lambda e, s, *_: (layer, e, 0, 0))
    grid_spec = pltpu.PrefetchScalarGridSpec(
        num_scalar_prefetch=1,
        grid=(N_EXPERTS, n_steps),
        in_specs=[pl.BlockSpec(memory_space=pl.ANY), pl.BlockSpec((1, D_MODEL), lambda e, s, *_: (0, 0)), wspec(), wspec(),
                  pl.BlockSpec((1, 1, EXPERT_FF, D_MODEL), lambda e, s, *_: (layer, e, 0, 0))],
        out_specs=pl.BlockSpec((2 * SLOT_TILE, TOK_EXT), lambda e, s, *_: (e * n_steps + s, 0)),
        scratch_shapes=[pltpu.VMEM((2, SLOT_TILE, TOK_EXT), F32),
                        pltpu.VMEM((D_MODEL, EXPERT_FF), BF16), pltpu.VMEM((D_MODEL, EXPERT_FF), BF16),
                        pltpu.VMEM((EXPERT_FF, D_MODEL), BF16), pltpu.SemaphoreType.DMA((2,))],
    )
    return pl.pallas_call(
        _ffn_kernel,
        grid_spec=grid_spec,
        out_shape=jax.ShapeDtypeStruct((N_EXPERTS * cap, TOK_EXT), F32),
        compiler_params=_cparams("arbitrary", "arbitrary"),
        name="expert_ffn",
    )(idx_flat, x1e, g2, wg, wu, wd)


WIN = 8
WIN_SHIFT = WIN.bit_length() - 1
STACK_TILE = 256


def _stack_rows(tc):
    rows = N_EXPERTS * (tc + 2 * (WIN - 1))
    return -(-rows // STACK_TILE) * STACK_TILE


def _combine_kernel(off_ref, x1_ref, ye_hbm, *rest, cap, n_blocks, final):
    if final:
        fg_ref, out_ref, ys_ref, sem = rest
    else:
        out_ref, ys_ref, sem = rest
    j = pl.program_id(0)
    tc = COMBINE_TILE
    buf = j % 2

    def windows(tile, e):
        lo = off_ref[e * (n_blocks + 1) + tile]
        hi = off_ref[e * (n_blocks + 1) + tile + 1]
        start = (lo >> WIN_SHIFT) << WIN_SHIFT
        return start, jnp.where(hi > lo, (hi - start + (WIN - 1)) >> WIN_SHIFT, 0)

    def stacked_rows(tile):
        total = jnp.int32(0)
        for e in range(N_EXPERTS):
            total = total + windows(tile, e)[1] * WIN
        return total

    def fetch(tile, buf_):
        base = jnp.int32(0)
        for e in range(N_EXPERTS):
            start, n_win = windows(tile, e)

            def issue(w, carry, e=e, start=start, base=base):
                src = pl.multiple_of(e * cap + start + w * WIN, WIN)
                dst = pl.multiple_of(base + w * WIN, WIN)
                pltpu.make_async_copy(ye_hbm.at[pl.ds(src, WIN)], ys_ref.at[buf_, pl.ds(dst, WIN)], sem.at[buf_]).start()
                return carry

            lax.fori_loop(0, n_win, issue, 0)
            base = base + n_win * WIN

    @pl.when(j == 0)
    def _():
        ys_ref[...] = jnp.zeros_like(ys_ref)
        fetch(0, 0)

    base = stacked_rows(j)
    n_total = base >> WIN_SHIFT
    for bit in range((_stack_rows(tc) // WIN).bit_length()):
        @pl.when((n_total & (1 << bit)) != 0)
        def _(bit=bit):
            rows = WIN << bit
            pltpu.make_async_copy(ye_hbm.at[pl.ds(0, rows)], ye_hbm.at[pl.ds(0, rows)], sem.at[buf]).wait()

    @pl.when(j + 1 < n_blocks)
    def _():
        fetch(j + 1, 1 - buf)

    tok0 = (j * tc).astype(F32)
    lane_t = lax.broadcasted_iota(I32, (STACK_TILE, tc), 1).astype(F32)
    row_i = lax.broadcasted_iota(I32, (STACK_TILE, 1), 0)

    def accumulate(kt, acc):
        r0 = pl.multiple_of(kt * STACK_TILE, STACK_TILE)
        rows = ys_ref[buf, pl.ds(r0, STACK_TILE), :]
        tok_local = rows[:, D_MODEL + TOKEN_ID_LANE:D_MODEL + TOKEN_ID_LANE + 1] - tok0
        tok_local = jnp.where(r0 + row_i < base, tok_local, -1.0)
        onehot = jnp.where(tok_local == lane_t, 1.0, 0.0).astype(BF16)
        out_ref[...] += lax.dot_general(onehot, rows[:, :D_MODEL].astype(BF16), (((0,), (0,)), ((), ())),
                                        preferred_element_type=F32)
        return acc

    n_kt = (base + (STACK_TILE - 1)) // STACK_TILE
    out_ref[...] = x1_ref[...]
    lax.fori_loop(0, n_kt, accumulate, 0)
    if final:
        out_ref[...] = _rms(out_ref[...], fg_ref[...])


def combine(off_flat, x1e, ye, cap, final_g=None):
    n = x1e.shape[0]
    tc = COMBINE_TILE
    nb = n // tc
    final = final_g is not None
    in_specs = [pl.BlockSpec((tc, D_MODEL), lambda j, *_: (j, 0)),
                pl.BlockSpec(memory_space=pl.ANY)]
    args = [x1e, ye]
    if final:
        in_specs.append(pl.BlockSpec((1, D_MODEL), lambda j, *_: (0, 0)))
        args.append(final_g)
    grid_spec = pltpu.PrefetchScalarGridSpec(
        num_scalar_prefetch=1,
        grid=(nb,),
        in_specs=in_specs,
        out_specs=pl.BlockSpec((tc, D_MODEL), lambda j, *_: (j, 0)),
        scratch_shapes=[pltpu.VMEM((2, _stack_rows(tc), TOK_EXT), F32), pltpu.SemaphoreType.DMA((2,))],
    )
    return pl.pallas_call(
        functools.partial(_combine_kernel, cap=cap, n_blocks=nb, final=final),
        grid_spec=grid_spec,
        out_shape=jax.ShapeDtypeStruct((n, D_MODEL), F32),
        compiler_params=_cparams("arbitrary"),
        name="moe_combine",
    )(off_flat, *args)


def moe(x1e, aff_t, g2, wg, wu, wd, layer, final_g=None):
    n = x1e.shape[0]
    cap = n // CAPACITY_DIV
    idx, rank = route(aff_t)
    off = jnp.concatenate([rank[:, ::COMBINE_TILE], jnp.full((N_EXPERTS, 1), cap, I32)], axis=1)
    ye = expert_ffn(idx.reshape(-1), x1e, g2, wg, wu, wd, cap, layer)
    return combine(off.reshape(-1), x1e, ye, cap, final_g)


def kernel(x_prompt, x_sample, norm1_g, w_in, conv_w, gate_bias, head_norm_g, w_out, norm2_g, w_router, w_gate, w_up, w_down, final_g):
    depth = w_in.shape[0]
    w_in_p = jnp.pad(w_in, ((0, 0), (0, 0), (0, D_IN_PAD - D_IN))).astype(BF16)
    bias_p = jnp.pad(gate_bias.reshape(depth, 1, 4 * N_HEADS), ((0, 0), (0, 0), (0, LANES - 4 * N_HEADS)))
    cw_p = jnp.pad(conv_w, ((0, 0), (0, 8 - conv_w.shape[1]), (0, 0)))
    wr_f = jnp.pad(w_router, ((0, 0), (0, 0), (0, LANES - N_EXPERTS)))
    wr_hi = wr_f.astype(BF16)
    wr_p = jnp.concatenate([wr_hi, (wr_f - wr_hi.astype(F32)).astype(BF16)], axis=-1)
    w_out_b = w_out.astype(BF16)
    fg = final_g.reshape(1, D_MODEL)

    outs = []
    for x in (x_prompt, x_sample):
        bsz, seq, _ = x.shape
        xf = x.reshape(bsz * seq, D_MODEL)
        for l in range(depth):
            gb, gcu, q, k, v, osig, grow = inproj(xf, norm1_g[l].reshape(1, -1), w_in_p[l], bias_p[l])
            hf, hb = mlstm(q, k, v, grow, seq)
            g2 = norm2_g[l].reshape(1, -1)
            x1e, aff_t = outproj(gb, gcu, hf, hb, osig, head_norm_g[l].reshape(1, -1), xf, cw_p[l], w_out_b[l],
                                 g2, wr_p[l], seq)
            xf = moe(x1e, aff_t, g2, w_gate, w_up, w_down, l, fg if l == depth - 1 else None)
        outs.append(xf.reshape(bsz, seq, D_MODEL))
    return tuple(outs)
```
